```python
import jax
import jax.numpy as jnp
from jax import lax
import numpy as np

D_MODEL = 1024
BATCH = 8
SEQ = 8192
DEPTH = 2
DEC_BATCH = 4
DEC_SEQ = 4096
PAST_LEN = 128

HEAD_DIM = 64
H_MLSTM = (3 * D_MODEL) // (8 * HEAD_DIM)
H_RET = (3 * D_MODEL) // (8 * HEAD_DIM)
H_NA = D_MODEL // HEAD_DIM - H_MLSTM - H_RET
W_MLSTM = H_MLSTM * HEAD_DIM
W_RET = H_RET * HEAD_DIM
W_NA = H_NA * HEAD_DIM
IN_SIZES = (W_MLSTM, W_MLSTM, W_MLSTM, W_MLSTM, 4 * H_MLSTM,
            W_RET, W_RET, W_RET, W_RET,
            W_NA, W_NA, W_NA)
IN_WIDTH = sum(IN_SIZES)
CHUNK = 128
CONV_W = 3
ROPE_BASE = 10000.0
GRID_W = 64
NA_ROWS_MAX = 8
NA_COLS = 16
NA_QCOLS = 16
NA_KCOLS = 2 * NA_COLS
N_GROUPS = 4
EXPERTS_PER_GROUP = 8
N_EXPERTS = N_GROUPS * EXPERTS_PER_GROUP
TOP_K = 2
EXPERT_FF = D_MODEL // 2
MOE_BLOCK = 128
RMS_EPS = 1e-6
F32 = jnp.float32

kernel_name = "hymba_mlstm_retnet_natten_hmoe_encoder"


def rmsnorm(x, g):
    xf = x.astype(F32)
    xf = xf * lax.rsqrt(jnp.mean(xf * xf, axis=-1, keepdims=True) + RMS_EPS)
    return (xf * g.astype(F32)).astype(x.dtype)


def to_heads(t, n_heads):
    b, s, _ = t.shape
    return t.reshape(b, s, n_heads, HEAD_DIM).transpose(0, 2, 1, 3).astype(F32)


def from_heads(t):
    b, h, s, d = t.shape
    return t.transpose(0, 2, 1, 3).reshape(b, s, h * d)


def head_rmsnorm(h, g, dtype):
    h = h * lax.rsqrt(jnp.mean(h * h, axis=-1, keepdims=True) + RMS_EPS)
    return (from_heads(h) * g.astype(F32)).astype(dtype)


def flip_t(a):
    return jnp.flip(a, axis=2)


def to_chunks(a):
    return a.reshape(a.shape[:2] + (a.shape[2] // CHUNK, CHUNK) + a.shape[3:])


def centred_dwconv(x, w, b):
    c = x.shape[-1]
    y = lax.conv_general_dilated(x, w[:, None, :], window_strides=(1,),
                                 padding=[(CONV_W // 2, CONV_W // 2)],
                                 dimension_numbers=("NWC", "WIO", "NWC"),
                                 feature_group_count=c)
    return y + b


def rotary(x):
    _, _, s, d = x.shape
    half = d // 2
    inv_freq = ROPE_BASE ** (-jnp.arange(half, dtype=F32) / half)
    ang = jnp.arange(s, dtype=F32)[:, None] * inv_freq[None, :]
    cos, sin = jnp.cos(ang), jnp.sin(ang)
    x1, x2 = x[..., :half], x[..., half:]
    return jnp.concatenate([x1 * cos - x2 * sin, x1 * sin + x2 * cos], axis=-1)


def mlstm_chunkwise(q, k, v, log_i, log_f):
    b, h, t, d = q.shape
    q, k, v = to_chunks(q), to_chunks(k) * (d ** -0.5), to_chunks(v)
    li, lf = to_chunks(log_i), to_chunks(log_f)
    a = jnp.cumsum(lf, axis=-1)
    a_tot = a[..., -1]
    w_loc = a_tot[..., None] - a + li
    m_loc = jnp.max(w_loc, axis=-1)
    e_loc = jnp.exp(w_loc - m_loc[..., None])
    kv = jnp.einsum('bhcld,bhcle->bhcde', k * e_loc[..., None], v)
    ksum = jnp.einsum('bhcl,bhcld->bhcd', e_loc, k)

    def step(carry, xs):
        c_st, n_st, m_st = carry
        kv_c, ks_c, at_c, ml_c = xs
        m_new = jnp.maximum(at_c + m_st, ml_c)
        s_old = jnp.exp(at_c + m_st - m_new)
        s_new = jnp.exp(ml_c - m_new)
        c_new = s_old[..., None, None] * c_st + s_new[..., None, None] * kv_c
        n_new = s_old[..., None] * n_st + s_new[..., None] * ks_c
        return (c_new, n_new, m_new), (c_st, n_st, m_st)

    init = (jnp.zeros((b, h, d, d), F32), jnp.zeros((b, h, d), F32), jnp.zeros((b, h), F32))
    xs = (jnp.moveaxis(kv, 2, 0), jnp.moveaxis(ksum, 2, 0),
          jnp.moveaxis(a_tot, 2, 0), jnp.moveaxis(m_loc, 2, 0))
    _, (c_prev, n_prev, m_prev) = lax.scan(step, init, xs)
    c_prev = jnp.moveaxis(c_prev, 0, 2)
    n_prev = jnp.moveaxis(n_prev, 0, 2)
    m_prev = jnp.moveaxis(m_prev, 0, 2)

    lower = jnp.tril(jnp.ones((CHUNK, CHUNK), dtype=bool))
    d_log = jnp.where(lower, a[..., :, None] - a[..., None, :] + li[..., None, :], -jnp.inf)
    g_inter = a + m_prev[..., None]
    m_row = jnp.maximum(jnp.max(d_log, axis=-1), g_inter)
    p = jnp.einsum('bhcid,bhcjd->bhcij', q, k) * jnp.exp(d_log - m_row[..., None])
    s_inter = jnp.exp(g_inter - m_row)
    num = jnp.einsum('bhcij,bhcjd->bhcid', p, v) + s_inter[..., None] * jnp.einsum('bhcid,bhcde->bhcie', q, c_prev)
    den = jnp.sum(p, axis=-1) + s_inter * jnp.einsum('bhcid,bhcd->bhci', q, n_prev)
    out = num / jnp.maximum(jnp.abs(den), jnp.exp(-m_row))[..., None]
    return out.reshape(b, h, t, d)


def retention_chunkwise(q, k, v, log_gamma):
    b, h, t, d = q.shape
    q, k, v = to_chunks(q), to_chunks(k) * (d ** -0.5), to_chunks(v)
    pos = jnp.arange(CHUNK, dtype=F32)
    diff = pos[:, None] - pos[None, :]
    lg = log_gamma[:, None, None]
    decay = jnp.where(diff >= 0, jnp.exp(lg * jnp.maximum(diff, 0.0)), 0.0)
    scores = jnp.einsum('bhcid,bhcjd->bhcij', q, k) * decay[None, :, None]
    y = jnp.einsum('bhcij,bhcjd->bhcid', scores, v)
    k_dec = k * jnp.exp(log_gamma[:, None] * (CHUNK - 1 - pos))[None, :, None, :, None]
    kv = jnp.einsum('bhcld,bhcle->bhcde', k_dec, v)
    chunk_decay = jnp.exp(log_gamma * CHUNK)[None, :, None, None]

    def step(s_st, kv_c):
        return chunk_decay * s_st + kv_c, s_st

    _, s_prev = lax.scan(step, jnp.zeros((b, h, d, d), F32), jnp.moveaxis(kv, 2, 0))
    s_prev = jnp.moveaxis(s_prev, 0, 2)
    q_dec = q * jnp.exp(log_gamma[:, None] * (pos + 1.0))[None, :, None, :, None]
    y = y + jnp.einsum('bhcid,bhcde->bhcie', q_dec, s_prev)
    return y.reshape(b, h, t, d)


def neighbourhood_attention(q, k, v, rpb):
    b, h, t, d = q.shape
    rows = t // GRID_W
    kr = min(NA_ROWS_MAX, rows)
    ncb = GRID_W // NA_QCOLS
    qcol = np.arange(GRID_W).reshape(ncb, NA_QCOLS)
    cb_start = np.clip(np.arange(ncb) * NA_QCOLS - NA_COLS // 2, 0, GRID_W - NA_KCOLS)
    key_cols = cb_start[:, None] + np.arange(NA_KCOLS)
    win_start = np.clip(qcol - NA_COLS // 2, 0, GRID_W - NA_COLS)
    kc = key_cols[:, None, :]
    col_ok = (kc >= win_start[..., None]) & (kc < win_start[..., None] + NA_COLS)
    col_idx = np.clip(kc - qcol[..., None] + NA_COLS - 1, 0, 2 * NA_COLS - 2)
    mask = np.broadcast_to(col_ok[:, :, None, :], (ncb, NA_QCOLS, kr, NA_KCOLS)).reshape(ncb, NA_QCOLS, kr * NA_KCOLS)
    q5 = q.reshape(b, h, rows, GRID_W, d) * (d ** -0.5)
    k5 = k.reshape(b, h, rows, GRID_W, d)
    v5 = v.reshape(b, h, rows, GRID_W, d)

    def row_block(r):
        start = jnp.clip(r - kr // 2, 0, rows - kr)
        kw = lax.dynamic_slice_in_dim(k5, start, kr, axis=2)[:, :, :, key_cols]
        vw = lax.dynamic_slice_in_dim(v5, start, kr, axis=2)[:, :, :, key_cols]
        kw = kw.transpose(0, 1, 3, 2, 4, 5).reshape(b, h, ncb, kr * NA_KCOLS, d)
        vw = vw.transpose(0, 1, 3, 2, 4, 5).reshape(b, h, ncb, kr * NA_KCOLS, d)
        qr = lax.dynamic_index_in_dim(q5, r, axis=2, keepdims=False).reshape(b, h, ncb, NA_QCOLS, d)
        rel_r = start + jnp.arange(kr) - r + NA_ROWS_MAX - 1
        bias = rpb[:, rel_r[None, None, :, None], col_idx[:, :, None, :]]
        bias = bias.reshape(h, ncb, NA_QCOLS, kr * NA_KCOLS)
        s = jnp.einsum('bhjqd,bhjkd->bhjqk', qr, kw) + bias[None]
        s = jnp.where(mask, s, -jnp.inf)
        p = jax.nn.softmax(s, axis=-1)
        o = jnp.einsum('bhjqk,bhjkd->bhjqd', p, vw)
        return o.reshape(b, h, GRID_W, d)

    out = lax.map(row_block, jnp.arange(rows))
    return out.transpose(1, 2, 0, 3, 4).reshape(b, h, t, d)


def expert_dispatch(xf, expert, gate, w1, w3, w2):
    n, dm = xf.shape
    a_n = expert.shape[0]
    token = jnp.arange(a_n, dtype=jnp.int32) // TOP_K
    counts = jnp.bincount(expert, length=N_EXPERTS)
    padded = (counts + MOE_BLOCK - 1) // MOE_BLOCK * MOE_BLOCK
    start = jnp.cumsum(counts) - counts
    pend = jnp.cumsum(padded)
    pstart = pend - padded
    order = jnp.argsort(expert)
    e_sorted = expert[order]
    dest = pstart[e_sorted] + jnp.arange(a_n) - start[e_sorted]
    n_slots = a_n + N_EXPERTS * MOE_BLOCK
    n_blocks = n_slots // MOE_BLOCK
    slot_tok = jnp.full((n_slots,), n, dtype=jnp.int32).at[dest].set(token[order])
    slot_gate = jnp.zeros((n_slots,), F32).at[dest].set(gate[order])
    block_expert = jnp.clip(jnp.searchsorted(pend, jnp.arange(n_blocks) * MOE_BLOCK, side='right'), 0, N_EXPERTS - 1)
    xpad = jnp.concatenate([xf, jnp.zeros((1, dm), xf.dtype)], axis=0)
    xs = xpad[slot_tok].reshape(n_blocks, MOE_BLOCK, dm)

    def run_block(args):
        xb, e = args
        hb = jax.nn.silu(xb @ w1[e]) * (xb @ w3[e])
        return hb @ w2[e]

    yb = lax.map(run_block, (xs, block_expert)).reshape(n_slots, dm)
    out = jnp.zeros((n + 1, dm), F32).at[slot_tok].add(yb.astype(F32) * slot_gate[:, None])
    return out[:n].astype(xf.dtype)


def hierarchical_moe(x, wg, bg, we, be, w1, w3, w2):
    b, t, dm = x.shape
    n = b * t
    xf = x.reshape(n, dm)
    g_logits = (xf @ wg).astype(F32) + bg.astype(F32)
    g_prob = jax.nn.softmax(g_logits, axis=-1)
    grp = jnp.argmax(g_logits, axis=-1)
    p_grp = jnp.take_along_axis(g_prob, grp[:, None], axis=-1)
    e_logits = ((xf @ we).astype(F32) + be.astype(F32)).reshape(n, N_GROUPS, EXPERTS_PER_GROUP)
    e_in = jnp.take_along_axis(e_logits, grp[:, None, None], axis=1)[:, 0]
    top_v, top_i = lax.top_k(e_in, TOP_K)
    gates = jax.nn.softmax(top_v, axis=-1) * p_grp
    expert = (grp[:, None] * EXPERTS_PER_GROUP + top_i).astype(jnp.int32)
    y = expert_dispatch(xf, expert.reshape(-1), gates.reshape(-1), w1, w3, w2)
    return y.reshape(b, t, dm)


def encoder(x, norm_mix, w_in, mlstm_conv_w, mlstm_conv_b, mlstm_gate_bias, mlstm_norm,
            ret_decay, ret_norm, na_rpb, w_out, norm_ffn, router_group_w, router_group_b,
            router_expert_w, router_expert_b, expert_w1, expert_w3, expert_w2, norm_final):
    b, t, _ = x.shape
    split_at = [int(s) for s in np.cumsum(IN_SIZES)[:-1]]
    for l in range(DEPTH):
        hx = rmsnorm(x, norm_mix[l])
        z = hx @ w_in[l]
        mq, mk, mv, mo, mg, rq, rk, rv, rg, nq, nk, nv = jnp.split(z, split_at, axis=-1)

        qk = jax.nn.silu(centred_dwconv(jnp.concatenate([mq, mk], axis=-1), mlstm_conv_w[l], mlstm_conv_b[l]))
        mq, mk = jnp.split(qk, 2, axis=-1)
        gates = (mg.reshape(b, t, 4, H_MLSTM).astype(F32) + mlstm_gate_bias[l].astype(F32)).transpose(2, 0, 3, 1)
        log_i = gates[:2]
        log_f = jax.nn.log_sigmoid(gates[2:])
        q, k, v = to_heads(mq, H_MLSTM), to_heads(mk, H_MLSTM), to_heads(mv, H_MLSTM)
        h_m = (mlstm_chunkwise(q, k, v, log_i[0], log_f[0])
               + flip_t(mlstm_chunkwise(flip_t(q), flip_t(k), flip_t(v), flip_t(log_i[1]), flip_t(log_f[1]))))
        y_m = head_rmsnorm(h_m, mlstm_norm[l], x.dtype) * jax.nn.sigmoid(mo)

        log_gamma = -jnp.exp(ret_decay[l].astype(F32))
        q, k, v = rotary(to_heads(rq, H_RET)), rotary(to_heads(rk, H_RET)), to_heads(rv, H_RET)
        h_r = (retention_chunkwise(q, k, v, log_gamma[0])
               + flip_t(retention_chunkwise(flip_t(q), flip_t(k), flip_t(v), log_gamma[1])))
        y_r = head_rmsnorm(h_r, ret_norm[l], x.dtype) * jax.nn.silu(rg)

        h_n = neighbourhood_attention(to_heads(nq, H_NA), to_heads(nk, H_NA), to_heads(nv, H_NA),
                                      na_rpb[l].astype(F32))
        y_n = from_heads(h_n).astype(x.dtype)

        x = x + jnp.concatenate([y_m, y_r, y_n], axis=-1) @ w_out[l]
        x = x + hierarchical_moe(rmsnorm(x, norm_ffn[l]), router_group_w[l], router_group_b[l],
                                 router_expert_w[l], router_expert_b[l],
                                 expert_w1[l], expert_w3[l], expert_w2[l])
    return rmsnorm(x, norm_final)


def setup_inputs(seed: int = 0) -> dict:
    key = jax.random.key(seed)
    ks = jax.random.split(key, 24)
    dm = D_MODEL

    def nrm(k, shape, scale):
        return scale * jax.random.normal(k, shape, F32)

    gate_bias = jnp.concatenate([
        nrm(ks[5], (DEPTH, 2, H_MLSTM), 0.1),
        jnp.linspace(3.0, 6.0, H_MLSTM, dtype=F32)[None, None, :] + nrm(ks[6], (DEPTH, 2, H_MLSTM), 0.1)], axis=1)
    ret_base = np.log(-np.log(1.0 - 2.0 ** (-5.0 - np.arange(H_RET)))).astype(np.float32)
    ret_decay = jnp.asarray(ret_base)[None, None, :] + nrm(ks[7], (DEPTH, 2, H_RET), 0.01)
    return {
        "x_prompt": nrm(ks[0], (BATCH, SEQ, dm), 1.0),
        "x_sample": nrm(ks[1], (DEC_BATCH, DEC_SEQ, dm), 1.0),
        "norm_mix": 1.0 + nrm(ks[2], (DEPTH, dm), 0.02),
        "w_in": nrm(ks[3], (DEPTH, dm, IN_WIDTH), dm ** -0.5),
        "mlstm_conv_w": nrm(ks[4], (DEPTH, CONV_W, 2 * W_MLSTM), CONV_W ** -0.5),
        "mlstm_conv_b": nrm(ks[8], (DEPTH, 2 * W_MLSTM), 0.02),
        "mlstm_gate_bias": gate_bias,
        "mlstm_norm": 1.0 + nrm(ks[9], (DEPTH, W_MLSTM), 0.02),
        "ret_decay": ret_decay,
        "ret_norm": 1.0 + nrm(ks[10], (DEPTH, W_RET), 0.02),
        "na_rpb": nrm(ks[11], (DEPTH, H_NA, 2 * NA_ROWS_MAX - 1, 2 * NA_COLS - 1), 0.02),
        "w_out": nrm(ks[12], (DEPTH, dm, dm), dm ** -0.5),
        "norm_ffn": 1.0 + nrm(ks[13], (DEPTH, dm), 0.02),
        "router_group_w": nrm(ks[14], (DEPTH, dm, N_GROUPS), dm ** -0.5),
        "router_group_b": nrm(ks[15], (DEPTH, N_GROUPS), 0.01),
        "router_expert_w": nrm(ks[16], (DEPTH, dm, N_EXPERTS), dm ** -0.5),
        "router_expert_b": nrm(ks[17], (DEPTH, N_EXPERTS), 0.01),
        "expert_w1": nrm(ks[18], (DEPTH, N_EXPERTS, dm, EXPERT_FF), dm ** -0.5),
        "expert_w3": nrm(ks[19], (DEPTH, N_EXPERTS, dm, EXPERT_FF), dm ** -0.5),
        "expert_w2": nrm(ks[20], (DEPTH, N_EXPERTS, EXPERT_FF, dm), EXPERT_FF ** -0.5),
        "norm_final": 1.0 + nrm(ks[21], (dm,), 0.02),
    }


def reference(x_prompt, x_sample, norm_mix, w_in, mlstm_conv_w, mlstm_conv_b, mlstm_gate_bias,
              mlstm_norm, ret_decay, ret_norm, na_rpb, w_out, norm_ffn, router_group_w,
              router_group_b, router_expert_w, router_expert_b, expert_w1, expert_w3,
              expert_w2, norm_final):
    weights = (norm_mix, w_in, mlstm_conv_w, mlstm_conv_b, mlstm_gate_bias, mlstm_norm,
               ret_decay, ret_norm, na_rpb, w_out, norm_ffn, router_group_w, router_group_b,
               router_expert_w, router_expert_b, expert_w1, expert_w3, expert_w2, norm_final)
    y_prompt = encoder(x_prompt, *weights)
    y_sample = encoder(x_sample, *weights)
    return (y_prompt, y_sample)
```

```python
import functools

import numpy as np
import jax
import jax.numpy as jnp
from jax import lax
from jax.experimental import pallas as pl
from jax.experimental.pallas import tpu as pltpu

F32 = jnp.float32
BF16 = jnp.bfloat16

D_MODEL = 1024
HEAD_DIM = 64
LANES = 128
CHUNK = 128
H_MLSTM = 6
H_RET = 6
H_NA = 4
W_MLSTM = H_MLSTM * HEAD_DIM
W_RET = H_RET * HEAD_DIM
W_NA = H_NA * HEAD_DIM
Z_WIDTH = 4 * W_MLSTM + 4 * W_RET + 3 * W_NA
GATE_LF_LANE = 16
ROPE_BASE = 10000.0
GRID_W = 64
NA_ROWS = 8
NA_COLS = 16
NA_QROWS = 4
NA_KROWS = NA_QROWS + NA_ROWS
N_GROUPS = 4
EXPERTS_PER_GROUP = 8
N_EXPERTS = 32
TOP_K = 2
EXPERT_FF = 512
RMS_EPS = 1e-6
NEG = -1e30
VMEM_LIMIT = 56 * 1024 * 1024

PROJ_TM = 512
MOE_BLK = 256
COMB_TM = 256

ZB_MQ, ZB_MK, ZB_MV, ZB_MO = 0, 3, 6, 9
ZB_RQ, ZB_RK, ZB_RV, ZB_RG = 12, 15, 18, 21
ZB_NQ, ZB_NK, ZB_NV = 24, 26, 28


def _dot(a, b):
    return jnp.dot(a, b, preferred_element_type=F32)


def _dot_nt(a, b):
    return lax.dot_general(a, b, (((1,), (1,)), ((), ())), preferred_element_type=F32)


def _dot_tn(a, b):
    return lax.dot_general(a, b, (((0,), (0,)), ((), ())), preferred_element_type=F32)


def _split(x, axis):
    hi = x.astype(BF16)
    lo = (x - hi.astype(F32)).astype(BF16)
    return jnp.concatenate([hi, lo], axis=axis)


def _sigmoid(x):
    return 1.0 / (1.0 + jnp.exp(-x))


def _log_sigmoid(x):
    return -(jnp.maximum(-x, 0.0) + jnp.log(1.0 + jnp.exp(-jnp.abs(x))))


def _cparams(sem):
    return pltpu.CompilerParams(dimension_semantics=sem, vmem_limit_bytes=VMEM_LIMIT)


def _proj_in_kernel(x_ref, g_ref, w_ref, z_ref, gate_ref):
    x = x_ref[...]
    ms = jnp.mean(x * x, axis=-1, keepdims=True)
    hx = (x * lax.rsqrt(ms + RMS_EPS) * g_ref[...]).astype(BF16)
    cw = 768
    for j in range(0, Z_WIDTH, cw):
        z_ref[:, j:j + cw] = _dot(hx, w_ref[:, j:j + cw]).astype(BF16)
    gate_ref[...] = _dot(hx, w_ref[:, Z_WIDTH:Z_WIDTH + LANES])


def _proj_in(x2, g, w):
    n = x2.shape[0]
    wz = w.shape[1]
    return pl.pallas_call(
        _proj_in_kernel,
        out_shape=(jax.ShapeDtypeStruct((n, Z_WIDTH), BF16),
                   jax.ShapeDtypeStruct((n, LANES), F32)),
        grid=(n // PROJ_TM,),
        in_specs=[pl.BlockSpec((PROJ_TM, D_MODEL), lambda i: (i, 0)),
                  pl.BlockSpec((1, D_MODEL), lambda i: (0, 0)),
                  pl.BlockSpec((D_MODEL, wz), lambda i: (0, 0))],
        out_specs=(pl.BlockSpec((PROJ_TM, Z_WIDTH), lambda i: (i, 0)),
                   pl.BlockSpec((PROJ_TM, LANES), lambda i: (i, 0))),
        compiler_params=_cparams(("arbitrary",)),
        name="proj_in",
    )(x2, g, w)


def _lane_lt64():
    return lax.broadcasted_iota(jnp.int32, (CHUNK, LANES), 1) < HEAD_DIM


def _tri_mask(d):
    r = lax.broadcasted_iota(jnp.int32, (CHUNK, CHUNK), 0)
    c = lax.broadcasted_iota(jnp.int32, (CHUNK, CHUNK), 1)
    return (c <= r) if d == 0 else (c >= r)


def _head_norm(h, bd2_ref):
    ms = _dot(_split(h * h, 1), bd2_ref[...]) * (1.0 / HEAD_DIM)
    return h * lax.rsqrt(ms + RMS_EPS)


def _mlstm_kernel(q_ref, k_ref, v_ref, o_ref, g_ref, gb_ref, cwq_ref, cwk_ref, cbq_ref,
                  cbk_ref, nw_ref, xa_ref, xr_ref, tri_ref, bdm_ref, bd2_ref, out_ref,
                  qc_ref, kc_ref, hf_ref, sst_ref, mprev_ref, *, seq):
    nc = seq // CHUNK
    lane_lo = _lane_lt64()
    row = lax.broadcasted_iota(jnp.int32, (CHUNK, LANES), 0)
    lane = lax.broadcasted_iota(jnp.int32, (CHUNK, LANES), 1)
    ones_aug = jnp.ones((CHUNK, LANES), BF16)

    def conv_chunk(src_ref, dst_ref, w_ref, b_ref, c, scale):
        t0 = pl.multiple_of(c * CHUNK, CHUNK)
        x = src_ref[pl.ds(t0, CHUNK), :].astype(F32)
        tp = pl.multiple_of(jnp.maximum(t0 - 16, 0), 16)
        tn = pl.multiple_of(jnp.minimum(t0 + CHUNK, seq - 16), 16)
        prev_last = src_ref[pl.ds(tp, 16), :].astype(F32)[15:16, :]
        next_first = src_ref[pl.ds(tn, 16), :].astype(F32)[0:1, :]
        prev_last = prev_last * jnp.where(c > 0, 1.0, 0.0)
        next_first = next_first * jnp.where(c < nc - 1, 1.0, 0.0)
        xm1 = jnp.where(row == 0, prev_last, pltpu.roll(x, 1, 0))
        xp1 = jnp.where(row == CHUNK - 1, next_first, pltpu.roll(x, CHUNK - 1, 0))
        y = w_ref[0:1, :] * xm1 + w_ref[1:2, :] * x + w_ref[2:3, :] * xp1 + b_ref[...]
        y = y * _sigmoid(y) * scale
        dst_ref[pl.ds(t0, CHUNK), :] = y.astype(BF16)

    def conv_body(c, carry):
        conv_chunk(q_ref, qc_ref, cwq_ref, cbq_ref, c, 1.0)
        conv_chunk(k_ref, kc_ref, cwk_ref, cbk_ref, c, HEAD_DIM ** -0.5)
        return carry

    lax.fori_loop(0, nc, conv_body, 0)

    def gate_tiles(t0, d):
        g = g_ref[pl.ds(t0, CHUNK), :] + gb_ref[...]
        lf = _log_sigmoid(g)
        a = _dot(tri_ref[d], _split(lf, 0))
        return g, a

    def v_aug_at(t0):
        return jnp.concatenate([v_ref[pl.ds(t0, CHUNK), :], ones_aug], axis=1)

    for d in (0, 1):
        xb_li = xa_ref[d, 0]
        xb_lf = xa_ref[d, 1]
        tri = _tri_mask(d)

        def scan_body(c, carry, d=d, xb_li=xb_li, xb_lf=xb_lf):
            s_st, m_st = carry
            cc = c if d == 0 else nc - 1 - c
            t0 = pl.multiple_of(cc * CHUNK, CHUNK)
            g, a = gate_tiles(t0, d)
            li_b = _dot(_split(g, 1), xb_li)
            a_b = _dot(_split(a, 1), xb_lf)
            a_tot = a_b[CHUNK - 1:CHUNK, :] if d == 0 else a_b[0:1, :]
            w_b = li_b + a_tot - a_b
            m_loc = jnp.max(w_b, axis=0, keepdims=True)
            m_new = jnp.maximum(a_tot + m_st, m_loc)
            s_old = jnp.exp(a_tot + m_st - m_new)
            s_new = jnp.exp(m_loc - m_new)
            e_b = jnp.exp(w_b - m_loc)
            k_e = (kc_ref[pl.ds(t0, CHUNK), :].astype(F32) * e_b).astype(BF16)
            kv = _dot_tn(k_e, v_aug_at(t0)) * bdm_ref[...]
            sst_ref[cc] = s_st.astype(BF16)
            mprev_ref[cc] = m_st
            s_old2 = jnp.concatenate([s_old, s_old], axis=1)
            s_new2 = jnp.concatenate([s_new, s_new], axis=1)
            return s_old2 * s_st + s_new2 * kv, m_new

        lax.fori_loop(0, nc, scan_body,
                      (jnp.zeros((CHUNK, 2 * LANES), F32), jnp.zeros((1, LANES), F32)))

        def out_body(c, carry, d=d, tri=tri):
            t0 = pl.multiple_of(c * CHUNK, CHUNK)
            g, a = gate_tiles(t0, d)
            q = qc_ref[pl.ds(t0, CHUNK), :]
            k = kc_ref[pl.ds(t0, CHUNK), :]
            v_aug = v_aug_at(t0)
            m_prev = mprev_ref[c]
            q_s = _dot(q, sst_ref[c])
            a_s = _split(a, 1)
            w_s = _split(jnp.where(lane < GATE_LF_LANE, g, a), 1)
            nums, dens, mrows = [], [], []
            for j in (0, 1):
                acol = _dot(a_s, xa_ref[d, 2 + j])
                rrow = _dot_nt(xr_ref[d, j], w_s)
                dlog = jnp.where(tri, acol - rrow, NEG)
                ginter = acol[:, 0:1] + m_prev[:, HEAD_DIM * j:HEAD_DIM * j + 1]
                mrow = jnp.maximum(jnp.max(dlog, axis=1, keepdims=True), ginter)
                qj = jnp.where(lane_lo if j == 0 else jnp.logical_not(lane_lo), q,
                               jnp.zeros_like(q))
                p = _dot_nt(qj, k) * jnp.exp(dlog - mrow)
                pv = _dot(p.astype(BF16), v_aug)
                sint = jnp.exp(ginter - mrow)
                nums.append(pv[:, :LANES] + sint * q_s[:, :LANES])
                dens.append(pv[:, LANES:] + sint * q_s[:, LANES:])
                mrows.append(jnp.broadcast_to(mrow, (CHUNK, LANES)))
            num = jnp.where(lane_lo, nums[0], nums[1])
            den = jnp.where(lane_lo, dens[0], dens[1])
            mrow_b = jnp.where(lane_lo, mrows[0], mrows[1])
            h = num / jnp.maximum(jnp.abs(den), jnp.exp(-mrow_b))
            if d == 0:
                hf_ref[pl.ds(t0, CHUNK), :] = h
            else:
                h = h + hf_ref[pl.ds(t0, CHUNK), :]
                y = _head_norm(h, bd2_ref) * nw_ref[...]
                y = y * _sigmoid(o_ref[pl.ds(t0, CHUNK), :].astype(F32))
                out_ref[pl.ds(t0, CHUNK), :] = y.astype(BF16)
            return carry

        lax.fori_loop(0, nc, out_body, 0)


def _mlstm(z3, gates3, gbias, conv_w, conv_b, norm_w, consts):
    b, seq, _ = z3.shape
    nc = seq // CHUNK
    npair = H_MLSTM // 2

    def zspec(blk0):
        return pl.BlockSpec((None, seq, LANES), lambda bi, p, blk0=blk0: (bi, 0, blk0 + p))

    kern = functools.partial(_mlstm_kernel, seq=seq)
    return pl.pallas_call(
        kern,
        out_shape=jax.ShapeDtypeStruct((b, seq, W_MLSTM), BF16),
        grid=(b, npair),
        in_specs=[zspec(ZB_MQ), zspec(ZB_MK), zspec(ZB_MV), zspec(ZB_MO),
                  pl.BlockSpec((None, seq, LANES), lambda bi, p: (bi, 0, 0)),
                  pl.BlockSpec((1, LANES), lambda bi, p: (0, 0)),
                  pl.BlockSpec((3, LANES), lambda bi, p: (0, p)),
                  pl.BlockSpec((3, LANES), lambda bi, p: (0, npair + p)),
                  pl.BlockSpec((1, LANES), lambda bi, p: (0, p)),
                  pl.BlockSpec((1, LANES), lambda bi, p: (0, npair + p)),
                  pl.BlockSpec((1, LANES), lambda bi, p: (0, p)),
                  pl.BlockSpec((None, 2, 4, 2 * LANES, LANES), lambda bi, p: (p, 0, 0, 0, 0)),
                  pl.BlockSpec((None, 2, 2, LANES, 2 * LANES), lambda bi, p: (p, 0, 0, 0, 0)),
                  pl.BlockSpec((2, CHUNK, 2 * CHUNK), lambda bi, p: (0, 0, 0)),
                  pl.BlockSpec((LANES, 2 * LANES), lambda bi, p: (0, 0)),
                  pl.BlockSpec((2 * LANES, LANES), lambda bi, p: (0, 0))],
        out_specs=pl.BlockSpec((None, seq, LANES), lambda bi, p: (bi, 0, p)),
        scratch_shapes=[pltpu.VMEM((seq, LANES), BF16),
                        pltpu.VMEM((seq, LANES), BF16),
                        pltpu.VMEM((seq, LANES), F32),
                        pltpu.VMEM((nc, CHUNK, 2 * LANES), BF16),
                        pltpu.VMEM((nc, 1, LANES), F32)],
        compiler_params=_cparams(("arbitrary", "arbitrary")),
        name="mlstm",
    )(z3, z3, z3, z3, gates3, gbias, conv_w, conv_w, conv_b, conv_b, norm_w,
      consts["xa"], consts["xr"], consts["tri"], consts["bdm2"], consts["bd2"])


def _ret_kernel(q_ref, k_ref, v_ref, gt_ref, rd_ref, nw_ref, ca_ref, sa_ref, cb_ref, sb_ref,
                sgn_ref, perm_ref, bdm_ref, bd2_ref, out_ref,
                qr_ref, kr_ref, hf_ref, sst_ref, *, seq):
    nc = seq // CHUNK
    lane_lo = _lane_lt64()
    rowf = lax.broadcasted_iota(jnp.int32, (CHUNK, LANES), 0).astype(F32)
    ri = lax.broadcasted_iota(jnp.int32, (CHUNK, CHUNK), 0)
    ci = lax.broadcasted_iota(jnp.int32, (CHUNK, CHUNK), 1)

    def rot_body(c, carry):
        t0 = pl.multiple_of(c * CHUNK, CHUNK)
        cb = cb_ref[c]
        sb = sb_ref[c]
        cos = ca_ref[...] * cb - sa_ref[...] * sb
        sin = (sa_ref[...] * cb + ca_ref[...] * sb) * sgn_ref[...]
        for src, dst in ((q_ref, qr_ref), (k_ref, kr_ref)):
            x = src[pl.ds(t0, CHUNK), :]
            xs = _dot(x, perm_ref[...])
            dst[pl.ds(t0, CHUNK), :] = (x.astype(F32) * cos + xs * sin).astype(BF16)
        return carry

    lax.fori_loop(0, nc, rot_body, 0)

    for d in (0, 1):
        lg_b = -jnp.exp(rd_ref[d])
        if d == 0:
            kdec = jnp.exp(lg_b * (CHUNK - 1.0 - rowf))
            qdec = jnp.exp(lg_b * (rowf + 1.0))
            dist = (ri - ci).astype(F32)
        else:
            kdec = jnp.exp(lg_b * rowf)
            qdec = jnp.exp(lg_b * (CHUNK - rowf))
            dist = (ci - ri).astype(F32)
        cdec = jnp.exp(lg_b * float(CHUNK))
        decays = []
        for j in (0, 1):
            lg_j = lg_b[:, HEAD_DIM * j:HEAD_DIM * j + 1]
            decays.append(jnp.where(dist >= 0.0, jnp.exp(lg_j * jnp.maximum(dist, 0.0)), 0.0))

        def scan_body(c, s_st, d=d, kdec=kdec, cdec=cdec):
            cc = c if d == 0 else nc - 1 - c
            t0 = pl.multiple_of(cc * CHUNK, CHUNK)
            k_d = (kr_ref[pl.ds(t0, CHUNK), :].astype(F32) * kdec).astype(BF16)
            kv = _dot_tn(k_d, v_ref[pl.ds(t0, CHUNK), :]) * bdm_ref[...]
            sst_ref[cc] = s_st.astype(BF16)
            return cdec * s_st + kv

        lax.fori_loop(0, nc, scan_body, jnp.zeros((CHUNK, LANES), F32))

        def out_body(c, carry, d=d, qdec=qdec, decays=decays):
            t0 = pl.multiple_of(c * CHUNK, CHUNK)
            q = qr_ref[pl.ds(t0, CHUNK), :]
            k = kr_ref[pl.ds(t0, CHUNK), :]
            v = v_ref[pl.ds(t0, CHUNK), :]
            y_inter = _dot((q.astype(F32) * qdec).astype(BF16), sst_ref[c])
            ys = []
            for j in (0, 1):
                qj = jnp.where(lane_lo if j == 0 else jnp.logical_not(lane_lo), q,
                               jnp.zeros_like(q))
                s = _dot_nt(qj, k) * decays[j]
                ys.append(_dot(s.astype(BF16), v))
            h = jnp.where(lane_lo, ys[0], ys[1]) + y_inter
            if d == 0:
                hf_ref[pl.ds(t0, CHUNK), :] = h
            else:
                h = h + hf_ref[pl.ds(t0, CHUNK), :]
                y = _head_norm(h, bd2_ref) * nw_ref[...]
                gt = gt_ref[pl.ds(t0, CHUNK), :].astype(F32)
                y = y * (gt * _sigmoid(gt))
                out_ref[pl.ds(t0, CHUNK), :] = y.astype(BF16)
            return carry

        lax.fori_loop(0, nc, out_body, 0)


def _retention(z3, rd, norm_w, consts, rope):
    b, seq, _ = z3.shape
    nc = seq // CHUNK
    npair = H_RET // 2

    def zspec(blk0):
        return pl.BlockSpec((None, seq, LANES), lambda bi, p, blk0=blk0: (bi, 0, blk0 + p))

    def full2(shape):
        return pl.BlockSpec(shape, lambda bi, p: (0, 0))

    kern = functools.partial(_ret_kernel, seq=seq)
    return pl.pallas_call(
        kern,
        out_shape=jax.ShapeDtypeStruct((b, seq, W_RET), BF16),
        grid=(b, npair),
        in_specs=[zspec(ZB_RQ), zspec(ZB_RK), zspec(ZB_RV), zspec(ZB_RG),
                  pl.BlockSpec((None, 2, 1, LANES), lambda bi, p: (p, 0, 0, 0)),
                  pl.BlockSpec((1, LANES), lambda bi, p: (0, p)),
                  full2((CHUNK, LANES)), full2((CHUNK, LANES)),
                  pl.BlockSpec((nc, 1, LANES), lambda bi, p: (0, 0, 0)),
                  pl.BlockSpec((nc, 1, LANES), lambda bi, p: (0, 0, 0)),
                  full2((1, LANES)), full2((LANES, LANES)), full2((LANES, LANES)),
                  full2((2 * LANES, LANES))],
        out_specs=pl.BlockSpec((None, seq, LANES), lambda bi, p: (bi, 0, p)),
        scratch_shapes=[pltpu.VMEM((seq, LANES), BF16),
                        pltpu.VMEM((seq, LANES), BF16),
                        pltpu.VMEM((seq, LANES), F32),
                        pltpu.VMEM((nc, CHUNK, LANES), BF16)],
        compiler_params=_cparams(("arbitrary", "arbitrary")),
        name="retention",
    )(z3, z3, z3, z3, rd, norm_w, rope["ca"], rope["sa"], rope["cb"], rope["sb"],
      consts["sgn"], consts["perm"], consts["bdm"], consts["bd2"])


def _na_kernel(q_ref, k_ref, v_ref, bias_ref, out_ref, *, seq):
    rows = seq // GRID_W
    ng = rows // NA_QROWS
    nq = NA_QROWS * GRID_W
    nk = NA_KROWS * GRID_W
    lane_lo = lax.broadcasted_iota(jnp.int32, (nq, LANES), 1) < HEAD_DIM
    ones_aug = jnp.ones((nk, LANES), BF16)

    def body(g, carry):
        t0 = pl.multiple_of(g * nq, nq)
        base = jnp.clip(g * NA_QROWS - NA_ROWS // 2, 0, rows - NA_KROWS)
        k0 = pl.multiple_of(base * GRID_W, GRID_W)
        case = jnp.where(g == 0, 0, jnp.where(g == ng - 1, 2, 1))
        q = q_ref[pl.ds(t0, nq), :]
        kk = k_ref[pl.ds(k0, nk), :]
        v_aug = jnp.concatenate([v_ref[pl.ds(k0, nk), :], ones_aug], axis=1)
        outs = []
        for j in (0, 1):
            qj = jnp.where(lane_lo if j == 0 else jnp.logical_not(lane_lo), q,
                           jnp.zeros_like(q))
            s = _dot_nt(qj, kk) + bias_ref[case, j]
            m = jnp.max(s, axis=1, keepdims=True)
            e = jnp.exp(s - m)
            pv = _dot(e.astype(BF16), v_aug)
            outs.append(pv[:, :LANES] / pv[:, LANES:])
        out_ref[pl.ds(t0, nq), :] = jnp.where(lane_lo, outs[0], outs[1]).astype(BF16)
        return carry

    lax.fori_loop(0, ng, body, 0)


def _na(z3, bias):
    b, seq, _ = z3.shape
    npair = H_NA // 2
    nq = NA_QROWS * GRID_W
    nk = NA_KROWS * GRID_W

    def zspec(blk0):
        return pl.BlockSpec((None, seq, LANES), lambda bi, p, blk0=blk0: (bi, 0, blk0 + p))

    kern = functools.partial(_na_kernel, seq=seq)
    return pl.pallas_call(
        kern,
        out_shape=jax.ShapeDtypeStruct((b, seq, W_NA), BF16),
        grid=(b, npair),
        in_specs=[zspec(ZB_NQ), zspec(ZB_NK), zspec(ZB_NV),
                  pl.BlockSpec((None, 3, 2, nq, nk), lambda bi, p: (p, 0, 0, 0, 0))],
        out_specs=pl.BlockSpec((None, seq, LANES), lambda bi, p: (bi, 0, p)),
        compiler_params=_cparams(("arbitrary", "arbitrary")),
        name="natten",
    )(z3, z3, z3, bias)


def _na_index_tables(seq):
    rows = seq // GRID_W
    ng = rows // NA_QROWS
    tabs = []
    for g in (0, 1, ng - 1):
        base = int(np.clip(g * NA_QROWS - NA_ROWS // 2, 0, rows - NA_KROWS))
        qr = g * NA_QROWS + np.arange(NA_QROWS)[:, None, None, None]
        qc = np.arange(GRID_W)[None, :, None, None]
        kr = base + np.arange(NA_KROWS)[None, None, :, None]
        kc = np.arange(GRID_W)[None, None, None, :]
        rstart = np.clip(qr - NA_ROWS // 2, 0, rows - NA_ROWS)
        cstart = np.clip(qc - NA_COLS // 2, 0, GRID_W - NA_COLS)
        ok = (kr >= rstart) & (kr < rstart + NA_ROWS) & (kc >= cstart) & (kc < cstart + NA_COLS)
        rel_r = np.clip(kr - qr + NA_ROWS - 1, 0, 2 * NA_ROWS - 2)
        rel_c = np.clip(kc - qc + NA_COLS - 1, 0, 2 * NA_COLS - 2)
        shp = (NA_QROWS * GRID_W, NA_KROWS * GRID_W)
        full = (NA_QROWS, GRID_W, NA_KROWS, GRID_W)
        tabs.append((np.broadcast_to(ok, full).reshape(shp),
                     np.broadcast_to(rel_r, full).reshape(shp),
                     np.broadcast_to(rel_c, full).reshape(shp)))
    ok = np.stack([t[0] for t in tabs])
    rr = np.stack([t[1] for t in tabs])
    rc = np.stack([t[2] for t in tabs])
    return ok, rr, rc


def _na_bias(rpb, seq):
    ok, rr, rc = _na_index_tables(seq)
    bias = rpb.astype(F32)[:, rr, rc]
    bias = jnp.where(ok[None], bias, NEG)
    h = rpb.shape[0]
    return bias.reshape(h // 2, 2, 3, bias.shape[2], bias.shape[3]).transpose(0, 2, 1, 3, 4)


def _proj_out_kernel(ym_ref, yr_ref, yn_ref, x_ref, wm_ref, wr_ref, wn_ref, g_ref, rw_ref,
                     rb_ref, x2_ref, xn_ref, route_ref):
    acc = _dot(ym_ref[...], wm_ref[...]) + _dot(yr_ref[...], wr_ref[...])
    acc = acc + _dot(yn_ref[...], wn_ref[...])
    x2 = x_ref[...] + acc
    x2_ref[...] = x2
    ms = jnp.mean(x2 * x2, axis=-1, keepdims=True)
    xn = x2 * lax.rsqrt(ms + RMS_EPS) * g_ref[...]
    xn_ref[...] = xn
    logits = jnp.dot(xn, rw_ref[...], preferred_element_type=F32,
                     precision=lax.Precision.HIGHEST) + rb_ref[...]
    tm = logits.shape[0]
    lane = lax.broadcasted_iota(jnp.int32, (tm, LANES), 1).astype(F32)
    glog = jnp.where(lane < N_GROUPS, logits, NEG)
    gmax = jnp.max(glog, axis=1, keepdims=True)
    grp = jnp.min(jnp.where(glog == gmax, lane, float(LANES)), axis=1, keepdims=True)
    p_grp = 1.0 / jnp.sum(jnp.exp(glog - gmax), axis=1, keepdims=True)
    lo = N_GROUPS + grp * EXPERTS_PER_GROUP
    ein = jnp.where((lane >= lo) & (lane < lo + EXPERTS_PER_GROUP), logits, NEG)
    v1 = jnp.max(ein, axis=1, keepdims=True)
    i1 = jnp.min(jnp.where(ein == v1, lane, float(LANES)), axis=1, keepdims=True)
    ein2 = jnp.where(lane == i1, NEG, ein)
    v2 = jnp.max(ein2, axis=1, keepdims=True)
    i2 = jnp.min(jnp.where(ein2 == v2, lane, float(LANES)), axis=1, keepdims=True)
    e21 = jnp.exp(v2 - v1)
    g1 = p_grp / (1.0 + e21)
    g2 = p_grp * e21 / (1.0 + e21)
    route = jnp.where(lane == 0, i1 - N_GROUPS,
                      jnp.where(lane == 1, i2 - N_GROUPS,
                                jnp.where(lane == 2, g1, jnp.where(lane == 3, g2, 0.0))))
    route_ref[...] = route


def _proj_out(ym, yr, yn, x2, wm, wr, wn, g, rw, rb):
    n = x2.shape[0]
    tm = PROJ_TM

    def rows(w):
        return pl.BlockSpec((tm, w), lambda i: (i, 0))

    def full(shape):
        return pl.BlockSpec(shape, lambda i: (0, 0))

    return pl.pallas_call(
        _proj_out_kernel,
        out_shape=(jax.ShapeDtypeStruct((n, D_MODEL), F32),
                   jax.ShapeDtypeStruct((n, D_MODEL), F32),
                   jax.ShapeDtypeStruct((n, LANES), F32)),
        grid=(n // tm,),
        in_specs=[rows(W_MLSTM), rows(W_RET), rows(W_NA), rows(D_MODEL),
                  full((W_MLSTM, D_MODEL)), full((W_RET, D_MODEL)), full((W_NA, D_MODEL)),
                  full((1, D_MODEL)), full((D_MODEL, LANES)), full((1, LANES))],
        out_specs=(rows(D_MODEL), rows(D_MODEL), rows(LANES)),
        compiler_params=_cparams(("arbitrary",)),
        name="proj_out_router",
    )(ym, yr, yn, x2, wm, wr, wn, g, rw, rb)


def _expert_kernel(be_ref, nused_ref, tok_hbm, xn_hbm, w1_ref, w3_ref, w2_ref, y_ref,
                   idx_smem, xbuf, sem_idx, sem_rows):
    i = pl.program_id(0)

    @pl.when(i < nused_ref[0])
    def _():
        cp = pltpu.make_async_copy(tok_hbm.at[i], idx_smem, sem_idx)
        cp.start()
        cp.wait()

        def issue(r, carry):
            tok = idx_smem[r]
            pltpu.make_async_copy(xn_hbm.at[pl.ds(tok, 1), :], xbuf.at[pl.ds(r, 1), :],
                                  sem_rows).start()
            return carry

        lax.fori_loop(0, MOE_BLK, issue, 0)

        def drain(r, carry):
            pltpu.make_async_copy(xn_hbm.at[pl.ds(0, 1), :], xbuf.at[pl.ds(r, 1), :],
                                  sem_rows).wait()
            return carry

        lax.fori_loop(0, MOE_BLK, drain, 0)
        xb = xbuf[...].astype(BF16)
        h1 = _dot(xb, w1_ref[...])
        h3 = _dot(xb, w3_ref[...])
        hb = (h1 * _sigmoid(h1) * h3).astype(BF16)
        y_ref[...] = _dot(hb, w2_ref[...])

    @pl.when(i >= nused_ref[0])
    def _():
        y_ref[...] = jnp.zeros_like(y_ref)


def _experts(block_expert, n_used, slot_tok2, xn, w1, w3, w2):
    n_blocks = slot_tok2.shape[0]
    grid_spec = pltpu.PrefetchScalarGridSpec(
        num_scalar_prefetch=2,
        grid=(n_blocks,),
        in_specs=[pl.BlockSpec(memory_space=pl.ANY),
                  pl.BlockSpec(memory_space=pl.ANY),
                  pl.BlockSpec((None, D_MODEL, EXPERT_FF), lambda i, be, nu: (be[i], 0, 0)),
                  pl.BlockSpec((None, D_MODEL, EXPERT_FF), lambda i, be, nu: (be[i], 0, 0)),
                  pl.BlockSpec((None, EXPERT_FF, D_MODEL), lambda i, be, nu: (be[i], 0, 0))],
        out_specs=pl.BlockSpec((MOE_BLK, D_MODEL), lambda i, be, nu: (i, 0)),
        scratch_shapes=[pltpu.SMEM((MOE_BLK,), jnp.int32),
                        pltpu.VMEM((MOE_BLK, D_MODEL), F32),
                        pltpu.SemaphoreType.DMA,
                        pltpu.SemaphoreType.DMA],
    )
    return pl.pallas_call(
        _expert_kernel,
        out_shape=jax.ShapeDtypeStruct((n_blocks * MOE_BLK, D_MODEL), F32),
        grid_spec=grid_spec,
        compiler_params=_cparams(("arbitrary",)),
        name="experts",
    )(block_expert, n_used, slot_tok2, xn, w1, w3, w2)


def _combine_kernel(dest_hbm, yb_hbm, x_ref, route_ref, g_ref, out_ref,
                    idx_smem, ybuf, sem_idx, sem_rows, *, final_norm):
    i = pl.program_id(0)
    cp = pltpu.make_async_copy(dest_hbm.at[i], idx_smem, sem_idx)
    cp.start()
    cp.wait()

    def issue(r, carry):
        for kk in range(TOP_K):
            slot = idx_smem[TOP_K * r + kk]
            pltpu.make_async_copy(yb_hbm.at[pl.ds(slot, 1), :],
                                  ybuf.at[kk, pl.ds(r, 1), :], sem_rows).start()
        return carry

    lax.fori_loop(0, COMB_TM, issue, 0)

    def drain(r, carry):
        for kk in range(TOP_K):
            pltpu.make_async_copy(yb_hbm.at[pl.ds(0, 1), :],
                                  ybuf.at[kk, pl.ds(r, 1), :], sem_rows).wait()
        return carry

    lax.fori_loop(0, COMB_TM, drain, 0)
    route = route_ref[...]
    y = x_ref[...]
    for kk in range(TOP_K):
        y = y + route[:, 2 + kk:3 + kk] * ybuf[kk]
    if final_norm:
        ms = jnp.mean(y * y, axis=-1, keepdims=True)
        y = y * lax.rsqrt(ms + RMS_EPS) * g_ref[...]
    out_ref[...] = y


def _combine(dest2, yb, x2, route, g, final_norm):
    n = x2.shape[0]
    tm = COMB_TM
    kern = functools.partial(_combine_kernel, final_norm=final_norm)
    return pl.pallas_call(
        kern,
        out_shape=jax.ShapeDtypeStruct((n, D_MODEL), F32),
        grid=(n // tm,),
        in_specs=[pl.BlockSpec(memory_space=pl.ANY),
                  pl.BlockSpec(memory_space=pl.ANY),
                  pl.BlockSpec((tm, D_MODEL), lambda i: (i, 0)),
                  pl.BlockSpec((tm, LANES), lambda i: (i, 0)),
                  pl.BlockSpec((1, D_MODEL), lambda i: (0, 0))],
        out_specs=pl.BlockSpec((tm, D_MODEL), lambda i: (i, 0)),
        scratch_shapes=[pltpu.SMEM((TOP_K * tm,), jnp.int32),
                        pltpu.VMEM((TOP_K, tm, yb.shape[1]), yb.dtype),
                        pltpu.SemaphoreType.DMA,
                        pltpu.SemaphoreType.DMA],
        compiler_params=_cparams(("arbitrary",)),
        name="combine",
    )(dest2, yb, x2, route, g)


def _dispatch_indices(route, n):
    expert = route[:, :TOP_K].astype(jnp.int32).reshape(-1)
    a_n = expert.shape[0]
    n_blocks = a_n // MOE_BLK + N_EXPERTS
    onehot = (expert[:, None] == jnp.arange(N_EXPERTS, dtype=jnp.int32)[None, :]).astype(jnp.int32)
    csum = jnp.cumsum(onehot, axis=0)
    counts = csum[-1]
    rank = jnp.take_along_axis(csum, expert[:, None], axis=1)[:, 0] - 1
    padded = (counts + MOE_BLK - 1) // MOE_BLK * MOE_BLK
    pend = jnp.cumsum(padded)
    pstart = pend - padded
    dest = pstart[expert] + rank
    token = jnp.arange(a_n, dtype=jnp.int32) // TOP_K
    slot_tok = jnp.zeros((n_blocks * MOE_BLK,), jnp.int32).at[dest].set(token)
    n_used = (pend[-1] // MOE_BLK).astype(jnp.int32)
    blk_start = jnp.arange(n_blocks, dtype=jnp.int32) * MOE_BLK
    block_expert = jnp.searchsorted(pend, blk_start, side="right").astype(jnp.int32)
    last_e = jnp.clip(jnp.searchsorted(pend, (n_used - 1) * MOE_BLK, side="right"), 0,
                      N_EXPERTS - 1).astype(jnp.int32)
    block_expert = jnp.where(blk_start < pend[-1], jnp.clip(block_expert, 0, N_EXPERTS - 1), last_e)
    return (block_expert, n_used.reshape(1), slot_tok.reshape(n_blocks, MOE_BLK),
            dest.astype(jnp.int32).reshape(n // COMB_TM, TOP_K * COMB_TM))


def _const_tables():
    lane = np.arange(LANES)
    grp = lane // HEAD_DIM
    r2 = np.arange(2 * LANES) % LANES
    npair = H_MLSTM // 2
    xa = np.zeros((npair, 2, 4, 2 * LANES, LANES), np.float32)
    xr = np.zeros((npair, 2, 2, LANES, 2 * LANES), np.float32)
    for p in range(npair):
        for d in range(2):
            li_lane = d * H_MLSTM + 2 * p + grp
            lf_lane = GATE_LF_LANE + li_lane
            xa[p, d, 0] = (r2[:, None] == li_lane[None, :])
            xa[p, d, 1] = (r2[:, None] == lf_lane[None, :])
            for j in range(2):
                lij = d * H_MLSTM + 2 * p + j
                lfj = GATE_LF_LANE + lij
                xa[p, d, 2 + j] = np.broadcast_to((r2 == lfj)[:, None], (2 * LANES, LANES))
                xr[p, d, j] = np.broadcast_to(((r2 == lfj).astype(np.float32)
                                               - (r2 == lij).astype(np.float32))[None, :],
                                              (LANES, 2 * LANES))
    ii = np.arange(CHUNK)
    low = (ii[None, :] <= ii[:, None]).astype(np.float32)
    upp = (ii[None, :] >= ii[:, None]).astype(np.float32)
    tri = np.stack([np.concatenate([low, low], 1), np.concatenate([upp, upp], 1)])
    bd = (grp[:, None] == grp[None, :]).astype(np.float32)
    perm = np.zeros((LANES, LANES), np.float32)
    half = HEAD_DIM // 2
    src = (lane // HEAD_DIM) * HEAD_DIM + (lane % HEAD_DIM + half) % HEAD_DIM
    perm[src, lane] = 1.0
    sgn = np.where(lane % HEAD_DIM < half, -1.0, 1.0).astype(np.float32)[None, :]
    return {
        "xa": jnp.asarray(xa, BF16), "xr": jnp.asarray(xr, BF16), "tri": jnp.asarray(tri, BF16),
        "bdm": jnp.asarray(bd, F32), "bdm2": jnp.asarray(np.concatenate([bd, bd], 1), F32),
        "bd2": jnp.asarray(np.concatenate([bd, bd], 0), BF16),
        "perm": jnp.asarray(perm, BF16), "sgn": jnp.asarray(sgn, F32),
    }


def _rope_tables(seq):
    half = HEAD_DIM // 2
    nc = seq // CHUNK
    inv_freq = ROPE_BASE ** (-np.arange(half, dtype=np.float64) / half)
    freq = inv_freq[np.arange(LANES) % half]
    ang_a = np.arange(CHUNK, dtype=np.float64)[:, None] * freq[None, :]
    ang_b = (np.arange(nc, dtype=np.float64) * CHUNK)[:, None] * freq[None, :]
    return {"ca": jnp.asarray(np.cos(ang_a), F32), "sa": jnp.asarray(np.sin(ang_a), F32),
            "cb": jnp.asarray(np.cos(ang_b)[:, None, :], F32),
            "sb": jnp.asarray(np.sin(ang_b)[:, None, :], F32)}


def _layer_weights(l, w_in, mlstm_gate_bias, ret_decay, w_out, router_group_w, router_group_b,
                   router_expert_w, router_expert_b):
    sizes = (W_MLSTM,) * 4 + (4 * H_MLSTM,) + (W_RET,) * 4 + (W_NA,) * 3
    offs = np.concatenate([[0], np.cumsum(sizes)])
    col = lambda i: w_in[l][:, int(offs[i]):int(offs[i + 1])]
    mq, mk, mv, mo, mg, rq, rk, rv, rg, nq, nk, nv = [col(i) for i in range(12)]
    scale = HEAD_DIM ** -0.5
    zpad = jnp.zeros((D_MODEL, 4), F32)
    gate_w = jnp.concatenate([mg[:, :2 * H_MLSTM], zpad, mg[:, 2 * H_MLSTM:],
                              jnp.zeros((D_MODEL, LANES - GATE_LF_LANE - 2 * H_MLSTM), F32)], axis=1)
    w_all = jnp.concatenate([mq, mk, mv, mo, rq, rk * scale, rv, rg, nq * scale, nk, nv, gate_w],
                            axis=1).astype(BF16)
    gb = mlstm_gate_bias[l].astype(F32).reshape(-1)
    gbias = jnp.concatenate([gb[:2 * H_MLSTM], jnp.zeros((4,), F32), gb[2 * H_MLSTM:],
                             jnp.zeros((LANES - GATE_LF_LANE - 2 * H_MLSTM,), F32)])[None, :]
    rd = ret_decay[l].astype(F32)
    rd = jnp.repeat(rd.reshape(2, H_RET // 2, 2), HEAD_DIM, axis=2)
    rd = rd.transpose(1, 0, 2)[:, :, None, :]
    wo = w_out[l].astype(BF16)
    rw = jnp.concatenate([router_group_w[l], router_expert_w[l],
                          jnp.zeros((D_MODEL, LANES - N_GROUPS - N_EXPERTS), F32)], axis=1)
    rb = jnp.concatenate([router_group_b[l].astype(F32), router_expert_b[l].astype(F32),
                          jnp.zeros((LANES - N_GROUPS - N_EXPERTS,), F32)])[None, :]
    return w_all, gbias, rd, wo[:W_MLSTM], wo[W_MLSTM:W_MLSTM + W_RET], wo[W_MLSTM + W_RET:], rw, rb


def _encoder(x, consts, norm_mix, w_in, mlstm_conv_w, mlstm_conv_b, mlstm_gate_bias, mlstm_norm,
             ret_decay, ret_norm, na_rpb, w_out, norm_ffn, router_group_w, router_group_b,
             router_expert_w, router_expert_b, expert_w1, expert_w3, expert_w2, norm_final):
    b, seq, _ = x.shape
    n = b * seq
    depth = w_in.shape[0]
    rope = _rope_tables(seq)
    x2 = x.reshape(n, D_MODEL).astype(F32)
    for l in range(depth):
        w_all, gbias, rd, wo_m, wo_r, wo_n, rw, rb = _layer_weights(
            l, w_in, mlstm_gate_bias, ret_decay, w_out, router_group_w, router_group_b,
            router_expert_w, router_expert_b)
        z, gates = _proj_in(x2, norm_mix[l].astype(F32)[None, :], w_all)
        z3 = z.reshape(b, seq, Z_WIDTH)
        y_m = _mlstm(z3, gates.reshape(b, seq, LANES), gbias, mlstm_conv_w[l].astype(F32),
                     mlstm_conv_b[l].astype(F32)[None, :], mlstm_norm[l].astype(F32)[None, :], consts)
        y_r = _retention(z3, rd, ret_norm[l].astype(F32)[None, :], consts, rope)
        y_n = _na(z3, _na_bias(na_rpb[l], seq))
        x2, xn, route = _proj_out(y_m.reshape(n, W_MLSTM), y_r.reshape(n, W_RET),
                                  y_n.reshape(n, W_NA), x2, wo_m, wo_r, wo_n,
                                  norm_ffn[l].astype(F32)[None, :], rw, rb)
        block_expert, n_used, slot_tok2, dest2 = _dispatch_indices(route, n)
        yb = _experts(block_expert, n_used, slot_tok2, xn,
                      expert_w1[l].astype(BF16), expert_w3[l].astype(BF16), expert_w2[l].astype(BF16))
        x2 = _combine(dest2, yb, x2, route, norm_final.astype(F32)[None, :],
                      final_norm=(l == depth - 1))
    return x2.reshape(b, seq, D_MODEL)


def kernel(x_prompt, x_sample, norm_mix, w_in, mlstm_conv_w, mlstm_conv_b, mlstm_gate_bias,
           mlstm_norm, ret_decay, ret_norm, na_rpb, w_out, norm_ffn, router_group_w,
           router_group_b, router_expert_w, router_expert_b, expert_w1, expert_w3, expert_w2,
           norm_final):
    consts = _const_tables()
    weights = (norm_mix, w_in, mlstm_conv_w, mlstm_conv_b, mlstm_gate_bias, mlstm_norm,
               ret_decay, ret_norm, na_rpb, w_out, norm_ffn, router_group_w, router_group_b,
               router_expert_w, router_expert_b, expert_w1, expert_w3, expert_w2, norm_final)
    return (_encoder(x_prompt, consts, *weights), _encoder(x_sample, consts, *weights))
```

```python
import functools

import numpy as np
import jax
import jax.numpy as jnp
from jax import lax
from jax.experimental import pallas as pl
from jax.experimental.pallas import tpu as pltpu

F32 = jnp.float32
BF16 = jnp.bfloat16

D_MODEL = 1024
HEAD_DIM = 64
LANES = 128
CHUNK = 128
H_MLSTM = 6
H_RET = 6
H_NA = 4
W_MLSTM = H_MLSTM * HEAD_DIM
W_RET = H_RET * HEAD_DIM
W_NA = H_NA * HEAD_DIM
Z_WIDTH = 4 * W_MLSTM + 4 * W_RET + 3 * W_NA
GATE_LF_LANE = 16
ROPE_BASE = 10000.0
GRID_W = 64
NA_ROWS = 8
NA_COLS = 16
NA_QROWS = 4
NA_KROWS = NA_QROWS + NA_ROWS
N_GROUPS = 4
EXPERTS_PER_GROUP = 8
N_EXPERTS = 32
TOP_K = 2
EXPERT_FF = 512
RMS_EPS = 1e-6
NEG = -1e30
VMEM_LIMIT = 56 * 1024 * 1024

PROJ_TM = 512
MOE_BLK = 256
COMB_TM = 256
MIX_UNROLL = 2
RET_UNROLL = 4

ZB_MQ, ZB_MK, ZB_MV, ZB_MO = 0, 3, 6, 9
ZB_RQ, ZB_RK, ZB_RV, ZB_RG = 12, 15, 18, 21
ZB_NQ, ZB_NK, ZB_NV = 24, 26, 28


def _dot(a, b):
    return jnp.dot(a, b, preferred_element_type=F32)


def _dot_nt(a, b):
    return lax.dot_general(a, b, (((1,), (1,)), ((), ())), preferred_element_type=F32)


def _dot_tn(a, b):
    return lax.dot_general(a, b, (((0,), (0,)), ((), ())), preferred_element_type=F32)


def _split(x, axis):
    hi = x.astype(BF16)
    lo = (x - hi.astype(F32)).astype(BF16)
    return jnp.concatenate([hi, lo], axis=axis)


def _sigmoid(x):
    return 1.0 / (1.0 + jnp.exp(-x))


def _log_sigmoid(x):
    return -(jnp.maximum(-x, 0.0) + jnp.log(1.0 + jnp.exp(-jnp.abs(x))))


def _cparams(sem):
    return pltpu.CompilerParams(dimension_semantics=sem, vmem_limit_bytes=VMEM_LIMIT)


def _proj_in_kernel(x_ref, g_ref, w_ref, z_ref, gate_ref):
    x = x_ref[...]
    ms = jnp.mean(x * x, axis=-1, keepdims=True)
    hx = (x * lax.rsqrt(ms + RMS_EPS) * g_ref[...]).astype(BF16)
    cw = 768
    for j in range(0, Z_WIDTH, cw):
        z_ref[:, j:j + cw] = _dot(hx, w_ref[:, j:j + cw]).astype(BF16)
    gate_ref[...] = _dot(hx, w_ref[:, Z_WIDTH:Z_WIDTH + LANES])


def _proj_in(x2, g, w):
    n = x2.shape[0]
    wz = w.shape[1]
    return pl.pallas_call(
        _proj_in_kernel,
        out_shape=(jax.ShapeDtypeStruct((n, Z_WIDTH), BF16),
                   jax.ShapeDtypeStruct((n, LANES), F32)),
        grid=(n // PROJ_TM,),
        in_specs=[pl.BlockSpec((PROJ_TM, D_MODEL), lambda i: (i, 0)),
                  pl.BlockSpec((1, D_MODEL), lambda i: (0, 0)),
                  pl.BlockSpec((D_MODEL, wz), lambda i: (0, 0))],
        out_specs=(pl.BlockSpec((PROJ_TM, Z_WIDTH), lambda i: (i, 0)),
                   pl.BlockSpec((PROJ_TM, LANES), lambda i: (i, 0))),
        compiler_params=_cparams(("arbitrary",)),
        name="proj_in",
    )(x2, g, w)


def _lane_lt64():
    return lax.broadcasted_iota(jnp.int32, (CHUNK, LANES), 1) < HEAD_DIM


def _tri_mask(d):
    r = lax.broadcasted_iota(jnp.int32, (CHUNK, CHUNK), 0)
    c = lax.broadcasted_iota(jnp.int32, (CHUNK, CHUNK), 1)
    return (c <= r) if d == 0 else (c >= r)


def _head_norm(h, bd2_ref):
    ms = _dot(_split(h * h, 1), bd2_ref[...]) * (1.0 / HEAD_DIM)
    return h * lax.rsqrt(ms + RMS_EPS)


def _mlstm_kernel(q_ref, k_ref, v_ref, o_ref, g_ref, gb_ref, cwq_ref, cwk_ref, cbq_ref,
                  cbk_ref, nw_ref, xa_ref, xr_ref, tri_ref, bdm_ref, bd2_ref, out_ref,
                  qc_ref, kc_ref, hf_ref, sst_ref, mprev_ref, *, seq):
    nc = seq // CHUNK
    lane_lo = _lane_lt64()
    row = lax.broadcasted_iota(jnp.int32, (CHUNK, LANES), 0)
    lane = lax.broadcasted_iota(jnp.int32, (CHUNK, LANES), 1)
    ones_aug = jnp.ones((CHUNK, LANES), BF16)

    def conv_chunk(src_ref, dst_ref, w_ref, b_ref, c, scale):
        t0 = pl.multiple_of(c * CHUNK, CHUNK)
        x = src_ref[pl.ds(t0, CHUNK), :].astype(F32)
        tp = pl.multiple_of(jnp.maximum(t0 - 16, 0), 16)
        tn = pl.multiple_of(jnp.minimum(t0 + CHUNK, seq - 16), 16)
        prev_last = src_ref[pl.ds(tp, 16), :].astype(F32)[15:16, :]
        next_first = src_ref[pl.ds(tn, 16), :].astype(F32)[0:1, :]
        prev_last = prev_last * jnp.where(c > 0, 1.0, 0.0)
        next_first = next_first * jnp.where(c < nc - 1, 1.0, 0.0)
        xm1 = jnp.where(row == 0, prev_last, pltpu.roll(x, 1, 0))
        xp1 = jnp.where(row == CHUNK - 1, next_first, pltpu.roll(x, CHUNK - 1, 0))
        y = w_ref[0:1, :] * xm1 + w_ref[1:2, :] * x + w_ref[2:3, :] * xp1 + b_ref[...]
        y = y * _sigmoid(y) * scale
        dst_ref[pl.ds(t0, CHUNK), :] = y.astype(BF16)

    def conv_body(c, carry):
        conv_chunk(q_ref, qc_ref, cwq_ref, cbq_ref, c, 1.0)
        conv_chunk(k_ref, kc_ref, cwk_ref, cbk_ref, c, HEAD_DIM ** -0.5)
        return carry

    lax.fori_loop(0, nc, conv_body, 0, unroll=MIX_UNROLL)

    def gate_tiles(t0, d):
        g = g_ref[pl.ds(t0, CHUNK), :] + gb_ref[...]
        lf = _log_sigmoid(g)
        a = _dot(tri_ref[d], _split(lf, 0))
        return g, a

    def v_aug_at(t0):
        return jnp.concatenate([v_ref[pl.ds(t0, CHUNK), :], ones_aug], axis=1)

    for d in (0, 1):
        xb_li = xa_ref[d, 0]
        xb_lf = xa_ref[d, 1]
        tri = _tri_mask(d)

        def scan_body(c, carry, d=d, xb_li=xb_li, xb_lf=xb_lf):
            s_st, m_st = carry
            cc = c if d == 0 else nc - 1 - c
            t0 = pl.multiple_of(cc * CHUNK, CHUNK)
            g, a = gate_tiles(t0, d)
            li_b = _dot(_split(g, 1), xb_li)
            a_b = _dot(_split(a, 1), xb_lf)
            a_tot = a_b[CHUNK - 1:CHUNK, :] if d == 0 else a_b[0:1, :]
            w_b = li_b + a_tot - a_b
            m_loc = jnp.max(w_b, axis=0, keepdims=True)
            m_new = jnp.maximum(a_tot + m_st, m_loc)
            s_old = jnp.exp(a_tot + m_st - m_new)
            s_new = jnp.exp(m_loc - m_new)
            e_b = jnp.exp(w_b - m_loc)
            k_e = (kc_ref[pl.ds(t0, CHUNK), :].astype(F32) * e_b).astype(BF16)
            kv = _dot_tn(k_e, v_aug_at(t0)) * bdm_ref[...]
            sst_ref[cc] = s_st.astype(BF16)
            mprev_ref[cc] = m_st
            s_old2 = jnp.concatenate([s_old, s_old], axis=1)
            s_new2 = jnp.concatenate([s_new, s_new], axis=1)
            return s_old2 * s_st + s_new2 * kv, m_new

        lax.fori_loop(0, nc, scan_body,
                      (jnp.zeros((CHUNK, 2 * LANES), F32), jnp.zeros((1, LANES), F32)),
                      unroll=MIX_UNROLL)

        def out_body(c, carry, d=d, tri=tri):
            t0 = pl.multiple_of(c * CHUNK, CHUNK)
            g, a = gate_tiles(t0, d)
            q = qc_ref[pl.ds(t0, CHUNK), :]
            k = kc_ref[pl.ds(t0, CHUNK), :]
            v_aug = v_aug_at(t0)
            m_prev = mprev_ref[c]
            q_s = _dot(q, sst_ref[c])
            a_s = _split(a, 1)
            w_s = _split(jnp.where(lane < GATE_LF_LANE, g, a), 1)
            nums, dens, mrows = [], [], []
            for j in (0, 1):
                acol = _dot(a_s, xa_ref[d, 2 + j])
                rrow = _dot_nt(xr_ref[d, j], w_s)
                dlog = jnp.where(tri, acol - rrow, NEG)
                ginter = acol[:, 0:1] + m_prev[:, HEAD_DIM * j:HEAD_DIM * j + 1]
                mrow = jnp.maximum(jnp.max(dlog, axis=1, keepdims=True), ginter)
                qj = jnp.where(lane_lo if j == 0 else jnp.logical_not(lane_lo), q,
                               jnp.zeros_like(q))
                p = _dot_nt(qj, k) * jnp.exp(dlog - mrow)
                pv = _dot(p.astype(BF16), v_aug)
                sint = jnp.exp(ginter - mrow)
                nums.append(pv[:, :LANES] + sint * q_s[:, :LANES])
                dens.append(pv[:, LANES:] + sint * q_s[:, LANES:])
                mrows.append(jnp.broadcast_to(mrow, (CHUNK, LANES)))
            num = jnp.where(lane_lo, nums[0], nums[1])
            den = jnp.where(lane_lo, dens[0], dens[1])
            mrow_b = jnp.where(lane_lo, mrows[0], mrows[1])
            h = num / jnp.maximum(jnp.abs(den), jnp.exp(-mrow_b))
            if d == 0:
                hf_ref[pl.ds(t0, CHUNK), :] = h
            else:
                h = h + hf_ref[pl.ds(t0, CHUNK), :]
                y = _head_norm(h, bd2_ref) * nw_ref[...]
                y = y * _sigmoid(o_ref[pl.ds(t0, CHUNK), :].astype(F32))
                out_ref[pl.ds(t0, CHUNK), :] = y.astype(BF16)
            return carry

        lax.fori_loop(0, nc, out_body, 0, unroll=MIX_UNROLL)


def _mlstm(z3, gates3, gbias, conv_w, conv_b, norm_w, consts):
    b, seq, _ = z3.shape
    nc = seq // CHUNK
    npair = H_MLSTM // 2

    def zspec(blk0):
        return pl.BlockSpec((None, seq, LANES), lambda bi, p, blk0=blk0: (bi, 0, blk0 + p))

    kern = functools.partial(_mlstm_kernel, seq=seq)
    return pl.pallas_call(
        kern,
        out_shape=jax.ShapeDtypeStruct((b, seq, W_MLSTM), BF16),
        grid=(b, npair),
        in_specs=[zspec(ZB_MQ), zspec(ZB_MK), zspec(ZB_MV), zspec(ZB_MO),
                  pl.BlockSpec((None, seq, LANES), lambda bi, p: (bi, 0, 0)),
                  pl.BlockSpec((1, LANES), lambda bi, p: (0, 0)),
                  pl.BlockSpec((3, LANES), lambda bi, p: (0, p)),
                  pl.BlockSpec((3, LANES), lambda bi, p: (0, npair + p)),
                  pl.BlockSpec((1, LANES), lambda bi, p: (0, p)),
                  pl.BlockSpec((1, LANES), lambda bi, p: (0, npair + p)),
                  pl.BlockSpec((1, LANES), lambda bi, p: (0, p)),
                  pl.BlockSpec((None, 2, 4, 2 * LANES, LANES), lambda bi, p: (p, 0, 0, 0, 0)),
                  pl.BlockSpec((None, 2, 2, LANES, 2 * LANES), lambda bi, p: (p, 0, 0, 0, 0)),
                  pl.BlockSpec((2, CHUNK, 2 * CHUNK), lambda bi, p: (0, 0, 0)),
                  pl.BlockSpec((LANES, 2 * LANES), lambda bi, p: (0, 0)),
                  pl.BlockSpec((2 * LANES, LANES), lambda bi, p: (0, 0))],
        out_specs=pl.BlockSpec((None, seq, LANES), lambda bi, p: (bi, 0, p)),
        scratch_shapes=[pltpu.VMEM((seq, LANES), BF16),
                        pltpu.VMEM((seq, LANES), BF16),
                        pltpu.VMEM((seq, LANES), F32),
                        pltpu.VMEM((nc, CHUNK, 2 * LANES), BF16),
                        pltpu.VMEM((nc, 1, LANES), F32)],
        compiler_params=_cparams(("arbitrary", "arbitrary")),
        name="mlstm",
    )(z3, z3, z3, z3, gates3, gbias, conv_w, conv_w, conv_b, conv_b, norm_w,
      consts["xa"], consts["xr"], consts["tri"], consts["bdm2"], consts["bd2"])


def _ret_kernel(q_ref, k_ref, v_ref, gt_ref, rd_ref, nw_ref, ca_ref, sa_ref, cb_ref, sb_ref,
                sgn_ref, perm_ref, bdm_ref, bd2_ref, out_ref,
                qr_ref, kr_ref, hf_ref, sst_ref, tab_ref, *, seq):
    nc = seq // CHUNK
    lane_lo = _lane_lt64()
    rowf =lax.broadcasted_iota(jnp.int32, (CHUNK, LANES), 0).astype(F32)
    ri = lax.broadcasted_iota(jnp.int32, (CHUNK, CHUNK), 0)
    ci = lax.broadcasted_iota(jnp.int32, (CHUNK, CHUNK), 1)

    def rot_body(c, carry):
        t0 = pl.multiple_of(c * CHUNK, CHUNK)
        cb = cb_ref[c]
        sb = sb_ref[c]
        cos = ca_ref[...] * cb - sa_ref[...] * sb
        sin = (sa_ref[...] * cb + ca_ref[...] * sb) * sgn_ref[...]
        for src, dst in ((q_ref, qr_ref), (k_ref, kr_ref)):
            x = src[pl.ds(t0, CHUNK), :]
            xs = _dot(x, perm_ref[...])
            dst[pl.ds(t0, CHUNK), :] = (x.astype(F32) * cos + xs * sin).astype(BF16)
        return carry

    lax.fori_loop(0, nc, rot_body, 0, unroll=MIX_UNROLL)

    for d in (0, 1):
        lg_b = -jnp.exp(rd_ref[d])
        if d == 0:
            kdec = jnp.exp(lg_b * (CHUNK - 1.0 - rowf))
            qdec = jnp.exp(lg_b * (rowf + 1.0))
            dist = (ri - ci).astype(F32)
        else:
            kdec = jnp.exp(lg_b * rowf)
            qdec = jnp.exp(lg_b * (CHUNK - rowf))
            dist = (ci - ri).astype(F32)
        cdec = jnp.exp(lg_b * float(CHUNK))
        tab_ref[0] = kdec
        tab_ref[1] = qdec
        for j in (0, 1):
            lg_j = lg_b[:, HEAD_DIM * j:HEAD_DIM * j + 1]
            tab_ref[2 + j] = jnp.where(dist >= 0.0, jnp.exp(lg_j * jnp.maximum(dist, 0.0)), 0.0)

        def scan_body(c, s_st, d=d, cdec=cdec):
            cc = c if d == 0 else nc - 1 - c
            t0 = pl.multiple_of(cc * CHUNK, CHUNK)
            k_d = (kr_ref[pl.ds(t0, CHUNK), :].astype(F32) * tab_ref[0]).astype(BF16)
            kv = _dot_tn(k_d, v_ref[pl.ds(t0, CHUNK), :]) * bdm_ref[...]
            sst_ref[cc] = s_st.astype(BF16)
            return cdec * s_st + kv

        lax.fori_loop(0, nc, scan_body, jnp.zeros((CHUNK, LANES), F32), unroll=RET_UNROLL)

        def out_body(c, carry, d=d):
            t0 = pl.multiple_of(c * CHUNK, CHUNK)
            q = qr_ref[pl.ds(t0, CHUNK), :]
            k = kr_ref[pl.ds(t0, CHUNK), :]
            v = v_ref[pl.ds(t0, CHUNK), :]
            y_inter = _dot((q.astype(F32) * tab_ref[1]).astype(BF16), sst_ref[c])
            ys = []
            for j in (0, 1):
                qj = jnp.where(lane_lo if j == 0 else jnp.logical_not(lane_lo), q,
                               jnp.zeros_like(q))
                s = _dot_nt(qj, k) * tab_ref[2 + j]
                ys.append(_dot(s.astype(BF16), v))
            h = jnp.where(lane_lo, ys[0], ys[1]) + y_inter
            if d == 0:
                hf_ref[pl.ds(t0, CHUNK), :] = h
            else:
                h = h + hf_ref[pl.ds(t0, CHUNK), :]
                y = _head_norm(h, bd2_ref) * nw_ref[...]
                gt = gt_ref[pl.ds(t0, CHUNK), :].astype(F32)
                y = y * (gt * _sigmoid(gt))
                out_ref[pl.ds(t0, CHUNK), :] = y.astype(BF16)
            return carry

        lax.fori_loop(0, nc, out_body, 0, unroll=RET_UNROLL)


def _retention(z3, rd, norm_w, consts, rope):
    b, seq, _ = z3.shape
    nc = seq // CHUNK
    npair = H_RET // 2

    def zspec(blk0):
        return pl.BlockSpec((None, seq, LANES), lambda bi, p, blk0=blk0: (bi, 0, blk0 + p))

    def full2(shape):
        return pl.BlockSpec(shape, lambda bi, p: (0, 0))

    kern = functools.partial(_ret_kernel, seq=seq)
    return pl.pallas_call(
        kern,
        out_shape=jax.ShapeDtypeStruct((b, seq, W_RET), BF16),
        grid=(b, npair),
        in_specs=[zspec(ZB_RQ), zspec(ZB_RK), zspec(ZB_RV), zspec(ZB_RG),
                  pl.BlockSpec((None, 2, 1, LANES), lambda bi, p: (p, 0, 0, 0)),
                  pl.BlockSpec((1, LANES), lambda bi, p: (0, p)),
                  full2((CHUNK, LANES)), full2((CHUNK, LANES)),
                  pl.BlockSpec((nc, 1, LANES), lambda bi, p: (0, 0, 0)),
                  pl.BlockSpec((nc, 1, LANES), lambda bi, p: (0, 0, 0)),
                  full2((1, LANES)), full2((LANES, LANES)), full2((LANES, LANES)),
                  full2((2 * LANES, LANES))],
        out_specs=pl.BlockSpec((None, seq, LANES), lambda bi, p: (bi, 0, p)),
        scratch_shapes=[pltpu.VMEM((seq, LANES), BF16),
                        pltpu.VMEM((seq, LANES), BF16),
                        pltpu.VMEM((seq, LANES), F32),
                        pltpu.VMEM((nc, CHUNK, LANES), BF16),
                        pltpu.VMEM((4, CHUNK, LANES), F32)],
        compiler_params=_cparams(("arbitrary", "arbitrary")),
        name="retention",
    )(z3, z3, z3, z3, rd, norm_w, rope["ca"], rope["sa"], rope["cb"], rope["sb"],
      consts["sgn"], consts["perm"], consts["bdm"], consts["bd2"])


def _na_kernel(q_ref, k_ref, v_ref, bias_ref, out_ref, *, seq):
    rows = seq // GRID_W
    ng = rows // NA_QROWS
    nq = NA_QROWS * GRID_W
    nk = NA_KROWS * GRID_W
    lane_lo = lax.broadcasted_iota(jnp.int32, (nq, LANES), 1) < HEAD_DIM
    ones_aug = jnp.ones((nk, LANES), BF16)

    def body(g, carry):
        t0 = pl.multiple_of(g * nq, nq)
        base = jnp.clip(g * NA_QROWS - NA_ROWS // 2, 0, rows - NA_KROWS)
        k0 = pl.multiple_of(base * GRID_W, GRID_W)
        case = jnp.where(g == 0, 0, jnp.where(g == ng - 1, 2, 1))
        q = q_ref[pl.ds(t0, nq), :]
        kk = k_ref[pl.ds(k0, nk), :]
        v_aug = jnp.concatenate([v_ref[pl.ds(k0, nk), :], ones_aug], axis=1)
        outs = []
        for j in (0, 1):
            qj = jnp.where(lane_lo if j == 0 else jnp.logical_not(lane_lo), q,
                           jnp.zeros_like(q))
            s = _dot_nt(qj, kk) + bias_ref[case, j]
            m = jnp.max(s, axis=1, keepdims=True)
            e = jnp.exp(s - m)
            pv = _dot(e.astype(BF16), v_aug)
            outs.append(pv[:, :LANES] / pv[:, LANES:])
        out_ref[pl.ds(t0, nq), :] = jnp.where(lane_lo, outs[0], outs[1]).astype(BF16)
        return carry

    lax.fori_loop(0, ng, body, 0)


def _na(z3, bias):
    b, seq, _ = z3.shape
    npair = H_NA // 2
    nq = NA_QROWS * GRID_W
    nk = NA_KROWS * GRID_W

    def zspec(blk0):
        return pl.BlockSpec((None, seq, LANES), lambda bi, p, blk0=blk0: (bi, 0, blk0 + p))

    kern = functools.partial(_na_kernel, seq=seq)
    return pl.pallas_call(
        kern,
        out_shape=jax.ShapeDtypeStruct((b, seq, W_NA), BF16),
        grid=(b, npair),
        in_specs=[zspec(ZB_NQ), zspec(ZB_NK), zspec(ZB_NV),
                  pl.BlockSpec((None, 3, 2, nq, nk), lambda bi, p: (p, 0, 0, 0, 0))],
        out_specs=pl.BlockSpec((None, seq, LANES), lambda bi, p: (bi, 0, p)),
        compiler_params=_cparams(("arbitrary", "arbitrary")),
        name="natten",
    )(z3, z3, z3, bias)


def _na_index_tables(seq):
    rows = seq // GRID_W
    ng = rows // NA_QROWS
    tabs = []
    for g in (0, 1, ng - 1):
        base = int(np.clip(g * NA_QROWS - NA_ROWS // 2, 0, rows - NA_KROWS))
        qr = g * NA_QROWS + np.arange(NA_QROWS)[:, None, None, None]
        qc = np.arange(GRID_W)[None, :, None, None]
        kr = base + np.arange(NA_KROWS)[None, None, :, None]
        kc = np.arange(GRID_W)[None, None, None, :]
        rstart = np.clip(qr - NA_ROWS // 2, 0, rows - NA_ROWS)
        cstart = np.clip(qc - NA_COLS // 2, 0, GRID_W - NA_COLS)
        ok = (kr >= rstart) & (kr < rstart + NA_ROWS) & (kc >= cstart) & (kc < cstart + NA_COLS)
        rel_r = np.clip(kr - qr + NA_ROWS - 1, 0, 2 * NA_ROWS - 2)
        rel_c = np.clip(kc - qc + NA_COLS - 1, 0, 2 * NA_COLS - 2)
        shp = (NA_QROWS * GRID_W, NA_KROWS * GRID_W)
        full = (NA_QROWS, GRID_W, NA_KROWS, GRID_W)
        tabs.append((np.broadcast_to(ok, full).reshape(shp),
                     np.broadcast_to(rel_r, full).reshape(shp),
                     np.broadcast_to(rel_c, full).reshape(shp)))
    ok = np.stack([t[0] for t in tabs])
    rr = np.stack([t[1] for t in tabs])
    rc = np.stack([t[2] for t in tabs])
    return ok, rr, rc


def _na_bias(rpb, seq):
    ok, rr, rc = _na_index_tables(seq)
    bias = rpb.astype(F32)[:, rr, rc]
    bias = jnp.where(ok[None], bias, NEG)
    h = rpb.shape[0]
    return bias.reshape(h // 2, 2, 3, bias.shape[2], bias.shape[3]).transpose(0, 2, 1, 3, 4)


def _proj_out_kernel(ym_ref, yr_ref, yn_ref, x_ref, wm_ref, wr_ref, wn_ref, g_ref, rw_ref,
                     rb_ref, lst_ref, x2_ref, xn_ref, route_ref, cnt_ref):
    @pl.when(pl.program_id(0) == 0)
    def _():
        cnt_ref[...] = jnp.zeros_like(cnt_ref)

    acc = _dot(ym_ref[...], wm_ref[...]) + _dot(yr_ref[...], wr_ref[...])
    acc = acc + _dot(yn_ref[...], wn_ref[...])
    x2 = x_ref[...] + acc
    x2_ref[...] = x2
    ms = jnp.mean(x2 * x2, axis=-1, keepdims=True)
    xn = x2 * lax.rsqrt(ms + RMS_EPS) * g_ref[...]
    xn_ref[...] = xn
    logits = jnp.dot(xn, rw_ref[...], preferred_element_type=F32,
                     precision=lax.Precision.HIGHEST) + rb_ref[...]
    tm = logits.shape[0]
    lane = lax.broadcasted_iota(jnp.int32, (tm, LANES), 1).astype(F32)
    glog = jnp.where(lane < N_GROUPS, logits, NEG)
    gmax = jnp.max(glog, axis=1, keepdims=True)
    grp = jnp.min(jnp.where(glog == gmax, lane, float(LANES)), axis=1, keepdims=True)
    p_grp = 1.0 / jnp.sum(jnp.exp(glog - gmax), axis=1, keepdims=True)
    lo = N_GROUPS + grp * EXPERTS_PER_GROUP
    ein = jnp.where((lane >= lo) & (lane < lo + EXPERTS_PER_GROUP), logits, NEG)
    v1 = jnp.max(ein, axis=1, keepdims=True)
    i1 = jnp.min(jnp.where(ein == v1, lane, float(LANES)), axis=1, keepdims=True)
    ein2 = jnp.where(lane == i1, NEG, ein)
    v2 = jnp.max(ein2, axis=1, keepdims=True)
    i2 = jnp.min(jnp.where(ein2 == v2, lane, float(LANES)), axis=1, keepdims=True)
    e21 = jnp.exp(v2 - v1)
    g1 = p_grp / (1.0 + e21)
    g2 = p_grp * e21 / (1.0 + e21)
    oh1 = jnp.where(lane == i1 - N_GROUPS, 1.0, 0.0)
    oh2 = jnp.where(lane == i2 - N_GROUPS, 1.0, 0.0)
    both = oh1 + oh2
    before = _dot(lst_ref[...], both.astype(BF16)) + cnt_ref[...]
    rk1 = jnp.sum(oh1 * before, axis=1, keepdims=True)
    rk2 = jnp.sum(oh2 * before, axis=1, keepdims=True)
    cnt_ref[...] += jnp.sum(both, axis=0, keepdims=True)
    vals = (i1 - N_GROUPS, i2 - N_GROUPS, g1, g2, rk1, rk2)
    route = jnp.zeros_like(logits)
    for c, val in enumerate(vals):
        route = jnp.where(lane == c, val, route)
    route_ref[...] = route


def _proj_out(ym, yr, yn, x2, wm, wr, wn, g, rw, rb):
    n = x2.shape[0]
    tm = PROJ_TM

    def rows(w):
        return pl.BlockSpec((tm, w), lambda i: (i, 0))

    def full(shape):
        return pl.BlockSpec(shape, lambda i: (0, 0))

    ii = np.arange(tm)
    lstrict = jnp.asarray(ii[None, :] < ii[:, None], BF16)
    return pl.pallas_call(
        _proj_out_kernel,
        out_shape=(jax.ShapeDtypeStruct((n, D_MODEL), F32),
                   jax.ShapeDtypeStruct((n, D_MODEL), F32),
                   jax.ShapeDtypeStruct((n, LANES), F32),
                   jax.ShapeDtypeStruct((1, LANES), F32)),
        grid=(n // tm,),
        in_specs=[rows(W_MLSTM), rows(W_RET), rows(W_NA), rows(D_MODEL),
                  full((W_MLSTM, D_MODEL)), full((W_RET, D_MODEL)), full((W_NA, D_MODEL)),
                  full((1, D_MODEL)), full((D_MODEL, LANES)), full((1, LANES)), full((tm, tm))],
        out_specs=(rows(D_MODEL), rows(D_MODEL), rows(LANES), full((1, LANES))),
        compiler_params=_cparams(("arbitrary",)),
        name="proj_out_router",
    )(ym, yr, yn, x2, wm, wr, wn, g, rw, rb, lstrict)


def _dispatch_kernel(pstart_ref, pend_ref, cnt_ref, nused_ref, meta_hbm, xn_ref, xs_hbm,
                     meta_smem, zero_buf, sem_meta, sem_rows, sem_zero):
    i = pl.program_id(0)
    nsteps = pl.num_programs(0)
    slot = lax.rem(i, 2)
    n_blocks = xs_hbm.shape[0] // MOE_BLK

    def meta_copy(step, s):
        return pltpu.make_async_copy(meta_hbm.at[step], meta_smem.at[s], sem_meta.at[s])

    def zero_copy(blk):
        start = pl.multiple_of(blk * MOE_BLK, MOE_BLK)
        return pltpu.make_async_copy(zero_buf, xs_hbm.at[pl.ds(start, MOE_BLK), :], sem_zero)

    @pl.when(i == 0)
    def _():
        meta_copy(0, 0).start()
        zero_buf[...] = jnp.zeros_like(zero_buf)

        def zstart(e, carry):
            @pl.when(cnt_ref[e] > 0)
            def _():
                zero_copy(pend_ref[e] // MOE_BLK - 1).start()
            return carry

        def zwait(e, carry):
            @pl.when(cnt_ref[e] > 0)
            def _():
                zero_copy(0).wait()
            return carry

        lax.fori_loop(0, N_EXPERTS, zstart, 0)
        lax.fori_loop(nused_ref[0], n_blocks, lambda b, c: (zero_copy(b).start(), c)[1], 0)
        lax.fori_loop(0, N_EXPERTS, zwait, 0)
        lax.fori_loop(nused_ref[0], n_blocks, lambda b, c: (zero_copy(0).wait(), c)[1], 0)

    meta_copy(i, slot).wait()

    @pl.when(i + 1 < nsteps)
    def _():
        meta_copy(i + 1, 1 - slot).start()

    def row_copy(r, dst):
        return pltpu.make_async_copy(xn_ref.at[pl.ds(r, 1), :], xs_hbm.at[pl.ds(dst, 1), :],
                                     sem_rows)

    def issue(r, carry):
        for kk in range(TOP_K):
            e = meta_smem[slot, 4 * r + kk]
            rk = meta_smem[slot, 4 * r + TOP_K + kk]
            row_copy(r, pstart_ref[e] + rk).start()
        return carry

    lax.fori_loop(0, COMB_TM, issue, 0, unroll=8)

    def drain(r, carry):
        for kk in range(TOP_K):
            row_copy(r, 0).wait()
        return carry

    lax.fori_loop(0, COMB_TM, drain, 0, unroll=8)


def _dispatch(pstart, pend, counts, n_used, meta, xn, n_slots):
    n = xn.shape[0]
    tm = COMB_TM
    grid_spec = pltpu.PrefetchScalarGridSpec(
        num_scalar_prefetch=4,
        grid=(n // tm,),
        in_specs=[pl.BlockSpec(memory_space=pl.ANY),
                  pl.BlockSpec((tm, D_MODEL), lambda i, *_: (i, 0))],
        out_specs=pl.BlockSpec(memory_space=pl.ANY),
        scratch_shapes=[pltpu.SMEM((2, 4 * tm), jnp.int32),
                        pltpu.VMEM((MOE_BLK, D_MODEL), F32),
                        pltpu.SemaphoreType.DMA((2,)),
                        pltpu.SemaphoreType.DMA,
                        pltpu.SemaphoreType.DMA],
    )
    return pl.pallas_call(
        _dispatch_kernel,
        out_shape=jax.ShapeDtypeStruct((n_slots, D_MODEL), F32),
        grid_spec=grid_spec,
        compiler_params=_cparams(("arbitrary",)),
        name="dispatch",
    )(pstart, pend, counts, n_used, meta, xn)


def _expert_kernel(be_ref, nused_ref, xs_ref, w1_ref, w3_ref, w2_ref, y_ref):
    i = pl.program_id(0)

    @pl.when(i < nused_ref[0])
    def _():
        xb = xs_ref[...].astype(BF16)
        h1 = _dot(xb, w1_ref[...])
        h3 = _dot(xb, w3_ref[...])
        hb = (h1 * _sigmoid(h1) * h3).astype(BF16)
        y_ref[...] = _dot(hb, w2_ref[...])

    @pl.when(i >= nused_ref[0])
    def _():
        y_ref[...] = jnp.zeros_like(y_ref)


def _experts(block_expert, n_used, xs, w1, w3, w2):
    n_blocks = xs.shape[0] // MOE_BLK

    def wspec(shape):
        return pl.BlockSpec(shape, lambda i, be, nu: (be[i], 0, 0))

    grid_spec = pltpu.PrefetchScalarGridSpec(
        num_scalar_prefetch=2,
        grid=(n_blocks,),
        in_specs=[pl.BlockSpec((MOE_BLK, D_MODEL),
                               lambda i, be, nu: (jnp.minimum(i, nu[0] - 1), 0)),
                  wspec((None, D_MODEL, EXPERT_FF)), wspec((None, D_MODEL, EXPERT_FF)),
                  wspec((None, EXPERT_FF, D_MODEL))],
        out_specs=pl.BlockSpec((MOE_BLK, D_MODEL), lambda i, be, nu: (i, 0)),
    )
    return pl.pallas_call(
        _expert_kernel,
        out_shape=jax.ShapeDtypeStruct((n_blocks * MOE_BLK, D_MODEL), F32),
        grid_spec=grid_spec,
        compiler_params=_cparams(("arbitrary",)),
        name="experts",
    )(block_expert, n_used, xs, w1, w3, w2)


def _combine_kernel(pstart_ref, meta_hbm, yb_hbm, x_ref, route_ref, g_ref, out_ref,
                    meta_smem, ybuf, sem_meta, sem_rows, *, final_norm):
    i = pl.program_id(0)
    nsteps = pl.num_programs(0)
    slot = lax.rem(i, 2)

    def meta_copy(step, s):
        return pltpu.make_async_copy(meta_hbm.at[step], meta_smem.at[s], sem_meta.at[s])

    def row_copy(src, s, kk, r):
        return pltpu.make_async_copy(yb_hbm.at[pl.ds(src, 1), :],
                                     ybuf.at[s, kk, pl.ds(r, 1), :], sem_rows.at[s])

    def issue_tile(s):
        def issue(r, carry):
            for kk in range(TOP_K):
                e = meta_smem[s, 4 * r + kk]
                rk = meta_smem[s, 4 * r + TOP_K + kk]
                row_copy(pstart_ref[e] + rk, s, kk, r).start()
            return carry

        lax.fori_loop(0, COMB_TM, issue, 0, unroll=8)

    @pl.when(i == 0)
    def _():
        meta_copy(0, 0).start()
        meta_copy(0, 0).wait()
        issue_tile(0)

        @pl.when(nsteps > 1)
        def _():
            meta_copy(1, 1).start()

    @pl.when(i + 1 < nsteps)
    def _():
        meta_copy(i + 1, 1 - slot).wait()
        issue_tile(1 - slot)

    @pl.when(i + 2 < nsteps)
    def _():
        meta_copy(i + 2, slot).start()

    def drain(r, carry):
        for kk in range(TOP_K):
            row_copy(0, slot, kk, r).wait()
        return carry

    lax.fori_loop(0, COMB_TM, drain, 0, unroll=8)
    route = route_ref[...]
    y = x_ref[...]
    for kk in range(TOP_K):
        y = y + route[:, 2 + kk:3 + kk] * ybuf[slot, kk]
    if final_norm:
        ms = jnp.mean(y * y, axis=-1, keepdims=True)
        y = y * lax.rsqrt(ms + RMS_EPS) * g_ref[...]
    out_ref[...] = y


def _combine(pstart, meta, yb, x2, route, g, final_norm):
    n = x2.shape[0]
    tm = COMB_TM
    kern = functools.partial(_combine_kernel, final_norm=final_norm)
    grid_spec = pltpu.PrefetchScalarGridSpec(
        num_scalar_prefetch=1,
        grid=(n // tm,),
        in_specs=[pl.BlockSpec(memory_space=pl.ANY),
                  pl.BlockSpec(memory_space=pl.ANY),
                  pl.BlockSpec((tm, D_MODEL), lambda i, *_: (i, 0)),
                  pl.BlockSpec((tm, LANES), lambda i, *_: (i, 0)),
                  pl.BlockSpec((1, D_MODEL), lambda i, *_: (0, 0))],
        out_specs=pl.BlockSpec((tm, D_MODEL), lambda i, *_: (i, 0)),
        scratch_shapes=[pltpu.SMEM((2, 4 * tm), jnp.int32),
                        pltpu.VMEM((2, TOP_K, tm, D_MODEL), F32),
                        pltpu.SemaphoreType.DMA((2,)),
                        pltpu.SemaphoreType.DMA((2,))],
    )
    return pl.pallas_call(
        kern,
        out_shape=jax.ShapeDtypeStruct((n, D_MODEL), F32),
        grid_spec=grid_spec,
        compiler_params=_cparams(("arbitrary",)),
        name="combine",
    )(pstart, meta, yb, x2, route, g)


def _dispatch_layout(route, counts_f, n):
    counts = counts_f[0, :N_EXPERTS].astype(jnp.int32)
    n_blocks = n * TOP_K // MOE_BLK + N_EXPERTS
    padded = (counts + MOE_BLK - 1) // MOE_BLK * MOE_BLK
    pend = jnp.cumsum(padded).astype(jnp.int32)
    pstart = pend - padded
    n_used = pend[-1] // MOE_BLK
    blk_start = jnp.arange(n_blocks, dtype=jnp.int32) * MOE_BLK
    blk_start = jnp.minimum(blk_start, (n_used - 1) * MOE_BLK)
    block_expert = jnp.sum((pend[None, :] <= blk_start[:, None]).astype(jnp.int32), axis=1)
    block_expert = jnp.clip(block_expert, 0, N_EXPERTS - 1).astype(jnp.int32)
    meta = jnp.concatenate([route[:, 0:TOP_K], route[:, 4:4 + TOP_K]], axis=1).astype(jnp.int32)
    meta = meta.reshape(n // COMB_TM, 4 * COMB_TM)
    return block_expert, n_used.reshape(1), pstart, pend, counts, meta, n_blocks * MOE_BLK


def _const_tables():
    lane = np.arange(LANES)
    grp = lane // HEAD_DIM
    r2 = np.arange(2 * LANES) % LANES
    npair = H_MLSTM // 2
    xa = np.zeros((npair, 2, 4, 2 * LANES, LANES), np.float32)
    xr = np.zeros((npair, 2, 2, LANES, 2 * LANES), np.float32)
    for p in range(npair):
        for d in range(2):
            li_lane = d * H_MLSTM + 2 * p + grp
            lf_lane = GATE_LF_LANE + li_lane
            xa[p, d, 0] = (r2[:, None] == li_lane[None, :])
            xa[p, d, 1] = (r2[:, None] == lf_lane[None, :])
            for j in range(2):
                lij = d * H_MLSTM + 2 * p + j
                lfj = GATE_LF_LANE + lij
                xa[p, d, 2 + j] = np.broadcast_to((r2 == lfj)[:, None], (2 * LANES, LANES))
                xr[p, d, j] = np.broadcast_to(((r2 == lfj).astype(np.float32)
                                               - (r2 == lij).astype(np.float32))[None, :],
                                              (LANES, 2 * LANES))
    ii = np.arange(CHUNK)
    low = (ii[None, :] <= ii[:, None]).astype(np.float32)
    upp = (ii[None, :] >= ii[:, None]).astype(np.float32)
    tri = np.stack([np.concatenate([low, low], 1), np.concatenate([upp, upp], 1)])
    bd = (grp[:, None] == grp[None, :]).astype(np.float32)
    perm = np.zeros((LANES, LANES), np.float32)
    half = HEAD_DIM // 2
    src = (lane // HEAD_DIM) * HEAD_DIM + (lane % HEAD_DIM + half) % HEAD_DIM
    perm[src, lane] = 1.0
    sgn = np.where(lane % HEAD_DIM < half, -1.0, 1.0).astype(np.float32)[None, :]
    return {
        "xa": jnp.asarray(xa, BF16), "xr": jnp.asarray(xr, BF16), "tri": jnp.asarray(tri, BF16),
        "bdm": jnp.asarray(bd, F32), "bdm2": jnp.asarray(np.concatenate([bd, bd], 1), F32),
        "bd2": jnp.asarray(np.concatenate([bd, bd], 0), BF16),
        "perm": jnp.asarray(perm, BF16), "sgn": jnp.asarray(sgn, F32),
    }


def _rope_tables(seq):
    half = HEAD_DIM // 2
    nc = seq // CHUNK
    inv_freq = ROPE_BASE ** (-np.arange(half, dtype=np.float64) / half)
    freq = inv_freq[np.arange(LANES) % half]
    ang_a = np.arange(CHUNK, dtype=np.float64)[:, None] * freq[None, :]
    ang_b = (np.arange(nc, dtype=np.float64) * CHUNK)[:, None] * freq[None, :]
    return {"ca": jnp.asarray(np.cos(ang_a), F32), "sa": jnp.asarray(np.sin(ang_a), F32),
            "cb": jnp.asarray(np.cos(ang_b)[:, None, :], F32),
            "sb": jnp.asarray(np.sin(ang_b)[:, None, :], F32)}


def _layer_weights(l, w_in, mlstm_gate_bias, ret_decay, w_out, router_group_w, router_group_b,
                   router_expert_w, router_expert_b):
    sizes = (W_MLSTM,) * 4 + (4 * H_MLSTM,) + (W_RET,) * 4 + (W_NA,) * 3
    offs = np.concatenate([[0], np.cumsum(sizes)])
    col = lambda i: w_in[l][:, int(offs[i]):int(offs[i + 1])]
    mq, mk, mv, mo, mg, rq, rk, rv, rg, nq, nk, nv = [col(i) for i in range(12)]
    scale = HEAD_DIM ** -0.5
    zpad = jnp.zeros((D_MODEL, 4), F32)
    gate_w = jnp.concatenate([mg[:, :2 * H_MLSTM], zpad, mg[:, 2 * H_MLSTM:],
                              jnp.zeros((D_MODEL, LANES - GATE_LF_LANE - 2 * H_MLSTM), F32)], axis=1)
    w_all = jnp.concatenate([mq, mk, mv, mo, rq, rk * scale, rv, rg, nq * scale, nk, nv, gate_w],
                            axis=1).astype(BF16)
    gb = mlstm_gate_bias[l].astype(F32).reshape(-1)
    gbias = jnp.concatenate([gb[:2 * H_MLSTM], jnp.zeros((4,), F32), gb[2 * H_MLSTM:],
                             jnp.zeros((LANES - GATE_LF_LANE - 2 * H_MLSTM,), F32)])[None, :]
    rd = ret_decay[l].astype(F32)
    rd = jnp.repeat(rd.reshape(2, H_RET // 2, 2), HEAD_DIM, axis=2)
    rd = rd.transpose(1, 0, 2)[:, :, None, :]
    wo = w_out[l].astype(BF16)
    rw = jnp.concatenate([router_group_w[l], router_expert_w[l],
                          jnp.zeros((D_MODEL, LANES - N_GROUPS - N_EXPERTS), F32)], axis=1)
    rb = jnp.concatenate([router_group_b[l].astype(F32), router_expert_b[l].astype(F32),
                          jnp.zeros((LANES - N_GROUPS - N_EXPERTS,), F32)])[None, :]
    return w_all, gbias, rd, wo[:W_MLSTM], wo[W_MLSTM:W_MLSTM + W_RET], wo[W_MLSTM + W_RET:], rw, rb


def _encoder(x, consts, norm_mix, w_in, mlstm_conv_w, mlstm_conv_b, mlstm_gate_bias, mlstm_norm,
             ret_decay, ret_norm, na_rpb, w_out, norm_ffn, router_group_w, router_group_b,
             router_expert_w, router_expert_b, expert_w1, expert_w3, expert_w2, norm_final):
    b, seq, _ = x.shape
    n = b * seq
    depth = w_in.shape[0]
    rope = _rope_tables(seq)
    x2 = x.reshape(n, D_MODEL).astype(F32)
    for l in range(depth):
        w_all, gbias, rd, wo_m, wo_r, wo_n, rw, rb = _layer_weights(
            l, w_in, mlstm_gate_bias, ret_decay, w_out, router_group_w, router_group_b,
            router_expert_w, router_expert_b)
        z, gates = _proj_in(x2, norm_mix[l].astype(F32)[None, :], w_all)
        z3 = z.reshape(b, seq, Z_WIDTH)
        y_m = _mlstm(z3, gates.reshape(b, seq, LANES), gbias, mlstm_conv_w[l].astype(F32),
                     mlstm_conv_b[l].astype(F32)[None, :], mlstm_norm[l].astype(F32)[None, :], consts)
        y_r = _retention(z3, rd, ret_norm[l].astype(F32)[None, :], consts, rope)
        y_n = _na(z3, _na_bias(na_rpb[l], seq))
        x2, xn, route, counts_f = _proj_out(y_m.reshape(n, W_MLSTM), y_r.reshape(n, W_RET),
                                            y_n.reshape(n, W_NA), x2, wo_m, wo_r, wo_n,
                                            norm_ffn[l].astype(F32)[None, :], rw, rb)
        block_expert, n_used, pstart, pend, counts, meta, n_slots = _dispatch_layout(route, counts_f, n)
        xs = _dispatch(pstart, pend, counts, n_used, meta, xn, n_slots)
        yb = _experts(block_expert, n_used, xs,
                      expert_w1[l].astype(BF16), expert_w3[l].astype(BF16), expert_w2[l].astype(BF16))
        x2 = _combine(pstart, meta, yb, x2, route, norm_final.astype(F32)[None, :],
                      final_norm=(l == depth - 1))
    return x2.reshape(b, seq, D_MODEL)


def kernel(x_prompt, x_sample, norm_mix, w_in, mlstm_conv_w, mlstm_conv_b, mlstm_gate_bias,
           mlstm_norm, ret_decay, ret_norm, na_rpb, w_out, norm_ffn, router_group_w,
           router_group_b, router_expert_w, router_expert_b, expert_w1, expert_w3, expert_w2,
           norm_final):
    consts = _const_tables()
    weights = (norm_mix, w_in, mlstm_conv_w, mlstm_conv_b, mlstm_gate_bias, mlstm_norm,
               ret_decay, ret_norm, na_rpb, w_out, norm_ffn, router_group_w, router_group_b,
               router_expert_w, router_expert_b, expert_w1, expert_w3, expert_w2, norm_final)
    return (_encoder(x_prompt, consts, *weights), _encoder(x_sample, consts, *weights))
```

```python
import functools

import numpy as np
import jax
import jax.numpy as jnp
from jax import lax
from jax.experimental import pallas as pl
from jax.experimental.pallas import tpu as pltpu

F32 = jnp.float32
BF16 = jnp.bfloat16

D_MODEL = 1024
HEAD_DIM = 64
LANES = 128
CHUNK = 128
H_MLSTM = 6
H_RET = 6
H_NA = 4
W_MLSTM = H_MLSTM * HEAD_DIM
W_RET = H_RET * HEAD_DIM
W_NA = H_NA * HEAD_DIM
Z_WIDTH = 4 * W_MLSTM + 4 * W_RET + 3 * W_NA
GATE_ROWS = 2 * (H_MLSTM // 2) * 8
ROPE_BASE = 10000.0
GRID_W = 64
NA_ROWS = 8
NA_COLS = 16
NA_QROWS = 4
NA_KROWS = NA_QROWS + NA_ROWS
N_GROUPS = 4
EXPERTS_PER_GROUP = 8
N_EXPERTS = 32
TOP_K = 2
EXPERT_FF = 512
RMS_EPS = 1e-6
NEG = -1e30
VMEM_LIMIT = 56 * 1024 * 1024

PROJ_TM = 512
MOE_BLK = 256
COMB_TM = 256
MIX_UNROLL = 2
RET_UNROLL = 4
GATE_GROUP = 8

ZB_MQ, ZB_MK, ZB_MV, ZB_MO = 0, 3, 6, 9
ZB_RQ, ZB_RK, ZB_RV, ZB_RG = 12, 15, 18, 21
ZB_NQ, ZB_NK, ZB_NV = 24, 26, 28


def _dot(a, b):
    return jnp.dot(a, b, preferred_element_type=F32)


def _dot_nt(a, b):
    return lax.dot_general(a, b, (((1,), (1,)), ((), ())), preferred_element_type=F32)


def _dot_tn(a, b):
    return lax.dot_general(a, b, (((0,), (0,)), ((), ())), preferred_element_type=F32)


def _split(x, axis):
    hi = x.astype(BF16).astype(F32)
    return jnp.concatenate([hi, x - hi], axis=axis).astype(BF16)


def _sigmoid(x):
    return 1.0 / (1.0 + jnp.exp(-x))


def _log_sigmoid(x):
    return -(jnp.maximum(-x, 0.0) + jnp.log(1.0 + jnp.exp(-jnp.abs(x))))


def _cparams(sem):
    return pltpu.CompilerParams(dimension_semantics=sem, vmem_limit_bytes=VMEM_LIMIT)


def _proj_in_kernel(x_ref, g_ref, w_ref, wg_ref, z_ref, gate_ref):
    x = x_ref[...]
    ms = jnp.mean(x * x, axis=-1, keepdims=True)
    hx = (x * lax.rsqrt(ms + RMS_EPS) * g_ref[...]).astype(BF16)
    cw = 768
    for j in range(0, Z_WIDTH, cw):
        z_ref[:, j:j + cw] = _dot(hx, w_ref[:, j:j + cw]).astype(BF16)
    gate_ref[...] = _dot_nt(wg_ref[...], hx)


def _proj_in(x2, g, w, wg):
    n = x2.shape[0]
    return pl.pallas_call(
        _proj_in_kernel,
        out_shape=(jax.ShapeDtypeStruct((n, Z_WIDTH), BF16),
                   jax.ShapeDtypeStruct((GATE_ROWS, n), F32)),
        grid=(n // PROJ_TM,),
        in_specs=[pl.BlockSpec((PROJ_TM, D_MODEL), lambda i: (i, 0)),
                  pl.BlockSpec((1, D_MODEL), lambda i: (0, 0)),
                  pl.BlockSpec((D_MODEL, Z_WIDTH), lambda i: (0, 0)),
                  pl.BlockSpec((GATE_ROWS, D_MODEL), lambda i: (0, 0))],
        out_specs=(pl.BlockSpec((PROJ_TM, Z_WIDTH), lambda i: (i, 0)),
                   pl.BlockSpec((GATE_ROWS, PROJ_TM), lambda i: (0, i))),
        compiler_params=_cparams(("arbitrary",)),
        name="proj_in",
    )(x2, g, w, wg)


def _lane_lt64():
    return lax.broadcasted_iota(jnp.int32, (CHUNK, LANES), 1) < HEAD_DIM


def _tri_mask(d):
    r = lax.broadcasted_iota(jnp.int32, (CHUNK, CHUNK), 0)
    c = lax.broadcasted_iota(jnp.int32, (CHUNK, CHUNK), 1)
    return (c <= r) if d == 0 else (c >= r)


def _head_norm(h, bd2_ref):
    ms = _dot(_split(h * h, 1), bd2_ref[...]) * (1.0 / HEAD_DIM)
    return h * lax.rsqrt(ms + RMS_EPS)


def _mlstm_kernel(q_ref, k_ref, v_ref, o_ref, li_ref, lf_ref, bli_ref, blf_ref, cwq_ref, cwk_ref,
                  cbq_ref, cbk_ref, nw_ref, trif_ref, trib_ref, xe_ref, xc_ref, bdm_ref, bd2_ref,
                  out_ref, qc_ref, kc_ref, kvf_ref, kvb_ref, st_ref, rr_ref, aa_ref, cc_ref,
                  atot_ref, mloc_ref, mpf_ref, mpb_ref, ee_ref, *, seq):
    nc = seq // CHUNK
    lane_lo = _lane_lt64()
    row = lax.broadcasted_iota(jnp.int32, (CHUNK, LANES), 0)
    grow = lax.broadcasted_iota(jnp.int32, (8, CHUNK), 0)
    glane = lax.broadcasted_iota(jnp.int32, (8, CHUNK), 1)
    is_fwd = grow < 2
    ones_aug = jnp.ones((CHUNK, LANES), BF16)
    tri_f = _tri_mask(0)
    tri_b = _tri_mask(1)

    def conv_chunk(src_ref, w_ref, b_ref, c, scale):
        t0 = pl.multiple_of(c * CHUNK, CHUNK)
        x = src_ref[pl.ds(t0, CHUNK), :].astype(F32)
        tp = pl.multiple_of(jnp.maximum(t0 - 16, 0), 16)
        tn = pl.multiple_of(jnp.minimum(t0 + CHUNK, seq - 16), 16)
        prev_last = src_ref[pl.ds(tp, 16), :].astype(F32)[15:16, :]
        next_first = src_ref[pl.ds(tn, 16), :].astype(F32)[0:1, :]
        prev_last = prev_last * jnp.where(c > 0, 1.0, 0.0)
        next_first = next_first * jnp.where(c < nc - 1, 1.0, 0.0)
        xm1 = jnp.where(row == 0, prev_last, pltpu.roll(x, 1, 0))
        xp1 = jnp.where(row == CHUNK - 1, next_first, pltpu.roll(x, CHUNK - 1, 0))
        y = w_ref[0:1, :] * xm1 + w_ref[1:2, :] * x + w_ref[2:3, :] * xp1 + b_ref[...]
        return y * _sigmoid(y) * scale

    def v_aug_at(t0):
        return jnp.concatenate([v_ref[pl.ds(t0, CHUNK), :], ones_aug], axis=1)

    rows_g = GATE_GROUP * 8
    grow_g = lax.broadcasted_iota(jnp.int32, (rows_g, CHUNK), 0)
    glane_g = lax.broadcasted_iota(jnp.int32, (rows_g, CHUNK), 1)
    is_fwd_g = lax.rem(grow_g, 8) < 2

    def stack_chunks(w):
        return jnp.concatenate([w[:, i * CHUNK:(i + 1) * CHUNK] for i in range(GATE_GROUP)],
                               axis=0)

    def gate_body(gi, carry):
        c0 = pl.multiple_of(gi * GATE_GROUP, GATE_GROUP)
        t0 = pl.multiple_of(gi * (GATE_GROUP * CHUNK), GATE_GROUP * CHUNK)
        li = stack_chunks(li_ref[:, pl.ds(t0, GATE_GROUP * CHUNK)] + bli_ref[...])
        lf = stack_chunks(_log_sigmoid(lf_ref[:, pl.ds(t0, GATE_GROUP * CHUNK)] + blf_ref[...]))
        lfs = _split(lf, 1)
        a = jnp.where(is_fwd_g, _dot(lfs, trif_ref[...]), _dot(lfs, trib_ref[...]))
        r = a - li
        xf = -r
        xb = -r
        s = 1
        while s < CHUNK:
            xf = jnp.maximum(xf, jnp.where(glane_g >= s, pltpu.roll(xf, s, 1), NEG))
            xb = jnp.maximum(xb, jnp.where(glane_g < CHUNK - s, pltpu.roll(xb, CHUNK - s, 1), NEG))
            s *= 2
        a_tot = jnp.where(is_fwd_g, a[:, CHUNK - 1:CHUNK], a[:, 0:1])
        w_loc = a_tot - r
        m_loc = jnp.max(w_loc, axis=1, keepdims=True)
        tiles = ((rr_ref, r), (aa_ref, a), (cc_ref, jnp.where(is_fwd_g, xf, xb)),
                 (atot_ref, a_tot), (mloc_ref, jnp.broadcast_to(m_loc, (rows_g, CHUNK))),
                 (ee_ref, jnp.exp(w_loc - m_loc)))
        for ref, val in tiles:
            ref[pl.ds(c0, GATE_GROUP)] = val.reshape(GATE_GROUP, 8, CHUNK)
        return carry

    lax.fori_loop(0, nc // GATE_GROUP, gate_body, 0)

    def pre_body(c, carry):
        t0 = pl.multiple_of(c * CHUNK, CHUNK)
        qc_ref[pl.ds(t0, CHUNK), :] = conv_chunk(q_ref, cwq_ref, cbq_ref, c, 1.0).astype(BF16)
        kc = conv_chunk(k_ref, cwk_ref, cbk_ref, c, HEAD_DIM ** -0.5).astype(BF16)
        kc_ref[pl.ds(t0, CHUNK), :] = kc
        e_b = _dot_tn(_split(ee_ref[c], 0), xe_ref[...])
        v_aug = v_aug_at(t0)
        kf = kc.astype(F32)
        kvf_ref[c] = (_dot_tn((kf * e_b[:, :LANES]).astype(BF16), v_aug) * bdm_ref[...]).astype(BF16)
        kvb_ref[c] = (_dot_tn((kf * e_b[:, LANES:]).astype(BF16), v_aug) * bdm_ref[...]).astype(BF16)
        return carry

    lax.fori_loop(0, nc, pre_body, 0, unroll=MIX_UNROLL)

    st_ref[...] = jnp.zeros_like(st_ref)

    def lane_pair(t, k0):
        v = jnp.where(lane_lo[0:1, :], t[k0:k0 + 1, :], t[k0 + 1:k0 + 2, :])
        return jnp.concatenate([v, v], axis=1)

    def scan_body(t, m):
        cf = t
        cb = nc - 1 - t
        atot = jnp.where(is_fwd, atot_ref[cf], atot_ref[cb])
        mloc = jnp.where(is_fwd, mloc_ref[cf], mloc_ref[cb])
        mpf_ref[cf] = m
        mpb_ref[cb] = m
        m_new = jnp.maximum(atot + m, mloc)
        s_old = jnp.exp(atot + m - m_new)
        s_new = jnp.exp(mloc - m_new)
        for di, (kv_ref, cidx) in enumerate(((kvf_ref, cf), (kvb_ref, cb))):
            kv = kv_ref[cidx].astype(F32)
            s_st = st_ref[di]
            kv_ref[cidx] = s_st.astype(BF16)
            st_ref[di] = lane_pair(s_old, 2 * di) * s_st + lane_pair(s_new, 2 * di) * kv
        return m_new

    lax.fori_loop(0, nc, scan_body, jnp.zeros((8, CHUNK), F32), unroll=MIX_UNROLL)

    def out_body(c, carry):
        t0 = pl.multiple_of(c * CHUNK, CHUNK)
        q = qc_ref[pl.ds(t0, CHUNK), :]
        k = kc_ref[pl.ds(t0, CHUNK), :]
        v_aug = v_aug_at(t0)
        mp = jnp.where(is_fwd, mpf_ref[c], mpb_ref[c])
        u = -jnp.maximum(cc_ref[c], mp)
        cols = jnp.concatenate([u, jnp.exp(mp + u), jnp.exp(u - aa_ref[c])], axis=0)
        bc = _dot_tn(_split(cols, 0), xc_ref[...])
        r = rr_ref[c]
        scores = []
        for j in (0, 1):
            qj = jnp.where(lane_lo if j == 0 else jnp.logical_not(lane_lo), q, jnp.zeros_like(q))
            scores.append(_dot_nt(qj, k))
        h = None
        for di, (kv_ref, tri) in enumerate(((kvf_ref, tri_f), (kvb_ref, tri_b))):
            q_s = _dot(q, kv_ref[c])
            pvs = []
            for j in (0, 1):
                kk = 2 * di + j
                arg = jnp.where(tri, bc[:, kk * LANES:(kk + 1) * LANES] - r[kk:kk + 1, :], NEG)
                pvs.append(_dot((scores[j] * jnp.exp(arg)).astype(BF16), v_aug))
            sint = bc[:, (4 + di) * LANES:(5 + di) * LANES]
            em = bc[:, (6 + di) * LANES:(7 + di) * LANES]
            num = jnp.where(lane_lo, pvs[0][:, :LANES], pvs[1][:, :LANES]) + sint * q_s[:, :LANES]
            den = jnp.where(lane_lo, pvs[0][:, LANES:], pvs[1][:, LANES:]) + sint * q_s[:, LANES:]
            hd = num / jnp.maximum(jnp.abs(den), em)
            h = hd if h is None else h + hd
        y = _head_norm(h, bd2_ref) * nw_ref[...]
        y = y * _sigmoid(o_ref[pl.ds(t0, CHUNK), :].astype(F32))
        out_ref[pl.ds(t0, CHUNK), :] = y.astype(BF16)
        return carry

    lax.fori_loop(0, nc, out_body, 0, unroll=MIX_UNROLL)


def _mlstm(z3, gates_t, gbias, conv_w, conv_b, norm_w, consts):
    b, seq, _ = z3.shape
    nc = seq // CHUNK
    npair = H_MLSTM // 2

    def zspec(blk0):
        return pl.BlockSpec((None, seq, LANES), lambda bi, p, blk0=blk0: (bi, 0, blk0 + p))

    def full2(shape):
        return pl.BlockSpec(shape, lambda bi, p: (0, 0))

    gate_tile = pltpu.VMEM((nc, 8, CHUNK), F32)
    kern = functools.partial(_mlstm_kernel, seq=seq)
    return pl.pallas_call(
        kern,
        out_shape=jax.ShapeDtypeStruct((b, seq, W_MLSTM), BF16),
        grid=(b, npair),
        in_specs=[zspec(ZB_MQ), zspec(ZB_MK), zspec(ZB_MV), zspec(ZB_MO),
                  pl.BlockSpec((8, seq), lambda bi, p: (p, bi)),
                  pl.BlockSpec((8, seq), lambda bi, p: (npair + p, bi)),
                  pl.BlockSpec((8, 1), lambda bi, p: (p, 0)),
                  pl.BlockSpec((8, 1), lambda bi, p: (npair + p, 0)),
                  pl.BlockSpec((3, LANES), lambda bi, p: (0, p)),
                  pl.BlockSpec((3, LANES), lambda bi, p: (0, npair + p)),
                  pl.BlockSpec((1, LANES), lambda bi, p: (0, p)),
                  pl.BlockSpec((1, LANES), lambda bi, p: (0, npair + p)),
                  pl.BlockSpec((1, LANES), lambda bi, p: (0, p)),
                  full2((2 * CHUNK, CHUNK)), full2((2 * CHUNK, CHUNK)),
                  full2((16, 2 * LANES)), full2((48, 8 * LANES)),
                  full2((LANES, 2 * LANES)), full2((2 * LANES, LANES))],
        out_specs=pl.BlockSpec((None, seq, LANES), lambda bi, p: (bi, 0, p)),
        scratch_shapes=[pltpu.VMEM((seq, LANES), BF16),
                        pltpu.VMEM((seq, LANES), BF16),
                        pltpu.VMEM((nc, CHUNK, 2 * LANES), BF16),
                        pltpu.VMEM((nc, CHUNK, 2 * LANES), BF16),
                        pltpu.VMEM((2, CHUNK, 2 * LANES), F32)] + [gate_tile] * 8,
        compiler_params=_cparams(("arbitrary", "arbitrary")),
        name="mlstm",
    )(z3, z3, z3, z3, gates_t, gates_t, gbias, gbias, conv_w, conv_w, conv_b, conv_b, norm_w,
      consts["trif"], consts["trib"], consts["xe"], consts["xc"], consts["bdm2"], consts["bd2"])


def _ret_kernel(q_ref, k_ref, v_ref, gt_ref, rd_ref, nw_ref, ca_ref, sa_ref, cb_ref, sb_ref,
                sgn_ref, perm_ref, bdm_ref, bd2_ref, out_ref,
                qr_ref, kr_ref, hf_ref, sst_ref, tab_ref, *, seq):
    nc = seq // CHUNK
    lane_lo = _lane_lt64()
    rowf =lax.broadcasted_iota(jnp.int32, (CHUNK, LANES), 0).astype(F32)
    ri = lax.broadcasted_iota(jnp.int32, (CHUNK, CHUNK), 0)
    ci = lax.broadcasted_iota(jnp.int32, (CHUNK, CHUNK), 1)

    def rot_body(c, carry):
        t0 = pl.multiple_of(c * CHUNK, CHUNK)
        cb = cb_ref[c]
        sb = sb_ref[c]
        cos = ca_ref[...] * cb - sa_ref[...] * sb
        sin = (sa_ref[...] * cb + ca_ref[...] * sb) * sgn_ref[...]
        for src, dst in ((q_ref, qr_ref), (k_ref, kr_ref)):
            x = src[pl.ds(t0, CHUNK), :]
            xs = _dot(x, perm_ref[...])
            dst[pl.ds(t0, CHUNK), :] = (x.astype(F32) * cos + xs * sin).astype(BF16)
        return carry

    lax.fori_loop(0, nc, rot_body, 0, unroll=MIX_UNROLL)

    for d in (0, 1):
        lg_b = -jnp.exp(rd_ref[d])
        if d == 0:
            kdec = jnp.exp(lg_b * (CHUNK - 1.0 - rowf))
            qdec = jnp.exp(lg_b * (rowf + 1.0))
            dist = (ri - ci).astype(F32)
        else:
            kdec = jnp.exp(lg_b * rowf)
            qdec = jnp.exp(lg_b * (CHUNK - rowf))
            dist = (ci - ri).astype(F32)
        cdec = jnp.exp(lg_b * float(CHUNK))
        tab_ref[0] = kdec
        tab_ref[1] = qdec
        for j in (0, 1):
            lg_j = lg_b[:, HEAD_DIM * j:HEAD_DIM * j + 1]
            tab_ref[2 + j] = jnp.where(dist >= 0.0, jnp.exp(lg_j * jnp.maximum(dist, 0.0)), 0.0)

        def scan_body(c, s_st, d=d, cdec=cdec):
            cc = c if d == 0 else nc - 1 - c
            t0 = pl.multiple_of(cc * CHUNK, CHUNK)
            k_d = (kr_ref[pl.ds(t0, CHUNK), :].astype(F32) * tab_ref[0]).astype(BF16)
            kv = _dot_tn(k_d, v_ref[pl.ds(t0, CHUNK), :]) * bdm_ref[...]
            sst_ref[cc] = s_st.astype(BF16)
            return cdec * s_st + kv

        lax.fori_loop(0, nc, scan_body, jnp.zeros((CHUNK, LANES), F32), unroll=RET_UNROLL)

        def out_body(c, carry, d=d):
            t0 = pl.multiple_of(c * CHUNK, CHUNK)
            q = qr_ref[pl.ds(t0, CHUNK), :]
            k = kr_ref[pl.ds(t0, CHUNK), :]
            v = v_ref[pl.ds(t0, CHUNK), :]
            y_inter = _dot((q.astype(F32) * tab_ref[1]).astype(BF16), sst_ref[c])
            ys = []
            for j in (0, 1):
                qj = jnp.where(lane_lo if j == 0 else jnp.logical_not(lane_lo), q,
                               jnp.zeros_like(q))
                s = _dot_nt(qj, k) * tab_ref[2 + j]
                ys.append(_dot(s.astype(BF16), v))
            h = jnp.where(lane_lo, ys[0], ys[1]) + y_inter
            if d == 0:
                hf_ref[pl.ds(t0, CHUNK), :] = h
            else:
                h = h + hf_ref[pl.ds(t0, CHUNK), :]
                y = _head_norm(h, bd2_ref) * nw_ref[...]
                gt = gt_ref[pl.ds(t0, CHUNK), :].astype(F32)
                y = y * (gt * _sigmoid(gt))
                out_ref[pl.ds(t0, CHUNK), :] = y.astype(BF16)
            return carry

        lax.fori_loop(0, nc, out_body, 0, unroll=RET_UNROLL)


def _retention(z3, rd, norm_w, consts, rope):
    b, seq, _ = z3.shape
    nc = seq // CHUNK
    npair = H_RET // 2

    def zspec(blk0):
        return pl.BlockSpec((None, seq, LANES), lambda bi, p, blk0=blk0: (bi, 0, blk0 + p))

    def full2(shape):
        return pl.BlockSpec(shape, lambda bi, p: (0, 0))

    kern = functools.partial(_ret_kernel, seq=seq)
    return pl.pallas_call(
        kern,
        out_shape=jax.ShapeDtypeStruct((b, seq, W_RET), BF16),
        grid=(b, npair),
        in_specs=[zspec(ZB_RQ), zspec(ZB_RK), zspec(ZB_RV), zspec(ZB_RG),
                  pl.BlockSpec((None, 2, 1, LANES), lambda bi, p: (p, 0, 0, 0)),
                  pl.BlockSpec((1, LANES), lambda bi, p: (0, p)),
                  full2((CHUNK, LANES)), full2((CHUNK, LANES)),
                  pl.BlockSpec((nc, 1, LANES), lambda bi, p: (0, 0, 0)),
                  pl.BlockSpec((nc, 1, LANES), lambda bi, p: (0, 0, 0)),
                  full2((1, LANES)), full2((LANES, LANES)), full2((LANES, LANES)),
                  full2((2 * LANES, LANES))],
        out_specs=pl.BlockSpec((None, seq, LANES), lambda bi, p: (bi, 0, p)),
        scratch_shapes=[pltpu.VMEM((seq, LANES), BF16),
                        pltpu.VMEM((seq, LANES), BF16),
                        pltpu.VMEM((seq, LANES), F32),
                        pltpu.VMEM((nc, CHUNK, LANES), BF16),
                        pltpu.VMEM((4, CHUNK, LANES), F32)],
        compiler_params=_cparams(("arbitrary", "arbitrary")),
        name="retention",
    )(z3, z3, z3, z3, rd, norm_w, rope["ca"], rope["sa"], rope["cb"], rope["sb"],
      consts["sgn"], consts["perm"], consts["bdm"], consts["bd2"])


def _na_kernel(q_ref, k_ref, v_ref, bias_ref, out_ref, *, seq):
    rows = seq // GRID_W
    ng = rows // NA_QROWS
    nq = NA_QROWS * GRID_W
    nk = NA_KROWS * GRID_W
    lane_lo = lax.broadcasted_iota(jnp.int32, (nq, LANES), 1) < HEAD_DIM
    ones_aug = jnp.ones((nk, LANES), BF16)

    def body(g, carry):
        t0 = pl.multiple_of(g * nq, nq)
        base = jnp.clip(g * NA_QROWS - NA_ROWS // 2, 0, rows - NA_KROWS)
        k0 = pl.multiple_of(base * GRID_W, GRID_W)
        case = jnp.where(g == 0, 0, jnp.where(g == ng - 1, 2, 1))
        q = q_ref[pl.ds(t0, nq), :]
        kk = k_ref[pl.ds(k0, nk), :]
        v_aug = jnp.concatenate([v_ref[pl.ds(k0, nk), :], ones_aug], axis=1)
        outs = []
        for j in (0, 1):
            qj = jnp.where(lane_lo if j == 0 else jnp.logical_not(lane_lo), q,
                           jnp.zeros_like(q))
            s = _dot_nt(qj, kk) + bias_ref[case, j]
            m = jnp.max(s, axis=1, keepdims=True)
            e = jnp.exp(s - m)
            pv = _dot(e.astype(BF16), v_aug)
            outs.append(pv[:, :LANES] / pv[:, LANES:])
        out_ref[pl.ds(t0, nq), :] = jnp.where(lane_lo, outs[0], outs[1]).astype(BF16)
        return carry

    lax.fori_loop(0, ng, body, 0)


def _na(z3, bias):
    b, seq, _ = z3.shape
    npair = H_NA // 2
    nq = NA_QROWS * GRID_W
    nk = NA_KROWS * GRID_W

    def zspec(blk0):
        return pl.BlockSpec((None, seq, LANES), lambda bi, p, blk0=blk0: (bi, 0, blk0 + p))

    kern = functools.partial(_na_kernel, seq=seq)
    return pl.pallas_call(
        kern,
        out_shape=jax.ShapeDtypeStruct((b, seq, W_NA), BF16),
        grid=(b, npair),
        in_specs=[zspec(ZB_NQ), zspec(ZB_NK), zspec(ZB_NV),
                  pl.BlockSpec((None, 3, 2, nq, nk), lambda bi, p: (p, 0, 0, 0, 0))],
        out_specs=pl.BlockSpec((None, seq, LANES), lambda bi, p: (bi, 0, p)),
        compiler_params=_cparams(("arbitrary", "arbitrary")),
        name="natten",
    )(z3, z3, z3, bias)


def _na_index_tables(seq):
    rows = seq // GRID_W
    ng = rows // NA_QROWS
    tabs = []
    for g in (0, 1, ng - 1):
        base = int(np.clip(g * NA_QROWS - NA_ROWS // 2, 0, rows - NA_KROWS))
        qr = g * NA_QROWS + np.arange(NA_QROWS)[:, None, None, None]
        qc = np.arange(GRID_W)[None, :, None, None]
        kr = base + np.arange(NA_KROWS)[None, None, :, None]
        kc = np.arange(GRID_W)[None, None, None, :]
        rstart = np.clip(qr - NA_ROWS // 2, 0, rows - NA_ROWS)
        cstart = np.clip(qc - NA_COLS // 2, 0, GRID_W - NA_COLS)
        ok = (kr >= rstart) & (kr < rstart + NA_ROWS) & (kc >= cstart) & (kc < cstart + NA_COLS)
        rel_r = np.clip(kr - qr + NA_ROWS - 1, 0, 2 * NA_ROWS - 2)
        rel_c = np.clip(kc - qc + NA_COLS - 1, 0, 2 * NA_COLS - 2)
        shp = (NA_QROWS * GRID_W, NA_KROWS * GRID_W)
        full = (NA_QROWS, GRID_W, NA_KROWS, GRID_W)
        tabs.append((np.broadcast_to(ok, full).reshape(shp),
                     np.broadcast_to(rel_r, full).reshape(shp),
                     np.broadcast_to(rel_c, full).reshape(shp)))
    ok = np.stack([t[0] for t in tabs])
    rr = np.stack([t[1] for t in tabs])
    rc = np.stack([t[2] for t in tabs])
    return ok, rr, rc


def _na_bias(rpb, seq):
    ok, rr, rc = _na_index_tables(seq)
    h = rpb.shape[0]
    n_r, n_c = 2 * NA_ROWS - 1, 2 * NA_COLS - 1
    full = (3, NA_QROWS, GRID_W, NA_KROWS, GRID_W)
    rr5, rc5 = rr.reshape(full), rc.reshape(full)
    oh_r = (rr5[:, :, 0, :, 0][..., None] == np.arange(n_r)).astype(np.float32)
    oh_c = (rc5[0, 0, :, 0, :][..., None] == np.arange(n_c)).astype(np.float32)
    cols = jnp.einsum("hab,qkb->haqk", rpb.astype(F32), jnp.asarray(oh_c),
                      precision=lax.Precision.HIGHEST)
    bias = jnp.einsum("ciea,haqk->hciqek", jnp.asarray(oh_r), cols,
                      precision=lax.Precision.HIGHEST)
    bias = bias.reshape(h, 3, NA_QROWS * GRID_W, NA_KROWS * GRID_W)
    bias = jnp.where(ok[None], bias, NEG)
    return bias.reshape(h // 2, 2, 3, bias.shape[2], bias.shape[3]).transpose(0, 2, 1, 3, 4)


def _proj_out_kernel(ym_ref, yr_ref, yn_ref, x_ref, wm_ref, wr_ref, wn_ref, g_ref, rw_ref,
                     rb_ref, lst_ref, x2_ref, xn_ref, route_ref, cnt_ref):
    @pl.when(pl.program_id(0) == 0)
    def _():
        cnt_ref[...] = jnp.zeros_like(cnt_ref)

    acc = _dot(ym_ref[...], wm_ref[...]) + _dot(yr_ref[...], wr_ref[...])
    acc = acc + _dot(yn_ref[...], wn_ref[...])
    x2 = x_ref[...] + acc
    x2_ref[...] = x2
    ms = jnp.mean(x2 * x2, axis=-1, keepdims=True)
    xn = x2 * lax.rsqrt(ms + RMS_EPS) * g_ref[...]
    xn_ref[...] = xn
    logits = jnp.dot(xn, rw_ref[...], preferred_element_type=F32,
                     precision=lax.Precision.HIGHEST) + rb_ref[...]
    tm = logits.shape[0]
    lane = lax.broadcasted_iota(jnp.int32, (tm, LANES), 1).astype(F32)
    glog = jnp.where(lane < N_GROUPS, logits, NEG)
    gmax = jnp.max(glog, axis=1, keepdims=True)
    grp = jnp.min(jnp.where(glog == gmax, lane, float(LANES)), axis=1, keepdims=True)
    p_grp = 1.0 / jnp.sum(jnp.exp(glog - gmax), axis=1, keepdims=True)
    lo = N_GROUPS + grp * EXPERTS_PER_GROUP
    ein = jnp.where((lane >= lo) & (lane < lo + EXPERTS_PER_GROUP), logits, NEG)
    v1 = jnp.max(ein, axis=1, keepdims=True)
    i1 = jnp.min(jnp.where(ein == v1, lane, float(LANES)), axis=1, keepdims=True)
    ein2 = jnp.where(lane == i1, NEG, ein)
    v2 = jnp.max(ein2, axis=1, keepdims=True)
    i2 = jnp.min(jnp.where(ein2 == v2, lane, float(LANES)), axis=1, keepdims=True)
    e21 = jnp.exp(v2 - v1)
    g1 = p_grp / (1.0 + e21)
    g2 = p_grp * e21 / (1.0 + e21)
    oh1 = jnp.where(lane == i1 - N_GROUPS, 1.0, 0.0)
    oh2 = jnp.where(lane == i2 - N_GROUPS, 1.0, 0.0)
    both = oh1 + oh2
    before = _dot(lst_ref[...], both.astype(BF16)) + cnt_ref[...]
    rk1 = jnp.sum(oh1 * before, axis=1, keepdims=True)
    rk2 = jnp.sum(oh2 * before, axis=1, keepdims=True)
    cnt_ref[...] += jnp.sum(both, axis=0, keepdims=True)
    vals = (i1 - N_GROUPS, i2 - N_GROUPS, g1, g2, rk1, rk2)
    route = jnp.zeros_like(logits)
    for c, val in enumerate(vals):
        route = jnp.where(lane == c, val, route)
    route_ref[...] = route


def _proj_out(ym, yr, yn, x2, wm, wr, wn, g, rw, rb):
    n = x2.shape[0]
    tm = PROJ_TM

    def rows(w):
        return pl.BlockSpec((tm, w), lambda i: (i, 0))

    def full(shape):
        return pl.BlockSpec(shape, lambda i: (0, 0))

    ii = np.arange(tm)
    lstrict = jnp.asarray(ii[None, :] < ii[:, None], BF16)
    return pl.pallas_call(
        _proj_out_kernel,
        out_shape=(jax.ShapeDtypeStruct((n, D_MODEL), F32),
                   jax.ShapeDtypeStruct((n, D_MODEL), F32),
                   jax.ShapeDtypeStruct((n, LANES), F32),
                   jax.ShapeDtypeStruct((1, LANES), F32)),
        grid=(n // tm,),
        in_specs=[rows(W_MLSTM), rows(W_RET), rows(W_NA), rows(D_MODEL),
                  full((W_MLSTM, D_MODEL)), full((W_RET, D_MODEL)), full((W_NA, D_MODEL)),
                  full((1, D_MODEL)), full((D_MODEL, LANES)), full((1, LANES)), full((tm, tm))],
        out_specs=(rows(D_MODEL), rows(D_MODEL), rows(LANES), full((1, LANES))),
        compiler_params=_cparams(("arbitrary",)),
        name="proj_out_router",
    )(ym, yr, yn, x2, wm, wr, wn, g, rw, rb, lstrict)


def _dispatch_kernel(pend_ref, cnt_ref, nused_ref, meta_hbm, xn_ref, xs_hbm,
                     meta0, meta1, zero_buf, sem_meta, sem_rows, sem_zero):
    i = pl.program_id(0)
    nsteps = pl.num_programs(0)
    slot = lax.rem(i, 2)
    n_blocks = xs_hbm.shape[0] // MOE_BLK

    metas = (meta0, meta1)

    def meta_copy(step, s):
        return pltpu.make_async_copy(meta_hbm.at[step], metas[s], sem_meta.at[s])

    def zero_copy(blk):
        start = pl.multiple_of(blk * MOE_BLK, MOE_BLK)
        return pltpu.make_async_copy(zero_buf, xs_hbm.at[pl.ds(start, MOE_BLK), :], sem_zero)

    @pl.when(i == 0)
    def _():
        meta_copy(0, 0).start()
        zero_buf[...] = jnp.zeros_like(zero_buf)

        def zstart(e, carry):
            @pl.when(cnt_ref[e] > 0)
            def _():
                zero_copy(pend_ref[e] // MOE_BLK - 1).start()
            return carry

        def zwait(e, carry):
            @pl.when(cnt_ref[e] > 0)
            def _():
                zero_copy(0).wait()
            return carry

        lax.fori_loop(0, N_EXPERTS, zstart, 0)
        lax.fori_loop(nused_ref[0], n_blocks, lambda b, c: (zero_copy(b).start(), c)[1], 0)
        lax.fori_loop(0, N_EXPERTS, zwait, 0)
        lax.fori_loop(nused_ref[0], n_blocks, lambda b, c: (zero_copy(0).wait(), c)[1], 0)

    def row_copy(r, dst):
        return pltpu.make_async_copy(xn_ref.at[pl.ds(r, 1), :], xs_hbm.at[pl.ds(dst, 1), :],
                                     sem_rows)

    for s in (0, 1):
        @pl.when(slot == s)
        def _(s=s):
            meta_copy(i, s).wait()

            @pl.when(i + 1 < nsteps)
            def _():
                meta_copy(i + 1, 1 - s).start()

            def issue(r, carry):
                for kk in range(TOP_K):
                    row_copy(r, metas[s][TOP_K * r + kk]).start()
                return carry

            lax.fori_loop(0, COMB_TM, issue, 0, unroll=8)

    def drain(r, carry):
        for kk in range(TOP_K):
            row_copy(r, 0).wait()
        return carry

    lax.fori_loop(0, COMB_TM, drain, 0, unroll=8)


def _dispatch(pend, counts, n_used, meta, xn, n_slots):
    n = xn.shape[0]
    tm = COMB_TM
    grid_spec = pltpu.PrefetchScalarGridSpec(
        num_scalar_prefetch=3,
        grid=(n // tm,),
        in_specs=[pl.BlockSpec(memory_space=pl.ANY),
                  pl.BlockSpec((tm, D_MODEL), lambda i, *_: (i, 0))],
        out_specs=pl.BlockSpec(memory_space=pl.ANY),
        scratch_shapes=[pltpu.SMEM((TOP_K * tm,), jnp.int32),
                        pltpu.SMEM((TOP_K * tm,), jnp.int32),
                        pltpu.VMEM((MOE_BLK, D_MODEL), F32),
                        pltpu.SemaphoreType.DMA((2,)),
                        pltpu.SemaphoreType.DMA,
                        pltpu.SemaphoreType.DMA],
    )
    return pl.pallas_call(
        _dispatch_kernel,
        out_shape=jax.ShapeDtypeStruct((n_slots, D_MODEL), F32),
        grid_spec=grid_spec,
        compiler_params=_cparams(("arbitrary",)),
        name="dispatch",
    )(pend, counts, n_used, meta, xn)


def _expert_kernel(be_ref, nused_ref, xs_ref, w1_ref, w3_ref, w2_ref, y_ref):
    i = pl.program_id(0)

    @pl.when(i < nused_ref[0])
    def _():
        xb = xs_ref[...].astype(BF16)
        h1 = _dot(xb, w1_ref[...])
        h3 = _dot(xb, w3_ref[...])
        hb = (h1 * _sigmoid(h1) * h3).astype(BF16)
        y_ref[...] = _dot(hb, w2_ref[...])

    @pl.when(i >= nused_ref[0])
    def _():
        y_ref[...] = jnp.zeros_like(y_ref)


def _experts(block_expert, n_used, xs, w1, w3, w2):
    n_blocks = xs.shape[0] // MOE_BLK

    def wspec(shape):
        return pl.BlockSpec(shape, lambda i, be, nu: (be[i], 0, 0))

    grid_spec = pltpu.PrefetchScalarGridSpec(
        num_scalar_prefetch=2,
        grid=(n_blocks,),
        in_specs=[pl.BlockSpec((MOE_BLK, D_MODEL),
                               lambda i, be, nu: (jnp.minimum(i, nu[0] - 1), 0)),
                  wspec((None, D_MODEL, EXPERT_FF)), wspec((None, D_MODEL, EXPERT_FF)),
                  wspec((None, EXPERT_FF, D_MODEL))],
        out_specs=pl.BlockSpec((MOE_BLK, D_MODEL), lambda i, be, nu: (i, 0)),
    )
    return pl.pallas_call(
        _expert_kernel,
        out_shape=jax.ShapeDtypeStruct((n_blocks * MOE_BLK, D_MODEL), F32),
        grid_spec=grid_spec,
        compiler_params=_cparams(("arbitrary",)),
        name="experts",
    )(block_expert, n_used, xs, w1, w3, w2)


def _combine_kernel(meta_hbm, yb_hbm, x_ref, route_ref, g_ref, out_ref,
                    meta0, meta1, ybuf, sem_meta, sem_rows, *, final_norm):
    i = pl.program_id(0)
    nsteps = pl.num_programs(0)
    slot = lax.rem(i, 2)
    metas = (meta0, meta1)

    def meta_copy(step, s):
        return pltpu.make_async_copy(meta_hbm.at[step], metas[s], sem_meta.at[s])

    def row_copy(src, s, kk, r):
        return pltpu.make_async_copy(yb_hbm.at[pl.ds(src, 1), :],
                                     ybuf.at[s, kk, pl.ds(r, 1), :], sem_rows.at[s])

    def issue_tile(s):
        def issue(r, carry):
            for kk in range(TOP_K):
                row_copy(metas[s][TOP_K * r + kk], s, kk, r).start()
            return carry

        lax.fori_loop(0, COMB_TM, issue, 0, unroll=8)

    @pl.when(i == 0)
    def _():
        meta_copy(0, 0).start()
        meta_copy(0, 0).wait()
        issue_tile(0)

        @pl.when(nsteps > 1)
        def _():
            meta_copy(1, 1).start()

    for s in (0, 1):
        @pl.when(slot == s)
        def _(s=s):
            @pl.when(i + 1 < nsteps)
            def _():
                meta_copy(i + 1, 1 - s).wait()
                issue_tile(1 - s)

            @pl.when(i + 2 < nsteps)
            def _():
                meta_copy(i + 2, s).start()

            def drain(r, carry):
                for kk in range(TOP_K):
                    row_copy(0, s, kk, r).wait()
                return carry

            lax.fori_loop(0, COMB_TM, drain, 0, unroll=8)
            route = route_ref[...]
            y = x_ref[...]
            for kk in range(TOP_K):
                y = y + route[:, 2 + kk:3 + kk] * ybuf[s, kk]
            if final_norm:
                ms = jnp.mean(y * y, axis=-1, keepdims=True)
                y = y * lax.rsqrt(ms + RMS_EPS) * g_ref[...]
            out_ref[...] = y


def _combine(meta, yb, x2, route, g, final_norm):
    n = x2.shape[0]
    tm = COMB_TM
    kern = functools.partial(_combine_kernel, final_norm=final_norm)
    return pl.pallas_call(
        kern,
        out_shape=jax.ShapeDtypeStruct((n, D_MODEL), F32),
        grid=(n // tm,),
        in_specs=[pl.BlockSpec(memory_space=pl.ANY),
                  pl.BlockSpec(memory_space=pl.ANY),
                  pl.BlockSpec((tm, D_MODEL), lambda i: (i, 0)),
                  pl.BlockSpec((tm, LANES), lambda i: (i, 0)),
                  pl.BlockSpec((1, D_MODEL), lambda i: (0, 0))],
        out_specs=pl.BlockSpec((tm, D_MODEL), lambda i: (i, 0)),
        scratch_shapes=[pltpu.SMEM((TOP_K * tm,), jnp.int32),
                        pltpu.SMEM((TOP_K * tm,), jnp.int32),
                        pltpu.VMEM((2, TOP_K, tm, D_MODEL), F32),
                        pltpu.SemaphoreType.DMA((2,)),
                        pltpu.SemaphoreType.DMA((2,))],
        compiler_params=_cparams(("arbitrary",)),
        name="combine",
    )(meta, yb, x2, route, g)


def _dispatch_layout(route, counts_f, n):
    counts = counts_f[0, :N_EXPERTS].astype(jnp.int32)
    n_blocks = n * TOP_K // MOE_BLK + N_EXPERTS
    padded = (counts + MOE_BLK - 1) // MOE_BLK * MOE_BLK
    pend = jnp.cumsum(padded).astype(jnp.int32)
    pstart = pend - padded
    n_used = pend[-1] // MOE_BLK
    blk_start = jnp.arange(n_blocks, dtype=jnp.int32) * MOE_BLK
    blk_start = jnp.minimum(blk_start, (n_used - 1) * MOE_BLK)
    block_expert = jnp.sum((pend[None, :] <= blk_start[:, None]).astype(jnp.int32), axis=1)
    block_expert = jnp.clip(block_expert, 0, N_EXPERTS - 1).astype(jnp.int32)
    expert = route[:, 0:TOP_K].astype(jnp.int32)
    rank = route[:, 4:4 + TOP_K].astype(jnp.int32)
    seg = jnp.sum(jnp.where(expert[..., None] == jnp.arange(N_EXPERTS, dtype=jnp.int32),
                            pstart, 0), axis=-1)
    meta = (seg + rank).reshape(n // COMB_TM, TOP_K * COMB_TM)
    return block_expert, n_used.reshape(1), pend, counts, meta, n_blocks * MOE_BLK


def _const_tables():
    lane = np.arange(LANES)
    grp = lane // HEAD_DIM
    xe = np.zeros((2, 8, 2 * LANES), np.float32)
    for di in range(2):
        for j in range(2):
            xe[:, 2 * di + j, di * LANES + j * HEAD_DIM:di * LANES + (j + 1) * HEAD_DIM] = 1.0
    xc = np.zeros((2, 24, 8 * LANES), np.float32)
    for kk in range(4):
        xc[:, kk, kk * LANES:(kk + 1) * LANES] = 1.0
    for di in range(2):
        for j in range(2):
            for blk, base in ((4 + di, 8), (6 + di, 16)):
                lo = blk * LANES + j * HEAD_DIM
                xc[:, base + 2 * di + j, lo:lo + HEAD_DIM] = 1.0
    ii = np.arange(CHUNK)
    incl_f = (ii[:, None] <= ii[None, :]).astype(np.float32)
    incl_b = (ii[:, None] >= ii[None, :]).astype(np.float32)
    bd = (grp[:, None] == grp[None, :]).astype(np.float32)
    perm = np.zeros((LANES, LANES), np.float32)
    half = HEAD_DIM // 2
    src = (lane // HEAD_DIM) * HEAD_DIM + (lane % HEAD_DIM + half) % HEAD_DIM
    perm[src, lane] = 1.0
    sgn = np.where(lane % HEAD_DIM < half, -1.0, 1.0).astype(np.float32)[None, :]
    return {
        "xe": jnp.asarray(xe.reshape(16, 2 * LANES), BF16),
        "xc": jnp.asarray(xc.reshape(48, 8 * LANES), BF16),
        "trif": jnp.asarray(np.concatenate([incl_f, incl_f], 0), BF16),
        "trib": jnp.asarray(np.concatenate([incl_b, incl_b], 0), BF16),
        "bdm": jnp.asarray(bd, F32), "bdm2": jnp.asarray(np.concatenate([bd, bd], 1), F32),
        "bd2": jnp.asarray(np.concatenate([bd, bd], 0), BF16),
        "perm": jnp.asarray(perm, BF16), "sgn": jnp.asarray(sgn, F32),
    }


def _rope_tables(seq):
    half = HEAD_DIM // 2
    nc = seq // CHUNK
    inv_freq = ROPE_BASE ** (-np.arange(half, dtype=np.float64) / half)
    freq = inv_freq[np.arange(LANES) % half]
    ang_a = np.arange(CHUNK, dtype=np.float64)[:, None] * freq[None, :]
    ang_b = (np.arange(nc, dtype=np.float64) * CHUNK)[:, None] * freq[None, :]
    return {"ca": jnp.asarray(np.cos(ang_a), F32), "sa": jnp.asarray(np.sin(ang_a), F32),
            "cb": jnp.asarray(np.cos(ang_b)[:, None, :], F32),
            "sb": jnp.asarray(np.sin(ang_b)[:, None, :], F32)}


def _layer_weights(l, w_in, mlstm_gate_bias, ret_decay, w_out, router_group_w, router_group_b,
                   router_expert_w, router_expert_b):
    sizes = (W_MLSTM,) * 4 + (4 * H_MLSTM,) + (W_RET,) * 4 + (W_NA,) * 3
    offs = np.concatenate([[0], np.cumsum(sizes)])
    col = lambda i: w_in[l][:, int(offs[i]):int(offs[i + 1])]
    mq, mk, mv, mo, mg, rq, rk, rv, rg, nq, nk, nv = [col(i) for i in range(12)]
    scale = HEAD_DIM ** -0.5
    w_all = jnp.concatenate([mq, mk, mv, mo, rq, rk * scale, rv, rg, nq * scale, nk, nv],
                            axis=1).astype(BF16)
    sel = np.zeros((GATE_ROWS, 4 * H_MLSTM), np.float32)
    for kind in range(2):
        for p in range(H_MLSTM // 2):
            for k in range(4):
                sel[(kind * (H_MLSTM // 2) + p) * 8 + k,
                    (2 * kind + k // 2) * H_MLSTM + 2 * p + k % 2] = 1.0
    sel = jnp.asarray(sel)
    wg = jnp.dot(sel, mg.T, precision=lax.Precision.HIGHEST).astype(BF16)
    gbias = jnp.dot(sel, mlstm_gate_bias[l].astype(F32).reshape(-1, 1),
                    precision=lax.Precision.HIGHEST)
    rd = ret_decay[l].astype(F32)
    rd = jnp.repeat(rd.reshape(2, H_RET // 2, 2), HEAD_DIM, axis=2)
    rd = rd.transpose(1, 0, 2)[:, :, None, :]
    wo = w_out[l].astype(BF16)
    rw = jnp.concatenate([router_group_w[l], router_expert_w[l],
                          jnp.zeros((D_MODEL, LANES - N_GROUPS - N_EXPERTS), F32)], axis=1)
    rb = jnp.concatenate([router_group_b[l].astype(F32), router_expert_b[l].astype(F32),
                          jnp.zeros((LANES - N_GROUPS - N_EXPERTS,), F32)])[None, :]
    return (w_all, wg, gbias, rd, wo[:W_MLSTM], wo[W_MLSTM:W_MLSTM + W_RET], wo[W_MLSTM + W_RET:],
            rw, rb)


def _encoder(x, consts, norm_mix, w_in, mlstm_conv_w, mlstm_conv_b, mlstm_gate_bias, mlstm_norm,
             ret_decay, ret_norm, na_rpb, w_out, norm_ffn, router_group_w, router_group_b,
             router_expert_w, router_expert_b, expert_w1, expert_w3, expert_w2, norm_final):
    b, seq, _ = x.shape
    n = b * seq
    depth = w_in.shape[0]
    rope = _rope_tables(seq)
    x2 = x.reshape(n, D_MODEL).astype(F32)
    for l in range(depth):
        w_all, wg, gbias, rd, wo_m, wo_r, wo_n, rw, rb = _layer_weights(
            l, w_in, mlstm_gate_bias, ret_decay, w_out, router_group_w, router_group_b,
            router_expert_w, router_expert_b)
        z, gates_t = _proj_in(x2, norm_mix[l].astype(F32)[None, :], w_all, wg)
        z3 = z.reshape(b, seq, Z_WIDTH)
        y_m = _mlstm(z3, gates_t, gbias, mlstm_conv_w[l].astype(F32),
                     mlstm_conv_b[l].astype(F32)[None, :], mlstm_norm[l].astype(F32)[None, :], consts)
        y_r = _retention(z3, rd, ret_norm[l].astype(F32)[None, :], consts, rope)
        y_n = _na(z3, _na_bias(na_rpb[l], seq))
        x2, xn, route, counts_f = _proj_out(y_m.reshape(n, W_MLSTM), y_r.reshape(n, W_RET),
                                            y_n.reshape(n, W_NA), x2, wo_m, wo_r, wo_n,
                                            norm_ffn[l].astype(F32)[None, :], rw, rb)
        block_expert, n_used, pend, counts, meta, n_slots = _dispatch_layout(route, counts_f, n)
        xs = _dispatch(pend, counts, n_used, meta, xn, n_slots)
        yb = _experts(block_expert, n_used, xs,
                      expert_w1[l].astype(BF16), expert_w3[l].astype(BF16), expert_w2[l].astype(BF16))
        x2 = _combine(meta, yb, x2, route, norm_final.astype(F32)[None, :],
                      final_norm=(l == depth - 1))
    return x2.reshape(b, seq, D_MODEL)


def kernel(x_prompt, x_sample, norm_mix, w_in, mlstm_conv_w, mlstm_conv_b, mlstm_gate_bias,
           mlstm_norm, ret_decay, ret_norm, na_rpb, w_out, norm_ffn, router_group_w,
           router_group_b, router_expert_w, router_expert_b, expert_w1, expert_w3, expert_w2,
           norm_final):
    consts = _const_tables()
    weights = (norm_mix, w_in, mlstm_conv_w, mlstm_conv_b, mlstm_gate_bias, mlstm_norm,
               ret_decay, ret_norm, na_rpb, w_out, norm_ffn, router_group_w, router_group_b,
               router_expert_w, router_expert_b, expert_w1, expert_w3, expert_w2, norm_final)
    return (_encoder(x_prompt, consts, *weights), _encoder(x_sample, consts, *weights))
```

```python
import functools

import numpy as np
import jax
import jax.numpy as jnp
from jax import lax
from jax.experimental import pallas as pl
from jax.experimental.pallas import tpu as pltpu

F32 = jnp.float32
BF16 = jnp.bfloat16

D_MODEL = 1024
HEAD_DIM = 64
LANES = 128
CHUNK = 128
H_MLSTM = 6
H_RET = 6
H_NA = 4
W_MLSTM = H_MLSTM * HEAD_DIM
W_RET = H_RET * HEAD_DIM
W_NA = H_NA * HEAD_DIM
Z_WIDTH = 4 * W_MLSTM + 4 * W_RET + 3 * W_NA
GATE_ROWS = 2 * (H_MLSTM // 2) * 8
ROPE_BASE = 10000.0
GRID_W = 64
NA_ROWS = 8
NA_COLS = 16
NA_QROWS = 4
NA_KROWS = NA_QROWS + NA_ROWS
N_GROUPS = 4
EXPERTS_PER_GROUP = 8
N_EXPERTS = 32
TOP_K = 2
EXPERT_FF = 512
RMS_EPS = 1e-6
NEG = -1e30
VMEM_LIMIT = 56 * 1024 * 1024

PROJ_TM = 512
MOE_BLK = 256
COMB_TM = 256
MIX_UNROLL = 2
RET_UNROLL = 4
GATE_GROUP = 8

ZB_MQ, ZB_MK, ZB_MV, ZB_MO = 0, 3, 6, 9
ZB_RQ, ZB_RK, ZB_RV, ZB_RG = 12, 15, 18, 21
ZB_NQ, ZB_NK, ZB_NV = 24, 26, 28


def _dot(a, b):
    return jnp.dot(a, b, preferred_element_type=F32)


def _dot_nt(a, b):
    return lax.dot_general(a, b, (((1,), (1,)), ((), ())), preferred_element_type=F32)


def _dot_tn(a, b):
    return lax.dot_general(a, b, (((0,), (0,)), ((), ())), preferred_element_type=F32)


def _split(x, axis):
    hi = x.astype(BF16).astype(F32)
    return jnp.concatenate([hi, x - hi], axis=axis).astype(BF16)


def _sigmoid(x):
    return 1.0 / (1.0 + jnp.exp(-x))


def _log_sigmoid(x):
    return -(jnp.maximum(-x, 0.0) + jnp.log(1.0 + jnp.exp(-jnp.abs(x))))


def _cparams(sem):
    return pltpu.CompilerParams(dimension_semantics=sem, vmem_limit_bytes=VMEM_LIMIT)


def _proj_in_kernel(x_ref, g_ref, w_ref, wg_ref, z_ref, gate_ref):
    x = x_ref[...]
    ms = jnp.mean(x * x, axis=-1, keepdims=True)
    hx = (x * lax.rsqrt(ms + RMS_EPS) * g_ref[...]).astype(BF16)
    cw = 768
    for j in range(0, Z_WIDTH, cw):
        z_ref[:, j:j + cw] = _dot(hx, w_ref[:, j:j + cw]).astype(BF16)
    gate_ref[...] = _dot_nt(wg_ref[...], hx)


def _proj_in(x2, g, w, wg):
    n = x2.shape[0]
    return pl.pallas_call(
        _proj_in_kernel,
        out_shape=(jax.ShapeDtypeStruct((n, Z_WIDTH), BF16),
                   jax.ShapeDtypeStruct((GATE_ROWS, n), F32)),
        grid=(n // PROJ_TM,),
        in_specs=[pl.BlockSpec((PROJ_TM, D_MODEL), lambda i: (i, 0)),
                  pl.BlockSpec((1, D_MODEL), lambda i: (0, 0)),
                  pl.BlockSpec((D_MODEL, Z_WIDTH), lambda i: (0, 0)),
                  pl.BlockSpec((GATE_ROWS, D_MODEL), lambda i: (0, 0))],
        out_specs=(pl.BlockSpec((PROJ_TM, Z_WIDTH), lambda i: (i, 0)),
                   pl.BlockSpec((GATE_ROWS, PROJ_TM), lambda i: (0, i))),
        compiler_params=_cparams(("arbitrary",)),
        name="proj_in",
    )(x2, g, w, wg)


def _lane_lt64():
    return lax.broadcasted_iota(jnp.int32, (CHUNK, LANES), 1) < HEAD_DIM


def _tri_mask(d):
    r = lax.broadcasted_iota(jnp.int32, (CHUNK, CHUNK), 0)
    c = lax.broadcasted_iota(jnp.int32, (CHUNK, CHUNK), 1)
    return (c <= r) if d == 0 else (c >= r)


def _head_norm(h, bd2_ref):
    ms = _dot(_split(h * h, 1), bd2_ref[...]) * (1.0 / HEAD_DIM)
    return h * lax.rsqrt(ms + RMS_EPS)


def _mlstm_kernel(q_ref, k_ref, v_ref, o_ref, li_ref, lf_ref, bli_ref, blf_ref, cwq_ref, cwk_ref,
                  cbq_ref, cbk_ref, nw_ref, trif_ref, trib_ref, xe_ref, xc_ref, bdm_ref, bd2_ref,
                  out_ref, qc_ref, kc_ref, kvf_ref, kvb_ref, st_ref, rr_ref, aa_ref, cc_ref,
                  atot_ref, mloc_ref, mpf_ref, mpb_ref, ee_ref, *, seq):
    nc = seq // CHUNK
    lane_lo = _lane_lt64()
    row = lax.broadcasted_iota(jnp.int32, (CHUNK, LANES), 0)
    grow = lax.broadcasted_iota(jnp.int32, (8, CHUNK), 0)
    glane = lax.broadcasted_iota(jnp.int32, (8, CHUNK), 1)
    is_fwd = grow < 2
    ones_aug = jnp.ones((CHUNK, LANES), BF16)
    tri_f = _tri_mask(0)
    tri_b = _tri_mask(1)

    def conv_chunk(src_ref, w_ref, b_ref, c, scale):
        t0 = pl.multiple_of(c * CHUNK, CHUNK)
        x = src_ref[pl.ds(t0, CHUNK), :].astype(F32)
        tp = pl.multiple_of(jnp.maximum(t0 - 16, 0), 16)
        tn = pl.multiple_of(jnp.minimum(t0 + CHUNK, seq - 16), 16)
        prev_last = src_ref[pl.ds(tp, 16), :].astype(F32)[15:16, :]
        next_first = src_ref[pl.ds(tn, 16), :].astype(F32)[0:1, :]
        prev_last = prev_last * jnp.where(c > 0, 1.0, 0.0)
        next_first = next_first * jnp.where(c < nc - 1, 1.0, 0.0)
        xm1 = jnp.where(row == 0, prev_last, pltpu.roll(x, 1, 0))
        xp1 = jnp.where(row == CHUNK - 1, next_first, pltpu.roll(x, CHUNK - 1, 0))
        y = w_ref[0:1, :] * xm1 + w_ref[1:2, :] * x + w_ref[2:3, :] * xp1 + b_ref[...]
        return y * _sigmoid(y) * scale

    def v_aug_at(t0):
        return jnp.concatenate([v_ref[pl.ds(t0, CHUNK), :], ones_aug], axis=1)

    rows_g = GATE_GROUP * 8
    grow_g = lax.broadcasted_iota(jnp.int32, (rows_g, CHUNK), 0)
    glane_g = lax.broadcasted_iota(jnp.int32, (rows_g, CHUNK), 1)
    is_fwd_g = lax.rem(grow_g, 8) < 2

    def stack_chunks(w):
        return jnp.concatenate([w[:, i * CHUNK:(i + 1) * CHUNK] for i in range(GATE_GROUP)],
                               axis=0)

    def gate_body(gi, carry):
        c0 = pl.multiple_of(gi * GATE_GROUP, GATE_GROUP)
        t0 = pl.multiple_of(gi * (GATE_GROUP * CHUNK), GATE_GROUP * CHUNK)
        li = stack_chunks(li_ref[:, pl.ds(t0, GATE_GROUP * CHUNK)] + bli_ref[...])
        lf = stack_chunks(_log_sigmoid(lf_ref[:, pl.ds(t0, GATE_GROUP * CHUNK)] + blf_ref[...]))
        lfs = _split(lf, 1)
        a = jnp.where(is_fwd_g, _dot(lfs, trif_ref[...]), _dot(lfs, trib_ref[...]))
        r = a - li
        xf = -r
        xb = -r
        s = 1
        while s < CHUNK:
            xf = jnp.maximum(xf, jnp.where(glane_g >= s, pltpu.roll(xf, s, 1), NEG))
            xb = jnp.maximum(xb, jnp.where(glane_g < CHUNK - s, pltpu.roll(xb, CHUNK - s, 1), NEG))
            s *= 2
        a_tot = jnp.where(is_fwd_g, a[:, CHUNK - 1:CHUNK], a[:, 0:1])
        w_loc = a_tot - r
        m_loc = jnp.max(w_loc, axis=1, keepdims=True)
        tiles = ((rr_ref, r), (aa_ref, a), (cc_ref, jnp.where(is_fwd_g, xf, xb)),
                 (atot_ref, a_tot), (mloc_ref, jnp.broadcast_to(m_loc, (rows_g, CHUNK))),
                 (ee_ref, jnp.exp(w_loc - m_loc)))
        for ref, val in tiles:
            ref[pl.ds(c0, GATE_GROUP)] = val.reshape(GATE_GROUP, 8, CHUNK)
        return carry

    lax.fori_loop(0, nc // GATE_GROUP, gate_body, 0)

    def pre_body(c, carry):
        t0 = pl.multiple_of(c * CHUNK, CHUNK)
        qc_ref[pl.ds(t0, CHUNK), :] = conv_chunk(q_ref, cwq_ref, cbq_ref, c, 1.0).astype(BF16)
        kc = conv_chunk(k_ref, cwk_ref, cbk_ref, c, HEAD_DIM ** -0.5).astype(BF16)
        kc_ref[pl.ds(t0, CHUNK), :] = kc
        e_b = _dot_tn(_split(ee_ref[c], 0), xe_ref[...])
        v_aug = v_aug_at(t0)
        kf = kc.astype(F32)
        kvf_ref[c] = (_dot_tn((kf * e_b[:, :LANES]).astype(BF16), v_aug) * bdm_ref[...]).astype(BF16)
        kvb_ref[c] = (_dot_tn((kf * e_b[:, LANES:]).astype(BF16), v_aug) * bdm_ref[...]).astype(BF16)
        return carry

    lax.fori_loop(0, nc, pre_body, 0, unroll=4)

    st_ref[...] = jnp.zeros_like(st_ref)

    def lane_pair(t, k0):
        v = jnp.where(lane_lo[0:1, :], t[k0:k0 + 1, :], t[k0 + 1:k0 + 2, :])
        return jnp.concatenate([v, v], axis=1)

    def scan_body(t, m):
        cf = t
        cb = nc - 1 - t
        atot = jnp.where(is_fwd, atot_ref[cf], atot_ref[cb])
        mloc = jnp.where(is_fwd, mloc_ref[cf], mloc_ref[cb])
        mpf_ref[cf] = m
        mpb_ref[cb] = m
        m_new = jnp.maximum(atot + m, mloc)
        s_old = jnp.exp(atot + m - m_new)
        s_new = jnp.exp(mloc - m_new)
        for di, (kv_ref, cidx) in enumerate(((kvf_ref, cf), (kvb_ref, cb))):
            kv = kv_ref[cidx].astype(F32)
            s_st = st_ref[di]
            kv_ref[cidx] = s_st.astype(BF16)
            st_ref[di] = lane_pair(s_old, 2 * di) * s_st + lane_pair(s_new, 2 * di) * kv
        return m_new

    lax.fori_loop(0, nc, scan_body, jnp.zeros((8, CHUNK), F32), unroll=MIX_UNROLL)

    def out_body(c, carry):
        t0 = pl.multiple_of(c * CHUNK, CHUNK)
        q = qc_ref[pl.ds(t0, CHUNK), :]
        k = kc_ref[pl.ds(t0, CHUNK), :]
        v_aug = v_aug_at(t0)
        mp = jnp.where(is_fwd, mpf_ref[c], mpb_ref[c])
        u = -jnp.maximum(cc_ref[c], mp)
        cols = jnp.concatenate([u, jnp.exp(mp + u), jnp.exp(u - aa_ref[c])], axis=0)
        bc = _dot_tn(_split(cols, 0), xc_ref[...])
        r = rr_ref[c]
        scores = []
        for j in (0, 1):
            qj = jnp.where(lane_lo if j == 0 else jnp.logical_not(lane_lo), q, jnp.zeros_like(q))
            scores.append(_dot_nt(qj, k))
        h = None
        for di, (kv_ref, tri) in enumerate(((kvf_ref, tri_f), (kvb_ref, tri_b))):
            q_s = _dot(q, kv_ref[c])
            pvs = []
            for j in (0, 1):
                kk = 2 * di + j
                arg = jnp.where(tri, bc[:, kk * LANES:(kk + 1) * LANES] - r[kk:kk + 1, :], NEG)
                pvs.append(_dot((scores[j] * jnp.exp(arg)).astype(BF16), v_aug))
            sint = bc[:, (4 + di) * LANES:(5 + di) * LANES]
            em = bc[:, (6 + di) * LANES:(7 + di) * LANES]
            num = jnp.where(lane_lo, pvs[0][:, :LANES], pvs[1][:, :LANES]) + sint * q_s[:, :LANES]
            den = jnp.where(lane_lo, pvs[0][:, LANES:], pvs[1][:, LANES:]) + sint * q_s[:, LANES:]
            hd = num / jnp.maximum(jnp.abs(den), em)
            h = hd if h is None else h + hd
        y = _head_norm(h, bd2_ref) * nw_ref[...]
        y = y * _sigmoid(o_ref[pl.ds(t0, CHUNK), :].astype(F32))
        out_ref[pl.ds(t0, CHUNK), :] = y.astype(BF16)
        return carry

    lax.fori_loop(0, nc, out_body, 0, unroll=4)


def _mlstm(z3, gates_t, gbias, conv_w, conv_b, norm_w, consts):
    b, seq, _ = z3.shape
    nc = seq // CHUNK
    npair = H_MLSTM // 2

    def zspec(blk0):
        return pl.BlockSpec((None, seq, LANES), lambda bi, p, blk0=blk0: (bi, 0, blk0 + p))

    def full2(shape):
        return pl.BlockSpec(shape, lambda bi, p: (0, 0))

    gate_tile = pltpu.VMEM((nc, 8, CHUNK), F32)
    kern = functools.partial(_mlstm_kernel, seq=seq)
    return pl.pallas_call(
        kern,
        out_shape=jax.ShapeDtypeStruct((b, seq, W_MLSTM), BF16),
        grid=(b, npair),
        in_specs=[zspec(ZB_MQ), zspec(ZB_MK), zspec(ZB_MV), zspec(ZB_MO),
                  pl.BlockSpec((8, seq), lambda bi, p: (p, bi)),
                  pl.BlockSpec((8, seq), lambda bi, p: (npair + p, bi)),
                  pl.BlockSpec((8, 1), lambda bi, p: (p, 0)),
                  pl.BlockSpec((8, 1), lambda bi, p: (npair + p, 0)),
                  pl.BlockSpec((3, LANES), lambda bi, p: (0, p)),
                  pl.BlockSpec((3, LANES), lambda bi, p: (0, npair + p)),
                  pl.BlockSpec((1, LANES), lambda bi, p: (0, p)),
                  pl.BlockSpec((1, LANES), lambda bi, p: (0, npair + p)),
                  pl.BlockSpec((1, LANES), lambda bi, p: (0, p)),
                  full2((2 * CHUNK, CHUNK)), full2((2 * CHUNK, CHUNK)),
                  full2((16, 2 * LANES)), full2((48, 8 * LANES)),
                  full2((LANES, 2 * LANES)), full2((2 * LANES, LANES))],
        out_specs=pl.BlockSpec((None, seq, LANES), lambda bi, p: (bi, 0, p)),
        scratch_shapes=[pltpu.VMEM((seq, LANES), BF16),
                        pltpu.VMEM((seq, LANES), BF16),
                        pltpu.VMEM((nc, CHUNK, 2 * LANES), BF16),
                        pltpu.VMEM((nc, CHUNK, 2 * LANES), BF16),
                        pltpu.VMEM((2, CHUNK, 2 * LANES), F32)] + [gate_tile] * 8,
        compiler_params=_cparams(("arbitrary", "arbitrary")),
        name="mlstm",
    )(z3, z3, z3, z3, gates_t, gates_t, gbias, gbias, conv_w, conv_w, conv_b, conv_b, norm_w,
      consts["trif"], consts["trib"], consts["xe"], consts["xc"], consts["bdm2"], consts["bd2"])


def _ret_kernel(q_ref, k_ref, v_ref, gt_ref, rd_ref, nw_ref, ca_ref, sa_ref, cb_ref, sb_ref,
                sgn_ref, perm_ref, bdm_ref, bd2_ref, out_ref,
                qr_ref, kr_ref, kvf_ref, kvb_ref, st_ref, tab_ref, *, seq):
    nc = seq // CHUNK
    lane_lo = _lane_lt64()
    rowf = lax.broadcasted_iota(jnp.int32, (CHUNK, LANES), 0).astype(F32)
    ri = lax.broadcasted_iota(jnp.int32, (CHUNK, CHUNK), 0)
    ci = lax.broadcasted_iota(jnp.int32, (CHUNK, CHUNK), 1)
    dist = (ri - ci).astype(F32)
    lg_f = -jnp.exp(rd_ref[0])
    lg_b = -jnp.exp(rd_ref[1])
    tab_ref[0] = jnp.exp(lg_f * (CHUNK - 1.0 - rowf))
    tab_ref[1] = jnp.exp(lg_b * rowf)
    tab_ref[2] = jnp.exp(lg_f * (rowf + 1.0))
    tab_ref[3] = jnp.exp(lg_b * (CHUNK - rowf))
    for j in (0, 1):
        lgf_j = lg_f[:, HEAD_DIM * j:HEAD_DIM * j + 1]
        lgb_j = lg_b[:, HEAD_DIM * j:HEAD_DIM * j + 1]
        tab_ref[4 + j] = (jnp.where(dist >= 0.0, jnp.exp(lgf_j * jnp.maximum(dist, 0.0)), 0.0)
                          + jnp.where(dist <= 0.0, jnp.exp(lgb_j * jnp.maximum(-dist, 0.0)), 0.0))
    cdec_f = jnp.exp(lg_f * float(CHUNK))
    cdec_b = jnp.exp(lg_b * float(CHUNK))

    def pre_body(c, carry):
        t0 = pl.multiple_of(c * CHUNK, CHUNK)
        cb = cb_ref[c]
        sb = sb_ref[c]
        cos = ca_ref[...] * cb - sa_ref[...] * sb
        sin = (sa_ref[...] * cb + ca_ref[...] * sb) * sgn_ref[...]
        rot = []
        for src, dst in ((q_ref, qr_ref), (k_ref, kr_ref)):
            x = src[pl.ds(t0, CHUNK), :]
            xs = _dot(x, perm_ref[...])
            xr = (x.astype(F32) * cos + xs * sin).astype(BF16)
            dst[pl.ds(t0, CHUNK), :] = xr
            rot.append(xr)
        kf = rot[1].astype(F32)
        v = v_ref[pl.ds(t0, CHUNK), :]
        kvf_ref[c] = (_dot_tn((kf * tab_ref[0]).astype(BF16), v) * bdm_ref[...]).astype(BF16)
        kvb_ref[c] = (_dot_tn((kf * tab_ref[1]).astype(BF16), v) * bdm_ref[...]).astype(BF16)
        return carry

    lax.fori_loop(0, nc, pre_body, 0, unroll=RET_UNROLL)

    st_ref[...] = jnp.zeros_like(st_ref)

    def scan_body(t, carry):
        for di, (kv_ref, cidx, cdec) in enumerate(((kvf_ref, t, cdec_f),
                                                   (kvb_ref, nc - 1 - t, cdec_b))):
            kv = kv_ref[cidx].astype(F32)
            s_st = st_ref[di]
            kv_ref[cidx] = s_st.astype(BF16)
            st_ref[di] = cdec * s_st + kv
        return carry

    lax.fori_loop(0, nc, scan_body, 0, unroll=RET_UNROLL)

    def out_body(c, carry):
        t0 = pl.multiple_of(c * CHUNK, CHUNK)
        q = qr_ref[pl.ds(t0, CHUNK), :]
        k = kr_ref[pl.ds(t0, CHUNK), :]
        v = v_ref[pl.ds(t0, CHUNK), :]
        qf = q.astype(F32)
        q_dec = jnp.concatenate([(qf * tab_ref[2]).astype(BF16), (qf * tab_ref[3]).astype(BF16)],
                                axis=1)
        states = jnp.concatenate([kvf_ref[c], kvb_ref[c]], axis=0)
        ys = []
        for j in (0, 1):
            qj = jnp.where(lane_lo if j == 0 else jnp.logical_not(lane_lo), q, jnp.zeros_like(q))
            s = _dot_nt(qj, k) * tab_ref[4 + j]
            ys.append(_dot(s.astype(BF16), v))
        h = jnp.where(lane_lo, ys[0], ys[1]) + _dot(q_dec, states)
        y = _head_norm(h, bd2_ref) * nw_ref[...]
        gt = gt_ref[pl.ds(t0, CHUNK), :].astype(F32)
        y = y * (gt * _sigmoid(gt))
        out_ref[pl.ds(t0, CHUNK), :] = y.astype(BF16)
        return carry

    lax.fori_loop(0, nc, out_body, 0, unroll=RET_UNROLL)


def _retention(z3, rd, norm_w, consts, rope):
    b, seq, _ = z3.shape
    nc = seq // CHUNK
    npair = H_RET // 2

    def zspec(blk0):
        return pl.BlockSpec((None, seq, LANES), lambda bi, p, blk0=blk0: (bi, 0, blk0 + p))

    def full2(shape):
        return pl.BlockSpec(shape, lambda bi, p: (0, 0))

    kern = functools.partial(_ret_kernel, seq=seq)
    return pl.pallas_call(
        kern,
        out_shape=jax.ShapeDtypeStruct((b, seq, W_RET), BF16),
        grid=(b, npair),
        in_specs=[zspec(ZB_RQ), zspec(ZB_RK), zspec(ZB_RV), zspec(ZB_RG),
                  pl.BlockSpec((None, 2, 1, LANES), lambda bi, p: (p, 0, 0, 0)),
                  pl.BlockSpec((1, LANES), lambda bi, p: (0, p)),
                  full2((CHUNK, LANES)), full2((CHUNK, LANES)),
                  pl.BlockSpec((nc, 1, LANES), lambda bi, p: (0, 0, 0)),
                  pl.BlockSpec((nc, 1, LANES), lambda bi, p: (0, 0, 0)),
                  full2((1, LANES)), full2((LANES, LANES)), full2((LANES, LANES)),
                  full2((2 * LANES, LANES))],
        out_specs=pl.BlockSpec((None, seq, LANES), lambda bi, p: (bi, 0, p)),
        scratch_shapes=[pltpu.VMEM((seq, LANES), BF16),
                        pltpu.VMEM((seq, LANES), BF16),
                        pltpu.VMEM((nc, CHUNK, LANES), BF16),
                        pltpu.VMEM((nc, CHUNK, LANES), BF16),
                        pltpu.VMEM((2, CHUNK, LANES), F32),
                        pltpu.VMEM((6, CHUNK, LANES), F32)],
        compiler_params=_cparams(("arbitrary", "arbitrary")),
        name="retention",
    )(z3, z3, z3, z3, rd, norm_w, rope["ca"], rope["sa"], rope["cb"], rope["sb"],
      consts["sgn"], consts["perm"], consts["bdm"], consts["bd2"])


def _na_kernel(q_ref, k_ref, v_ref, bias_ref, out_ref, *, seq):
    rows = seq // GRID_W
    ng = rows // NA_QROWS
    nq = NA_QROWS * GRID_W
    nk = NA_KROWS * GRID_W
    lane_lo = lax.broadcasted_iota(jnp.int32, (nq, LANES), 1) < HEAD_DIM
    ones_aug = jnp.ones((nk, LANES), BF16)

    def body(g, carry):
        t0 = pl.multiple_of(g * nq, nq)
        base = jnp.clip(g * NA_QROWS - NA_ROWS // 2, 0, rows - NA_KROWS)
        k0 = pl.multiple_of(base * GRID_W, GRID_W)
        case = jnp.where(g == 0, 0, jnp.where(g == ng - 1, 2, 1))
        q = q_ref[pl.ds(t0, nq), :]
        kk = k_ref[pl.ds(k0, nk), :]
        v_aug = jnp.concatenate([v_ref[pl.ds(k0, nk), :], ones_aug], axis=1)
        outs = []
        for j in (0, 1):
            qj = jnp.where(lane_lo if j == 0 else jnp.logical_not(lane_lo), q,
                           jnp.zeros_like(q))
            s = _dot_nt(qj, kk) + bias_ref[case, j]
            m = jnp.max(s, axis=1, keepdims=True)
            e = jnp.exp(s - m)
            pv = _dot(e.astype(BF16), v_aug)
            outs.append(pv[:, :LANES] / pv[:, LANES:])
        out_ref[pl.ds(t0, nq), :] = jnp.where(lane_lo, outs[0], outs[1]).astype(BF16)
        return carry

    lax.fori_loop(0, ng, body, 0)


def _na(z3, bias):
    b, seq, _ = z3.shape
    npair = H_NA // 2
    nq = NA_QROWS * GRID_W
    nk = NA_KROWS * GRID_W

    def zspec(blk0):
        return pl.BlockSpec((None, seq, LANES), lambda bi, p, blk0=blk0: (bi, 0, blk0 + p))

    kern = functools.partial(_na_kernel, seq=seq)
    return pl.pallas_call(
        kern,
        out_shape=jax.ShapeDtypeStruct((b, seq, W_NA), BF16),
        grid=(b, npair),
        in_specs=[zspec(ZB_NQ), zspec(ZB_NK), zspec(ZB_NV),
                  pl.BlockSpec((None, 3, 2, nq, nk), lambda bi, p: (p, 0, 0, 0, 0))],
        out_specs=pl.BlockSpec((None, seq, LANES), lambda bi, p: (bi, 0, p)),
        compiler_params=_cparams(("arbitrary", "arbitrary")),
        name="natten",
    )(z3, z3, z3, bias)


def _na_index_tables(seq):
    rows = seq // GRID_W
    ng = rows // NA_QROWS
    tabs = []
    for g in (0, 1, ng - 1):
        base = int(np.clip(g * NA_QROWS - NA_ROWS // 2, 0, rows - NA_KROWS))
        qr = g * NA_QROWS + np.arange(NA_QROWS)[:, None, None, None]
        qc = np.arange(GRID_W)[None, :, None, None]
        kr = base + np.arange(NA_KROWS)[None, None, :, None]
        kc = np.arange(GRID_W)[None, None, None, :]
        rstart = np.clip(qr - NA_ROWS // 2, 0, rows - NA_ROWS)
        cstart = np.clip(qc - NA_COLS // 2, 0, GRID_W - NA_COLS)
        ok = (kr >= rstart) & (kr < rstart + NA_ROWS) & (kc >= cstart) & (kc < cstart + NA_COLS)
        rel_r = np.clip(kr - qr + NA_ROWS - 1, 0, 2 * NA_ROWS - 2)
        rel_c = np.clip(kc - qc + NA_COLS - 1, 0, 2 * NA_COLS - 2)
        shp = (NA_QROWS * GRID_W, NA_KROWS * GRID_W)
        full = (NA_QROWS, GRID_W, NA_KROWS, GRID_W)
        tabs.append((np.broadcast_to(ok, full).reshape(shp),
                     np.broadcast_to(rel_r, full).reshape(shp),
                     np.broadcast_to(rel_c, full).reshape(shp)))
    ok = np.stack([t[0] for t in tabs])
    rr = np.stack([t[1] for t in tabs])
    rc = np.stack([t[2] for t in tabs])
    return ok, rr, rc


def _na_bias(rpb, seq):
    ok, rr, rc = _na_index_tables(seq)
    h = rpb.shape[0]
    n_r, n_c = 2 * NA_ROWS - 1, 2 * NA_COLS - 1
    full = (3, NA_QROWS, GRID_W, NA_KROWS, GRID_W)
    rr5, rc5 = rr.reshape(full), rc.reshape(full)
    oh_r = (rr5[:, :, 0, :, 0][..., None] == np.arange(n_r)).astype(np.float32)
    oh_c = (rc5[0, 0, :, 0, :][..., None] == np.arange(n_c)).astype(np.float32)
    cols = jnp.einsum("hab,qkb->haqk", rpb.astype(F32), jnp.asarray(oh_c),
                      precision=lax.Precision.HIGHEST)
    bias = jnp.einsum("ciea,haqk->hciqek", jnp.asarray(oh_r), cols,
                      precision=lax.Precision.HIGHEST)
    bias = bias.reshape(h, 3, NA_QROWS * GRID_W, NA_KROWS * GRID_W)
    bias = jnp.where(ok[None], bias, NEG)
    return bias.reshape(h // 2, 2, 3, bias.shape[2], bias.shape[3]).transpose(0, 2, 1, 3, 4)


def _proj_out_kernel(ym_ref, yr_ref, yn_ref, x_ref, wo_ref, g_ref, rw_ref,
                     rb_ref, lst_ref, x2_ref, xn_ref, route_ref, cnt_ref):
    @pl.when(pl.program_id(0) == 0)
    def _():
        cnt_ref[...] = jnp.zeros_like(cnt_ref)

    y_cat = jnp.concatenate([ym_ref[...], yr_ref[...], yn_ref[...]], axis=1)
    x2 = x_ref[...] + _dot(y_cat, wo_ref[...])
    x2_ref[...] = x2
    ms = jnp.mean(x2 * x2, axis=-1, keepdims=True)
    xn = x2 * lax.rsqrt(ms + RMS_EPS) * g_ref[...]
    xn_ref[...] = xn
    xh = xn.astype(BF16)
    xl = (xn - xh.astype(F32)).astype(BF16)
    logits = (_dot(xh, rw_ref[0]) + _dot(xl, rw_ref[0]) + _dot(xh, rw_ref[1])) + rb_ref[...]
    tm = logits.shape[0]
    lane = lax.broadcasted_iota(jnp.int32, (tm, LANES), 1).astype(F32)
    glog = jnp.where(lane < N_GROUPS, logits, NEG)
    gmax = jnp.max(glog, axis=1, keepdims=True)
    grp = jnp.min(jnp.where(glog == gmax, lane, float(LANES)), axis=1, keepdims=True)
    p_grp = 1.0 / jnp.sum(jnp.exp(glog - gmax), axis=1, keepdims=True)
    lo = N_GROUPS + grp * EXPERTS_PER_GROUP
    ein = jnp.where((lane >= lo) & (lane < lo + EXPERTS_PER_GROUP), logits, NEG)
    v1 = jnp.max(ein, axis=1, keepdims=True)
    i1 = jnp.min(jnp.where(ein == v1, lane, float(LANES)), axis=1, keepdims=True)
    ein2 = jnp.where(lane == i1, NEG, ein)
    v2 = jnp.max(ein2, axis=1, keepdims=True)
    i2 = jnp.min(jnp.where(ein2 == v2, lane, float(LANES)), axis=1, keepdims=True)
    e21 = jnp.exp(v2 - v1)
    g1 = p_grp / (1.0 + e21)
    g2 = p_grp * e21 / (1.0 + e21)
    oh1 = jnp.where(lane == i1 - N_GROUPS, 1.0, 0.0)
    oh2 = jnp.where(lane == i2 - N_GROUPS, 1.0, 0.0)
    both = oh1 + oh2
    before = _dot(lst_ref[...], both.astype(BF16)) + cnt_ref[...]
    rk1 = jnp.sum(oh1 * before, axis=1, keepdims=True)
    rk2 = jnp.sum(oh2 * before, axis=1, keepdims=True)
    cnt_ref[...] += jnp.sum(both, axis=0, keepdims=True)
    vals = (i1 - N_GROUPS, i2 - N_GROUPS, g1, g2, rk1, rk2)
    route = jnp.zeros_like(logits)
    for c, val in enumerate(vals):
        route = jnp.where(lane == c, val, route)
    route_ref[...] = route


def _proj_out(ym, yr, yn, x2, wo, g, rw, rb):
    n = x2.shape[0]
    tm = PROJ_TM

    def rows(w):
        return pl.BlockSpec((tm, w), lambda i: (i, 0))

    def full(shape):
        return pl.BlockSpec(shape, lambda i: (0, 0))

    ii = np.arange(tm)
    lstrict = jnp.asarray(ii[None, :] < ii[:, None], BF16)
    return pl.pallas_call(
        _proj_out_kernel,
        out_shape=(jax.ShapeDtypeStruct((n, D_MODEL), F32),
                   jax.ShapeDtypeStruct((n, D_MODEL), F32),
                   jax.ShapeDtypeStruct((n, LANES), F32),
                   jax.ShapeDtypeStruct((1, LANES), F32)),
        grid=(n // tm,),
        in_specs=[rows(W_MLSTM), rows(W_RET), rows(W_NA), rows(D_MODEL),
                  full((D_MODEL, D_MODEL)), full((1, D_MODEL)),
                  pl.BlockSpec((2, D_MODEL, LANES), lambda i: (0, 0, 0)),
                  full((1, LANES)), full((tm, tm))],
        out_specs=(rows(D_MODEL), rows(D_MODEL), rows(LANES), full((1, LANES))),
        compiler_params=_cparams(("arbitrary",)),
        name="proj_out_router",
    )(ym, yr, yn, x2, wo, g, rw, rb, lstrict)


def _dispatch_kernel(pend_ref, cnt_ref, nused_ref, meta_hbm, xn_ref, xs_hbm,
                     meta0, meta1, zero_buf, sem_meta, sem_rows, sem_zero):
    i = pl.program_id(0)
    nsteps = pl.num_programs(0)
    slot = lax.rem(i, 2)
    n_blocks = xs_hbm.shape[0] // MOE_BLK

    metas = (meta0, meta1)

    def meta_copy(step, s):
        return pltpu.make_async_copy(meta_hbm.at[step], metas[s], sem_meta.at[s])

    def zero_copy(blk):
        start = pl.multiple_of(blk * MOE_BLK, MOE_BLK)
        return pltpu.make_async_copy(zero_buf, xs_hbm.at[pl.ds(start, MOE_BLK), :], sem_zero)

    @pl.when(i == 0)
    def _():
        meta_copy(0, 0).start()
        zero_buf[...] = jnp.zeros_like(zero_buf)

        def zstart(e, carry):
            @pl.when(cnt_ref[e] > 0)
            def _():
                zero_copy(pend_ref[e] // MOE_BLK - 1).start()
            return carry

        def zwait(e, carry):
            @pl.when(cnt_ref[e] > 0)
            def _():
                zero_copy(0).wait()
            return carry

        lax.fori_loop(0, N_EXPERTS, zstart, 0)
        lax.fori_loop(nused_ref[0], n_blocks, lambda b, c: (zero_copy(b).start(), c)[1], 0)
        lax.fori_loop(0, N_EXPERTS, zwait, 0)
        lax.fori_loop(nused_ref[0], n_blocks, lambda b, c: (zero_copy(0).wait(), c)[1], 0)

    def row_copy(r, dst):
        return pltpu.make_async_copy(xn_ref.at[pl.ds(r, 1), :], xs_hbm.at[pl.ds(dst, 1), :],
                                     sem_rows)

    for s in (0, 1):
        @pl.when(slot == s)
        def _(s=s):
            meta_copy(i, s).wait()

            @pl.when(i + 1 < nsteps)
            def _():
                meta_copy(i + 1, 1 - s).start()

            def issue(r, carry):
                for kk in range(TOP_K):
                    row_copy(r, metas[s][TOP_K * r + kk]).start()
                return carry

            lax.fori_loop(0, COMB_TM, issue, 0, unroll=8)

    def drain(r, carry):
        for kk in range(TOP_K):
            row_copy(r, 0).wait()
        return carry

    lax.fori_loop(0, COMB_TM, drain, 0, unroll=8)


def _dispatch(pend, counts, n_used, meta, xn, n_slots):
    n = xn.shape[0]
    tm = COMB_TM
    grid_spec = pltpu.PrefetchScalarGridSpec(
        num_scalar_prefetch=3,
        grid=(n // tm,),
        in_specs=[pl.BlockSpec(memory_space=pl.ANY),
                  pl.BlockSpec((tm, D_MODEL), lambda i, *_: (i, 0))],
        out_specs=pl.BlockSpec(memory_space=pl.ANY),
        scratch_shapes=[pltpu.SMEM((TOP_K * tm,), jnp.int32),
                        pltpu.SMEM((TOP_K * tm,), jnp.int32),
                        pltpu.VMEM((MOE_BLK, D_MODEL), F32),
                        pltpu.SemaphoreType.DMA((2,)),
                        pltpu.SemaphoreType.DMA,
                        pltpu.SemaphoreType.DMA],
    )
    return pl.pallas_call(
        _dispatch_kernel,
        out_shape=jax.ShapeDtypeStruct((n_slots, D_MODEL), F32),
        grid_spec=grid_spec,
        compiler_params=_cparams(("arbitrary",)),
        name="dispatch",
    )(pend, counts, n_used, meta, xn)


def _expert_kernel(be_ref, nused_ref, xs_ref, w1_ref, w3_ref, w2_ref, y_ref):
    i = pl.program_id(0)

    @pl.when(i < nused_ref[0])
    def _():
        xb = xs_ref[...].astype(BF16)
        h1 = _dot(xb, w1_ref[...])
        h3 = _dot(xb, w3_ref[...])
        hb = (h1 * _sigmoid(h1) * h3).astype(BF16)
        y_ref[...] = _dot(hb, w2_ref[...])

    @pl.when(i >= nused_ref[0])
    def _():
        y_ref[...] = jnp.zeros_like(y_ref)


def _experts(block_expert, n_used, xs, w1, w3, w2):
    n_blocks = xs.shape[0] // MOE_BLK

    def wspec(shape):
        return pl.BlockSpec(shape, lambda i, be, nu: (be[i], 0, 0))

    grid_spec = pltpu.PrefetchScalarGridSpec(
        num_scalar_prefetch=2,
        grid=(n_blocks,),
        in_specs=[pl.BlockSpec((MOE_BLK, D_MODEL),
                               lambda i, be, nu: (jnp.minimum(i, nu[0] - 1), 0)),
                  wspec((None, D_MODEL, EXPERT_FF)), wspec((None, D_MODEL, EXPERT_FF)),
                  wspec((None, EXPERT_FF, D_MODEL))],
        out_specs=pl.BlockSpec((MOE_BLK, D_MODEL), lambda i, be, nu: (i, 0)),
    )
    return pl.pallas_call(
        _expert_kernel,
        out_shape=jax.ShapeDtypeStruct((n_blocks * MOE_BLK, D_MODEL), F32),
        grid_spec=grid_spec,
        compiler_params=_cparams(("arbitrary",)),
        name="experts",
    )(block_expert, n_used, xs, w1, w3, w2)


def _combine_kernel(meta_hbm, yb_hbm, x_ref, route_ref, g_ref, out_ref,
                    meta0, meta1, ybuf, sem_meta, sem_rows, *, final_norm):
    i = pl.program_id(0)
    nsteps = pl.num_programs(0)
    slot = lax.rem(i, 2)
    metas = (meta0, meta1)

    def meta_copy(step, s):
        return pltpu.make_async_copy(meta_hbm.at[step], metas[s], sem_meta.at[s])

    def row_copy(src, s, kk, r):
        return pltpu.make_async_copy(yb_hbm.at[pl.ds(src, 1), :],
                                     ybuf.at[s, kk, pl.ds(r, 1), :], sem_rows.at[s])

    def issue_tile(s):
        def issue(r, carry):
            for kk in range(TOP_K):
                row_copy(metas[s][TOP_K * r + kk], s, kk, r).start()
            return carry

        lax.fori_loop(0, COMB_TM, issue, 0, unroll=8)

    @pl.when(i == 0)
    def _():
        meta_copy(0, 0).start()
        meta_copy(0, 0).wait()
        issue_tile(0)

        @pl.when(nsteps > 1)
        def _():
            meta_copy(1, 1).start()

    for s in (0, 1):
        @pl.when(slot == s)
        def _(s=s):
            @pl.when(i + 1 < nsteps)
            def _():
                meta_copy(i + 1, 1 - s).wait()
                issue_tile(1 - s)

            @pl.when(i + 2 < nsteps)
            def _():
                meta_copy(i + 2, s).start()

            def drain(r, carry):
                for kk in range(TOP_K):
                    row_copy(0, s, kk, r).wait()
                return carry

            lax.fori_loop(0, COMB_TM, drain, 0, unroll=8)
            route = route_ref[...]
            y = x_ref[...]
            for kk in range(TOP_K):
                y = y + route[:, 2 + kk:3 + kk] * ybuf[s, kk]
            if final_norm:
                ms = jnp.mean(y * y, axis=-1, keepdims=True)
                y = y * lax.rsqrt(ms + RMS_EPS) * g_ref[...]
            out_ref[...] = y


def _combine(meta, yb, x2, route, g, final_norm):
    n = x2.shape[0]
    tm = COMB_TM
    kern = functools.partial(_combine_kernel, final_norm=final_norm)
    return pl.pallas_call(
        kern,
        out_shape=jax.ShapeDtypeStruct((n, D_MODEL), F32),
        grid=(n // tm,),
        in_specs=[pl.BlockSpec(memory_space=pl.ANY),
                  pl.BlockSpec(memory_space=pl.ANY),
                  pl.BlockSpec((tm, D_MODEL), lambda i: (i, 0)),
                  pl.BlockSpec((tm, LANES), lambda i: (i, 0)),
                  pl.BlockSpec((1, D_MODEL), lambda i: (0, 0))],
        out_specs=pl.BlockSpec((tm, D_MODEL), lambda i: (i, 0)),
        scratch_shapes=[pltpu.SMEM((TOP_K * tm,), jnp.int32),
                        pltpu.SMEM((TOP_K * tm,), jnp.int32),
                        pltpu.VMEM((2, TOP_K, tm, D_MODEL), F32),
                        pltpu.SemaphoreType.DMA((2,)),
                        pltpu.SemaphoreType.DMA((2,))],
        compiler_params=_cparams(("arbitrary",)),
        name="combine",
    )(meta, yb, x2, route, g)


def _dispatch_layout(route, counts_f, n):
    counts = counts_f[0, :N_EXPERTS].astype(jnp.int32)
    n_blocks = n * TOP_K // MOE_BLK + N_EXPERTS
    padded = (counts + MOE_BLK - 1) // MOE_BLK * MOE_BLK
    pend = jnp.cumsum(padded).astype(jnp.int32)
    pstart = pend - padded
    n_used = pend[-1] // MOE_BLK
    blk_start = jnp.arange(n_blocks, dtype=jnp.int32) * MOE_BLK
    blk_start = jnp.minimum(blk_start, (n_used - 1) * MOE_BLK)
    block_expert = jnp.sum((pend[None, :] <= blk_start[:, None]).astype(jnp.int32), axis=1)
    block_expert = jnp.clip(block_expert, 0, N_EXPERTS - 1).astype(jnp.int32)
    expert = route[:, 0:TOP_K].astype(jnp.int32)
    rank = route[:, 4:4 + TOP_K].astype(jnp.int32)
    seg = jnp.sum(jnp.where(expert[..., None] == jnp.arange(N_EXPERTS, dtype=jnp.int32),
                            pstart, 0), axis=-1)
    meta = (seg + rank).reshape(n // COMB_TM, TOP_K * COMB_TM)
    return block_expert, n_used.reshape(1), pend, counts, meta, n_blocks * MOE_BLK


def _const_tables():
    lane = np.arange(LANES)
    grp = lane // HEAD_DIM
    xe = np.zeros((2, 8, 2 * LANES), np.float32)
    for di in range(2):
        for j in range(2):
            xe[:, 2 * di + j, di * LANES + j * HEAD_DIM:di * LANES + (j + 1) * HEAD_DIM] = 1.0
    xc = np.zeros((2, 24, 8 * LANES), np.float32)
    for kk in range(4):
        xc[:, kk, kk * LANES:(kk + 1) * LANES] = 1.0
    for di in range(2):
        for j in range(2):
            for blk, base in ((4 + di, 8), (6 + di, 16)):
                lo = blk * LANES + j * HEAD_DIM
                xc[:, base + 2 * di + j, lo:lo + HEAD_DIM] = 1.0
    ii = np.arange(CHUNK)
    incl_f = (ii[:, None] <= ii[None, :]).astype(np.float32)
    incl_b = (ii[:, None] >= ii[None, :]).astype(np.float32)
    bd = (grp[:, None] == grp[None, :]).astype(np.float32)
    perm = np.zeros((LANES, LANES), np.float32)
    half = HEAD_DIM // 2
    src = (lane // HEAD_DIM) * HEAD_DIM + (lane % HEAD_DIM + half) % HEAD_DIM
    perm[src, lane] = 1.0
    sgn = np.where(lane % HEAD_DIM < half, -1.0, 1.0).astype(np.float32)[None, :]
    return {
        "xe": jnp.asarray(xe.reshape(16, 2 * LANES), BF16),
        "xc": jnp.asarray(xc.reshape(48, 8 * LANES), BF16),
        "trif": jnp.asarray(np.concatenate([incl_f, incl_f], 0), BF16),
        "trib": jnp.asarray(np.concatenate([incl_b, incl_b], 0), BF16),
        "bdm": jnp.asarray(bd, F32), "bdm2": jnp.asarray(np.concatenate([bd, bd], 1), F32),
        "bd2": jnp.asarray(np.concatenate([bd, bd], 0), BF16),
        "perm": jnp.asarray(perm, BF16), "sgn": jnp.asarray(sgn, F32),
    }


def _rope_tables(seq):
    half = HEAD_DIM // 2
    nc = seq // CHUNK
    inv_freq = ROPE_BASE ** (-np.arange(half, dtype=np.float64) / half)
    freq = inv_freq[np.arange(LANES) % half]
    ang_a = np.arange(CHUNK, dtype=np.float64)[:, None] * freq[None, :]
    ang_b = (np.arange(nc, dtype=np.float64) * CHUNK)[:, None] * freq[None, :]
    return {"ca": jnp.asarray(np.cos(ang_a), F32), "sa": jnp.asarray(np.sin(ang_a), F32),
            "cb": jnp.asarray(np.cos(ang_b)[:, None, :], F32),
            "sb": jnp.asarray(np.sin(ang_b)[:, None, :], F32)}


def _layer_weights(l, w_in, mlstm_gate_bias, ret_decay, w_out, router_group_w, router_group_b,
                   router_expert_w, router_expert_b):
    sizes = (W_MLSTM,) * 4 + (4 * H_MLSTM,) + (W_RET,) * 4 + (W_NA,) * 3
    offs = np.concatenate([[0], np.cumsum(sizes)])
    col = lambda i: w_in[l][:, int(offs[i]):int(offs[i + 1])]
    mq, mk, mv, mo, mg, rq, rk, rv, rg, nq, nk, nv = [col(i) for i in range(12)]
    scale = HEAD_DIM ** -0.5
    w_all = jnp.concatenate([mq, mk, mv, mo, rq, rk * scale, rv, rg, nq * scale, nk, nv],
                            axis=1).astype(BF16)
    sel = np.zeros((GATE_ROWS, 4 * H_MLSTM), np.float32)
    for kind in range(2):
        for p in range(H_MLSTM // 2):
            for k in range(4):
                sel[(kind * (H_MLSTM // 2) + p) * 8 + k,
                    (2 * kind + k // 2) * H_MLSTM + 2 * p + k % 2] = 1.0
    sel = jnp.asarray(sel)
    wg = jnp.dot(sel, mg.T, precision=lax.Precision.HIGHEST).astype(BF16)
    gbias = jnp.dot(sel, mlstm_gate_bias[l].astype(F32).reshape(-1, 1),
                    precision=lax.Precision.HIGHEST)
    rd = ret_decay[l].astype(F32)
    rd = jnp.repeat(rd.reshape(2, H_RET // 2, 2), HEAD_DIM, axis=2)
    rd = rd.transpose(1, 0, 2)[:, :, None, :]
    wo = w_out[l].astype(BF16)
    rw = jnp.concatenate([router_group_w[l], router_expert_w[l],
                          jnp.zeros((D_MODEL, LANES - N_GROUPS - N_EXPERTS), F32)], axis=1)
    rb = jnp.concatenate([router_group_b[l].astype(F32), router_expert_b[l].astype(F32),
                          jnp.zeros((LANES - N_GROUPS - N_EXPERTS,), F32)])[None, :]
    rw_hi = rw.astype(BF16)
    rw2 = jnp.stack([rw_hi, (rw - rw_hi.astype(F32)).astype(BF16)])
    return w_all, wg, gbias, rd, wo, rw2, rb


def _encoder(x, consts, norm_mix, w_in, mlstm_conv_w, mlstm_conv_b, mlstm_gate_bias, mlstm_norm,
             ret_decay, ret_norm, na_rpb, w_out, norm_ffn, router_group_w, router_group_b,
             router_expert_w, router_expert_b, expert_w1, expert_w3, expert_w2, norm_final):
    b, seq, _ = x.shape
    n = b * seq
    depth = w_in.shape[0]
    rope = _rope_tables(seq)
    x2 = x.reshape(n, D_MODEL).astype(F32)
    for l in range(depth):
        w_all, wg, gbias, rd, wo, rw, rb = _layer_weights(
            l, w_in, mlstm_gate_bias, ret_decay, w_out, router_group_w, router_group_b,
            router_expert_w, router_expert_b)
        z, gates_t = _proj_in(x2, norm_mix[l].astype(F32)[None, :], w_all, wg)
        z3 = z.reshape(b, seq, Z_WIDTH)
        y_m = _mlstm(z3, gates_t, gbias, mlstm_conv_w[l].astype(F32),
                     mlstm_conv_b[l].astype(F32)[None, :], mlstm_norm[l].astype(F32)[None, :], consts)
        y_r = _retention(z3, rd, ret_norm[l].astype(F32)[None, :], consts, rope)
        y_n = _na(z3, _na_bias(na_rpb[l], seq))
        x2, xn, route, counts_f = _proj_out(y_m.reshape(n, W_MLSTM), y_r.reshape(n, W_RET),
                                            y_n.reshape(n, W_NA), x2, wo,
                                            norm_ffn[l].astype(F32)[None, :], rw, rb)
        block_expert, n_used, pend, counts, meta, n_slots = _dispatch_layout(route, counts_f, n)
        xs = _dispatch(pend, counts, n_used, meta, xn, n_slots)
        yb = _experts(block_expert, n_used, xs,
                      expert_w1[l].astype(BF16), expert_w3[l].astype(BF16), expert_w2[l].astype(BF16))
        x2 = _combine(meta, yb, x2, route, norm_final.astype(F32)[None, :],
                      final_norm=(l == depth - 1))
    return x2.reshape(b, seq, D_MODEL)


def kernel(x_prompt, x_sample, norm_mix, w_in, mlstm_conv_w, mlstm_conv_b, mlstm_gate_bias,
           mlstm_norm, ret_decay, ret_norm, na_rpb, w_out, norm_ffn, router_group_w,
           router_group_b, router_expert_w, router_expert_b, expert_w1, expert_w3, expert_w2,
           norm_final):
    consts = _const_tables()
    weights = (norm_mix, w_in, mlstm_conv_w, mlstm_conv_b, mlstm_gate_bias, mlstm_norm,
               ret_decay, ret_norm, na_rpb, w_out, norm_ffn, router_group_w, router_group_b,
               router_expert_w, router_expert_b, expert_w1, expert_w3, expert_w2, norm_final)
    return (_encoder(x_prompt, consts, *weights), _encoder(x_sample, consts, *weights))
```

```python
import functools

import numpy as np
import jax
import jax.numpy as jnp
from jax import lax
from jax.experimental import pallas as pl
from jax.experimental.pallas import tpu as pltpu

F32 = jnp.float32
BF16 = jnp.bfloat16

D_MODEL = 1024
HEAD_DIM = 64
LANES = 128
CHUNK = 128
H_MLSTM = 6
H_RET = 6
H_NA = 4
W_MLSTM = H_MLSTM * HEAD_DIM
W_RET = H_RET * HEAD_DIM
W_NA = H_NA * HEAD_DIM
Z_WIDTH = 4 * W_MLSTM + 4 * W_RET + 3 * W_NA
GATE_ROWS = 2 * (H_MLSTM // 2) * 8
ROPE_BASE = 10000.0
GRID_W = 64
NA_ROWS = 8
NA_COLS = 16
NA_QROWS = 4
NA_KROWS = NA_QROWS + NA_ROWS
N_GROUPS = 4
EXPERTS_PER_GROUP = 8
N_EXPERTS = 32
TOP_K = 2
EXPERT_FF = 512
RMS_EPS = 1e-6
NEG = -1e30
VMEM_LIMIT = 56 * 1024 * 1024

PROJ_TM = 512
MOE_BLK = 256
COMB_TM = 256
MIX_UNROLL = 2
RET_UNROLL = 4
GATE_GROUP = 8

ZB_MQ, ZB_MK, ZB_MV, ZB_MO = 0, 3, 6, 9
ZB_RQ, ZB_RK, ZB_RV, ZB_RG = 12, 15, 18, 21
ZB_NQ, ZB_NK, ZB_NV = 24, 26, 28


def _dot(a, b):
    return jnp.dot(a, b, preferred_element_type=F32)


def _dot_nt(a, b):
    return lax.dot_general(a, b, (((1,), (1,)), ((), ())), preferred_element_type=F32)


def _dot_tn(a, b):
    return lax.dot_general(a, b, (((0,), (0,)), ((), ())), preferred_element_type=F32)


def _split(x, axis):
    hi = x.astype(BF16).astype(F32)
    return jnp.concatenate([hi, x - hi], axis=axis).astype(BF16)


def _sigmoid(x):
    return 1.0 / (1.0 + jnp.exp(-x))


def _log_sigmoid(x):
    return -(jnp.maximum(-x, 0.0) + jnp.log(1.0 + jnp.exp(-jnp.abs(x))))


def _cparams(sem):
    return pltpu.CompilerParams(dimension_semantics=sem, vmem_limit_bytes=VMEM_LIMIT)


def _proj_in_kernel(x_ref, g_ref, w_ref, wg_ref, z_ref, gate_ref):
    x = x_ref[...]
    ms = jnp.mean(x * x, axis=-1, keepdims=True)
    hx = (x * lax.rsqrt(ms + RMS_EPS) * g_ref[...]).astype(BF16)
    cw = 768
    for j in range(0, Z_WIDTH, cw):
        z_ref[:, j:j + cw] = _dot(hx, w_ref[:, j:j + cw]).astype(BF16)
    gate_ref[...] = _dot_nt(wg_ref[...], hx)


def _proj_in(x2, g, w, wg):
    n = x2.shape[0]
    return pl.pallas_call(
        _proj_in_kernel,
        out_shape=(jax.ShapeDtypeStruct((n, Z_WIDTH), BF16),
                   jax.ShapeDtypeStruct((GATE_ROWS, n), F32)),
        grid=(n // PROJ_TM,),
        in_specs=[pl.BlockSpec((PROJ_TM, D_MODEL), lambda i: (i, 0)),
                  pl.BlockSpec((1, D_MODEL), lambda i: (0, 0)),
                  pl.BlockSpec((D_MODEL, Z_WIDTH), lambda i: (0, 0)),
                  pl.BlockSpec((GATE_ROWS, D_MODEL), lambda i: (0, 0))],
        out_specs=(pl.BlockSpec((PROJ_TM, Z_WIDTH), lambda i: (i, 0)),
                   pl.BlockSpec((GATE_ROWS, PROJ_TM), lambda i: (0, i))),
        compiler_params=_cparams(("arbitrary",)),
        name="proj_in",
    )(x2, g, w, wg)


def _lane_lt64():
    return lax.broadcasted_iota(jnp.int32, (CHUNK, LANES), 1) < HEAD_DIM


def _tri_mask(d):
    r = lax.broadcasted_iota(jnp.int32, (CHUNK, CHUNK), 0)
    c = lax.broadcasted_iota(jnp.int32, (CHUNK, CHUNK), 1)
    return (c <= r) if d == 0 else (c >= r)


def _head_norm(h, bd2_ref):
    ms = _dot(_split(h * h, 1), bd2_ref[...]) * (1.0 / HEAD_DIM)
    return h * lax.rsqrt(ms + RMS_EPS)


def _mlstm_kernel(q_ref, k_ref, v_ref, o_ref, li_ref, lf_ref, bli_ref, blf_ref, cwq_ref, cwk_ref,
                  cbq_ref, cbk_ref, nw_ref, trif_ref, trib_ref, xc_ref, bdm_ref, bd2_ref,
                  out_ref, qc_ref, kc_ref, kvf_ref, kvb_ref, st_ref, rr_ref, aa_ref, cc_ref,
                  atot_ref, mloc_ref, mpf_ref, mpb_ref, ee_ref, *, seq):
    nc = seq // CHUNK
    lane_lo = _lane_lt64()
    row = lax.broadcasted_iota(jnp.int32, (CHUNK, LANES), 0)
    grow = lax.broadcasted_iota(jnp.int32, (8, CHUNK), 0)
    glane = lax.broadcasted_iota(jnp.int32, (8, CHUNK), 1)
    is_fwd = grow < 2
    ones_aug = jnp.ones((CHUNK, LANES), BF16)
    tri_f = _tri_mask(0)
    tri_b = _tri_mask(1)

    def conv_chunk(src_ref, w_ref, b_ref, c, scale):
        t0 = pl.multiple_of(c * CHUNK, CHUNK)
        x = src_ref[pl.ds(t0, CHUNK), :].astype(F32)
        tp = pl.multiple_of(jnp.maximum(t0 - 16, 0), 16)
        tn = pl.multiple_of(jnp.minimum(t0 + CHUNK, seq - 16), 16)
        prev_last = src_ref[pl.ds(tp, 16), :].astype(F32)[15:16, :]
        next_first = src_ref[pl.ds(tn, 16), :].astype(F32)[0:1, :]
        prev_last = prev_last * jnp.where(c > 0, 1.0, 0.0)
        next_first = next_first * jnp.where(c < nc - 1, 1.0, 0.0)
        xm1 = jnp.where(row == 0, prev_last, pltpu.roll(x, 1, 0))
        xp1 = jnp.where(row == CHUNK - 1, next_first, pltpu.roll(x, CHUNK - 1, 0))
        y = w_ref[0:1, :] * xm1 + w_ref[1:2, :] * x + w_ref[2:3, :] * xp1 + b_ref[...]
        return y * _sigmoid(y) * scale

    def v_aug_at(t0):
        return jnp.concatenate([v_ref[pl.ds(t0, CHUNK), :], ones_aug], axis=1)

    rows_g = GATE_GROUP * 8
    grow_g = lax.broadcasted_iota(jnp.int32, (rows_g, CHUNK), 0)
    glane_g = lax.broadcasted_iota(jnp.int32, (rows_g, CHUNK), 1)
    is_fwd_g = lax.rem(grow_g, 8) < 2

    def stack_chunks(w):
        return jnp.concatenate([w[:, i * CHUNK:(i + 1) * CHUNK] for i in range(GATE_GROUP)],
                               axis=0)

    def gate_body(gi, carry):
        c0 = pl.multiple_of(gi * GATE_GROUP, GATE_GROUP)
        t0 = pl.multiple_of(gi * (GATE_GROUP * CHUNK), GATE_GROUP * CHUNK)
        li = stack_chunks(li_ref[:, pl.ds(t0, GATE_GROUP * CHUNK)] + bli_ref[...])
        lf = stack_chunks(_log_sigmoid(lf_ref[:, pl.ds(t0, GATE_GROUP * CHUNK)] + blf_ref[...]))
        lfs = _split(lf, 1)
        a = jnp.where(is_fwd_g, _dot(lfs, trif_ref[...]), _dot(lfs, trib_ref[...]))
        r = a - li
        xf = -r
        xb = -r
        s = 1
        while s < CHUNK:
            xf = jnp.maximum(xf, jnp.where(glane_g >= s, pltpu.roll(xf, s, 1), NEG))
            xb = jnp.maximum(xb, jnp.where(glane_g < CHUNK - s, pltpu.roll(xb, CHUNK - s, 1), NEG))
            s *= 2
        a_tot = jnp.where(is_fwd_g, a[:, CHUNK - 1:CHUNK], a[:, 0:1])
        w_loc = a_tot - r
        m_loc = jnp.max(w_loc, axis=1, keepdims=True)
        tiles = ((rr_ref, r), (aa_ref, a), (cc_ref, jnp.where(is_fwd_g, xf, xb)),
                 (atot_ref, a_tot), (mloc_ref, jnp.broadcast_to(m_loc, (rows_g, CHUNK))),
                 (ee_ref, jnp.exp(w_loc - m_loc)))
        for ref, val in tiles:
            ref[pl.ds(c0, GATE_GROUP)] = val.reshape(GATE_GROUP, 8, CHUNK)
        return carry

    lax.fori_loop(0, nc // GATE_GROUP, gate_body, 0)

    def pre_body(c, carry):
        t0 = pl.multiple_of(c * CHUNK, CHUNK)
        qc_ref[pl.ds(t0, CHUNK), :] = conv_chunk(q_ref, cwq_ref, cbq_ref, c, 1.0).astype(BF16)
        kc = conv_chunk(k_ref, cwk_ref, cbk_ref, c, HEAD_DIM ** -0.5).astype(BF16)
        kc_ref[pl.ds(t0, CHUNK), :] = kc
        k_t = kc.astype(F32).T
        e = ee_ref[c]
        v_aug = v_aug_at(t0)
        for di, kv_ref in enumerate((kvf_ref, kvb_ref)):
            e_rows = jnp.where(row < HEAD_DIM, e[2 * di:2 * di + 1, :], e[2 * di + 1:2 * di + 2, :])
            kv = _dot((k_t * e_rows).astype(BF16), v_aug) * bdm_ref[...]
            kv_ref[c] = kv.astype(BF16)
        return carry

    lax.fori_loop(0, nc, pre_body, 0, unroll=4)

    st_ref[...] = jnp.zeros_like(st_ref)

    def lane_pair(t, k0):
        v = jnp.where(lane_lo[0:1, :], t[k0:k0 + 1, :], t[k0 + 1:k0 + 2, :])
        return jnp.concatenate([v, v], axis=1)

    def scan_body(t, m):
        cf = t
        cb = nc - 1 - t
        atot = jnp.where(is_fwd, atot_ref[cf], atot_ref[cb])
        mloc = jnp.where(is_fwd, mloc_ref[cf], mloc_ref[cb])
        mpf_ref[cf] = m
        mpb_ref[cb] = m
        m_new = jnp.maximum(atot + m, mloc)
        s_old = jnp.exp(atot + m - m_new)
        s_new = jnp.exp(mloc - m_new)
        for di, (kv_ref, cidx) in enumerate(((kvf_ref, cf), (kvb_ref, cb))):
            kv = kv_ref[cidx].astype(F32)
            s_st = st_ref[di]
            kv_ref[cidx] = s_st.astype(BF16)
            st_ref[di] = lane_pair(s_old, 2 * di) * s_st + lane_pair(s_new, 2 * di) * kv
        return m_new

    lax.fori_loop(0, nc, scan_body, jnp.zeros((8, CHUNK), F32), unroll=MIX_UNROLL)

    def out_body(c, carry):
        t0 = pl.multiple_of(c * CHUNK, CHUNK)
        q = qc_ref[pl.ds(t0, CHUNK), :]
        k = kc_ref[pl.ds(t0, CHUNK), :]
        v_aug = v_aug_at(t0)
        mp = jnp.where(is_fwd, mpf_ref[c], mpb_ref[c])
        u = -jnp.maximum(cc_ref[c], mp)
        cols = jnp.concatenate([u, jnp.exp(mp + u), jnp.exp(u - aa_ref[c])], axis=0)
        bc = _dot_tn(_split(cols, 0), xc_ref[...])
        r = rr_ref[c]
        scores = []
        for j in (0, 1):
            qj = jnp.where(lane_lo if j == 0 else jnp.logical_not(lane_lo), q, jnp.zeros_like(q))
            scores.append(_dot_nt(qj, k))
        h = None
        for di, (kv_ref, tri) in enumerate(((kvf_ref, tri_f), (kvb_ref, tri_b))):
            q_s = _dot(q, kv_ref[c])
            pvs = []
            for j in (0, 1):
                kk = 2 * di + j
                arg = jnp.where(tri, bc[:, kk * LANES:(kk + 1) * LANES] - r[kk:kk + 1, :], NEG)
                pvs.append(_dot((scores[j] * jnp.exp(arg)).astype(BF16), v_aug))
            sint = bc[:, (4 + di) * LANES:(5 + di) * LANES]
            em = bc[:, (6 + di) * LANES:(7 + di) * LANES]
            num = jnp.where(lane_lo, pvs[0][:, :LANES], pvs[1][:, :LANES]) + sint * q_s[:, :LANES]
            den = jnp.where(lane_lo, pvs[0][:, LANES:], pvs[1][:, LANES:]) + sint * q_s[:, LANES:]
            hd = num / jnp.maximum(jnp.abs(den), em)
            h = hd if h is None else h + hd
        y = _head_norm(h, bd2_ref) * nw_ref[...]
        y = y * _sigmoid(o_ref[pl.ds(t0, CHUNK), :].astype(F32))
        out_ref[pl.ds(t0, CHUNK), :] = y.astype(BF16)
        return carry

    lax.fori_loop(0, nc, out_body, 0, unroll=4)


def _mlstm(z3, gates_t, gbias, conv_w, conv_b, norm_w, consts):
    b, seq, _ = z3.shape
    nc = seq // CHUNK
    npair = H_MLSTM // 2

    def zspec(blk0):
        return pl.BlockSpec((None, seq, LANES), lambda bi, p, blk0=blk0: (bi, 0, blk0 + p))

    def full2(shape):
        return pl.BlockSpec(shape, lambda bi, p: (0, 0))

    gate_tile = pltpu.VMEM((nc, 8, CHUNK), F32)
    kern = functools.partial(_mlstm_kernel, seq=seq)
    return pl.pallas_call(
        kern,
        out_shape=jax.ShapeDtypeStruct((b, seq, W_MLSTM), BF16),
        grid=(b, npair),
        in_specs=[zspec(ZB_MQ), zspec(ZB_MK), zspec(ZB_MV), zspec(ZB_MO),
                  pl.BlockSpec((8, seq), lambda bi, p: (p, bi)),
                  pl.BlockSpec((8, seq), lambda bi, p: (npair + p, bi)),
                  pl.BlockSpec((8, 1), lambda bi, p: (p, 0)),
                  pl.BlockSpec((8, 1), lambda bi, p: (npair + p, 0)),
                  pl.BlockSpec((3, LANES), lambda bi, p: (0, p)),
                  pl.BlockSpec((3, LANES), lambda bi, p: (0, npair + p)),
                  pl.BlockSpec((1, LANES), lambda bi, p: (0, p)),
                  pl.BlockSpec((1, LANES), lambda bi, p: (0, npair + p)),
                  pl.BlockSpec((1, LANES), lambda bi, p: (0, p)),
                  full2((2 * CHUNK, CHUNK)), full2((2 * CHUNK, CHUNK)),
                  full2((48, 8 * LANES)),
                  full2((LANES, 2 * LANES)), full2((2 * LANES, LANES))],
        out_specs=pl.BlockSpec((None, seq, LANES), lambda bi, p: (bi, 0, p)),
        scratch_shapes=[pltpu.VMEM((seq, LANES), BF16),
                        pltpu.VMEM((seq, LANES), BF16),
                        pltpu.VMEM((nc, CHUNK, 2 * LANES), BF16),
                        pltpu.VMEM((nc, CHUNK, 2 * LANES), BF16),
                        pltpu.VMEM((2, CHUNK, 2 * LANES), F32)] + [gate_tile] * 8,
        compiler_params=_cparams(("arbitrary", "arbitrary")),
        name="mlstm",
    )(z3, z3, z3, z3, gates_t, gates_t, gbias, gbias, conv_w, conv_w, conv_b, conv_b, norm_w,
      consts["trif"], consts["trib"], consts["xc"], consts["bdm2"], consts["bd2"])


def _ret_kernel(q_ref, k_ref, v_ref, gt_ref, rd_ref, nw_ref, ca_ref, sa_ref, cb_ref, sb_ref,
                sgn_ref, perm_ref, bdm_ref, bd2_ref, out_ref,
                qr_ref, kr_ref, kvf_ref, kvb_ref, st_ref, tab_ref, *, seq):
    nc = seq // CHUNK
    lane_lo = _lane_lt64()
    rowf = lax.broadcasted_iota(jnp.int32, (CHUNK, LANES), 0).astype(F32)
    ri = lax.broadcasted_iota(jnp.int32, (CHUNK, CHUNK), 0)
    ci = lax.broadcasted_iota(jnp.int32, (CHUNK, CHUNK), 1)
    dist = (ri - ci).astype(F32)
    lg_f = -jnp.exp(rd_ref[0])
    lg_b = -jnp.exp(rd_ref[1])
    tab_ref[0] = jnp.exp(lg_f * (CHUNK - 1.0 - rowf))
    tab_ref[1] = jnp.exp(lg_b * rowf)
    tab_ref[2] = jnp.exp(lg_f * (rowf + 1.0))
    tab_ref[3] = jnp.exp(lg_b * (CHUNK - rowf))
    for j in (0, 1):
        lgf_j = lg_f[:, HEAD_DIM * j:HEAD_DIM * j + 1]
        lgb_j = lg_b[:, HEAD_DIM * j:HEAD_DIM * j + 1]
        tab_ref[4 + j] = (jnp.where(dist >= 0.0, jnp.exp(lgf_j * jnp.maximum(dist, 0.0)), 0.0)
                          + jnp.where(dist <= 0.0, jnp.exp(lgb_j * jnp.maximum(-dist, 0.0)), 0.0))
    cdec_f = jnp.exp(lg_f * float(CHUNK))
    cdec_b = jnp.exp(lg_b * float(CHUNK))

    def pre_body(c, carry):
        t0 = pl.multiple_of(c * CHUNK, CHUNK)
        cb = cb_ref[c]
        sb = sb_ref[c]
        cos = ca_ref[...] * cb - sa_ref[...] * sb
        sin = (sa_ref[...] * cb + ca_ref[...] * sb) * sgn_ref[...]
        rot = []
        for src, dst in ((q_ref, qr_ref), (k_ref, kr_ref)):
            x = src[pl.ds(t0, CHUNK), :]
            xs = _dot(x, perm_ref[...])
            xr = (x.astype(F32) * cos + xs * sin).astype(BF16)
            dst[pl.ds(t0, CHUNK), :] = xr
            rot.append(xr)
        kf = rot[1].astype(F32)
        v = v_ref[pl.ds(t0, CHUNK), :]
        kvf_ref[c] = (_dot_tn((kf * tab_ref[0]).astype(BF16), v) * bdm_ref[...]).astype(BF16)
        kvb_ref[c] = (_dot_tn((kf * tab_ref[1]).astype(BF16), v) * bdm_ref[...]).astype(BF16)
        return carry

    lax.fori_loop(0, nc, pre_body, 0, unroll=RET_UNROLL)

    st_ref[...] = jnp.zeros_like(st_ref)

    def scan_body(t, carry):
        for di, (kv_ref, cidx, cdec) in enumerate(((kvf_ref, t, cdec_f),
                                                   (kvb_ref, nc - 1 - t, cdec_b))):
            kv = kv_ref[cidx].astype(F32)
            s_st = st_ref[di]
            kv_ref[cidx] = s_st.astype(BF16)
            st_ref[di] = cdec * s_st + kv
        return carry

    lax.fori_loop(0, nc, scan_body, 0, unroll=RET_UNROLL)

    def out_body(c, carry):
        t0 = pl.multiple_of(c * CHUNK, CHUNK)
        q = qr_ref[pl.ds(t0, CHUNK), :]
        k = kr_ref[pl.ds(t0, CHUNK), :]
        v = v_ref[pl.ds(t0, CHUNK), :]
        qf = q.astype(F32)
        q_dec = jnp.concatenate([(qf * tab_ref[2]).astype(BF16), (qf * tab_ref[3]).astype(BF16)],
                                axis=1)
        states = jnp.concatenate([kvf_ref[c], kvb_ref[c]], axis=0)
        ys = []
        for j in (0, 1):
            qj = jnp.where(lane_lo if j == 0 else jnp.logical_not(lane_lo), q, jnp.zeros_like(q))
            s = _dot_nt(qj, k) * tab_ref[4 + j]
            ys.append(_dot(s.astype(BF16), v))
        h = jnp.where(lane_lo, ys[0], ys[1]) + _dot(q_dec, states)
        y = _head_norm(h, bd2_ref) * nw_ref[...]
        gt = gt_ref[pl.ds(t0, CHUNK), :].astype(F32)
        y = y * (gt * _sigmoid(gt))
        out_ref[pl.ds(t0, CHUNK), :] = y.astype(BF16)
        return carry

    lax.fori_loop(0, nc, out_body, 0, unroll=RET_UNROLL)


def _retention(z3, rd, norm_w, consts, rope):
    b, seq, _ = z3.shape
    nc = seq // CHUNK
    npair = H_RET // 2

    def zspec(blk0):
        return pl.BlockSpec((None, seq, LANES), lambda bi, p, blk0=blk0: (bi, 0, blk0 + p))

    def full2(shape):
        return pl.BlockSpec(shape, lambda bi, p: (0, 0))

    kern = functools.partial(_ret_kernel, seq=seq)
    return pl.pallas_call(
        kern,
        out_shape=jax.ShapeDtypeStruct((b, seq, W_RET), BF16),
        grid=(b, npair),
        in_specs=[zspec(ZB_RQ), zspec(ZB_RK), zspec(ZB_RV), zspec(ZB_RG),
                  pl.BlockSpec((None, 2, 1, LANES), lambda bi, p: (p, 0, 0, 0)),
                  pl.BlockSpec((1, LANES), lambda bi, p: (0, p)),
                  full2((CHUNK, LANES)), full2((CHUNK, LANES)),
                  pl.BlockSpec((nc, 1, LANES), lambda bi, p: (0, 0, 0)),
                  pl.BlockSpec((nc, 1, LANES), lambda bi, p: (0, 0, 0)),
                  full2((1, LANES)), full2((LANES, LANES)), full2((LANES, LANES)),
                  full2((2 * LANES, LANES))],
        out_specs=pl.BlockSpec((None, seq, LANES), lambda bi, p: (bi, 0, p)),
        scratch_shapes=[pltpu.VMEM((seq, LANES), BF16),
                        pltpu.VMEM((seq, LANES), BF16),
                        pltpu.VMEM((nc, CHUNK, LANES), BF16),
                        pltpu.VMEM((nc, CHUNK, LANES), BF16),
                        pltpu.VMEM((2, CHUNK, LANES), F32),
                        pltpu.VMEM((6, CHUNK, LANES), F32)],
        compiler_params=_cparams(("arbitrary", "arbitrary")),
        name="retention",
    )(z3, z3, z3, z3, rd, norm_w, rope["ca"], rope["sa"], rope["cb"], rope["sb"],
      consts["sgn"], consts["perm"], consts["bdm"], consts["bd2"])


def _na_kernel(q_ref, k_ref, v_ref, bias_ref, out_ref, *, seq):
    rows = seq // GRID_W
    ng = rows // NA_QROWS
    nq = NA_QROWS * GRID_W
    nk = NA_KROWS * GRID_W
    lane_lo = lax.broadcasted_iota(jnp.int32, (nq, LANES), 1) < HEAD_DIM
    ones_aug = jnp.ones((nk, LANES), BF16)

    def body(g, carry):
        t0 = pl.multiple_of(g * nq, nq)
        base = jnp.clip(g * NA_QROWS - NA_ROWS // 2, 0, rows - NA_KROWS)
        k0 = pl.multiple_of(base * GRID_W, GRID_W)
        case = jnp.where(g == 0, 0, jnp.where(g == ng - 1, 2, 1))
        q = q_ref[pl.ds(t0, nq), :]
        kk = k_ref[pl.ds(k0, nk), :]
        v_aug = jnp.concatenate([v_ref[pl.ds(k0, nk), :], ones_aug], axis=1)
        outs = []
        for j in (0, 1):
            qj = jnp.where(lane_lo if j == 0 else jnp.logical_not(lane_lo), q,
                           jnp.zeros_like(q))
            s = _dot_nt(qj, kk) + bias_ref[case, j]
            m = jnp.max(s, axis=1, keepdims=True)
            e = jnp.exp(s - m)
            pv = _dot(e.astype(BF16), v_aug)
            outs.append(pv[:, :LANES] / pv[:, LANES:])
        out_ref[pl.ds(t0, nq), :] = jnp.where(lane_lo, outs[0], outs[1]).astype(BF16)
        return carry

    lax.fori_loop(0, ng, body, 0, unroll=2)


def _na(z3, bias):
    b, seq, _ = z3.shape
    npair = H_NA // 2
    nq = NA_QROWS * GRID_W
    nk = NA_KROWS * GRID_W

    def zspec(blk0):
        return pl.BlockSpec((None, seq, LANES), lambda bi, p, blk0=blk0: (bi, 0, blk0 + p))

    kern = functools.partial(_na_kernel, seq=seq)
    return pl.pallas_call(
        kern,
        out_shape=jax.ShapeDtypeStruct((b, seq, W_NA), BF16),
        grid=(b, npair),
        in_specs=[zspec(ZB_NQ), zspec(ZB_NK), zspec(ZB_NV),
                  pl.BlockSpec((None, 3, 2, nq, nk), lambda bi, p: (p, 0, 0, 0, 0))],
        out_specs=pl.BlockSpec((None, seq, LANES), lambda bi, p: (bi, 0, p)),
        compiler_params=_cparams(("arbitrary", "arbitrary")),
        name="natten",
    )(z3, z3, z3, bias)


def _na_index_tables(seq):
    rows = seq // GRID_W
    ng = rows // NA_QROWS
    tabs = []
    for g in (0, 1, ng - 1):
        base = int(np.clip(g * NA_QROWS - NA_ROWS // 2, 0, rows - NA_KROWS))
        qr = g * NA_QROWS + np.arange(NA_QROWS)[:, None, None, None]
        qc = np.arange(GRID_W)[None, :, None, None]
        kr = base + np.arange(NA_KROWS)[None, None, :, None]
        kc = np.arange(GRID_W)[None, None, None, :]
        rstart = np.clip(qr - NA_ROWS // 2, 0, rows - NA_ROWS)
        cstart = np.clip(qc - NA_COLS // 2, 0, GRID_W - NA_COLS)
        ok = (kr >= rstart) & (kr < rstart + NA_ROWS) & (kc >= cstart) & (kc < cstart + NA_COLS)
        rel_r = np.clip(kr - qr + NA_ROWS - 1, 0, 2 * NA_ROWS - 2)
        rel_c = np.clip(kc - qc + NA_COLS - 1, 0, 2 * NA_COLS - 2)
        shp = (NA_QROWS * GRID_W, NA_KROWS * GRID_W)
        full = (NA_QROWS, GRID_W, NA_KROWS, GRID_W)
        tabs.append((np.broadcast_to(ok, full).reshape(shp),
                     np.broadcast_to(rel_r, full).reshape(shp),
                     np.broadcast_to(rel_c, full).reshape(shp)))
    ok = np.stack([t[0] for t in tabs])
    rr = np.stack([t[1] for t in tabs])
    rc = np.stack([t[2] for t in tabs])
    return ok, rr, rc


def _na_bias(rpb, seq):
    ok, rr, rc = _na_index_tables(seq)
    h = rpb.shape[0]
    n_r, n_c = 2 * NA_ROWS - 1, 2 * NA_COLS - 1
    full = (3, NA_QROWS, GRID_W, NA_KROWS, GRID_W)
    rr5, rc5 = rr.reshape(full), rc.reshape(full)
    oh_r = (rr5[:, :, 0, :, 0][..., None] == np.arange(n_r)).astype(np.float32)
    oh_c = (rc5[0, 0, :, 0, :][..., None] == np.arange(n_c)).astype(np.float32)
    cols = jnp.einsum("hab,qkb->haqk", rpb.astype(F32), jnp.asarray(oh_c),
                      precision=lax.Precision.HIGHEST)
    bias = jnp.einsum("ciea,haqk->hciqek", jnp.asarray(oh_r), cols,
                      precision=lax.Precision.HIGHEST)
    bias = bias.reshape(h, 3, NA_QROWS * GRID_W, NA_KROWS * GRID_W)
    bias = jnp.where(ok[None], bias, NEG)
    return bias.reshape(h // 2, 2, 3, bias.shape[2], bias.shape[3]).transpose(0, 2, 1, 3, 4)


def _proj_out_kernel(ym_ref, yr_ref, yn_ref, x_ref, wo_ref, g_ref, rw_ref,
                     rb_ref, lst_ref, x2_ref, xn_ref, route_ref, cnt_ref):
    @pl.when(pl.program_id(0) == 0)
    def _():
        cnt_ref[...] = jnp.zeros_like(cnt_ref)

    y_cat = jnp.concatenate([ym_ref[...], yr_ref[...], yn_ref[...]], axis=1)
    x2 = x_ref[...] + _dot(y_cat, wo_ref[...])
    x2_ref[...] = x2
    ms = jnp.mean(x2 * x2, axis=-1, keepdims=True)
    xn = x2 * lax.rsqrt(ms + RMS_EPS) * g_ref[...]
    xn_ref[...] = xn
    xh = xn.astype(BF16)
    xl = (xn - xh.astype(F32)).astype(BF16)
    logits = (_dot(xh, rw_ref[0]) + _dot(xl, rw_ref[0]) + _dot(xh, rw_ref[1])) + rb_ref[...]
    tm = logits.shape[0]
    lane = lax.broadcasted_iota(jnp.int32, (tm, LANES), 1).astype(F32)
    glog = jnp.where(lane < N_GROUPS, logits, NEG)
    gmax = jnp.max(glog, axis=1, keepdims=True)
    grp = jnp.min(jnp.where(glog == gmax, lane, float(LANES)), axis=1, keepdims=True)
    p_grp = 1.0 / jnp.sum(jnp.exp(glog - gmax), axis=1, keepdims=True)
    lo = N_GROUPS + grp * EXPERTS_PER_GROUP
    ein = jnp.where((lane >= lo) & (lane < lo + EXPERTS_PER_GROUP), logits, NEG)
    v1 = jnp.max(ein, axis=1, keepdims=True)
    i1 = jnp.min(jnp.where(ein == v1, lane, float(LANES)), axis=1, keepdims=True)
    ein2 = jnp.where(lane == i1, NEG, ein)
    v2 = jnp.max(ein2, axis=1, keepdims=True)
    i2 = jnp.min(jnp.where(ein2 == v2, lane, float(LANES)), axis=1, keepdims=True)
    e21 = jnp.exp(v2 - v1)
    g1 = p_grp / (1.0 + e21)
    g2 = p_grp * e21 / (1.0 + e21)
    oh1 = jnp.where(lane == i1 - N_GROUPS, 1.0, 0.0)
    oh2 = jnp.where(lane == i2 - N_GROUPS, 1.0, 0.0)
    both = oh1 + oh2
    before = _dot(lst_ref[...], both.astype(BF16)) + cnt_ref[...]
    rk1 = jnp.sum(oh1 * before, axis=1, keepdims=True)
    rk2 = jnp.sum(oh2 * before, axis=1, keepdims=True)
    cnt_ref[...] += jnp.sum(both, axis=0, keepdims=True)
    vals = (i1 - N_GROUPS, i2 - N_GROUPS, g1, g2, rk1, rk2)
    route = jnp.zeros_like(logits)
    for c, val in enumerate(vals):
        route = jnp.where(lane == c, val, route)
    route_ref[...] = route


def _proj_out(ym, yr, yn, x2, wo, g, rw, rb):
    n = x2.shape[0]
    tm = PROJ_TM

    def rows(w):
        return pl.BlockSpec((tm, w), lambda i: (i, 0))

    def full(shape):
        return pl.BlockSpec(shape, lambda i: (0, 0))

    ii = np.arange(tm)
    lstrict = jnp.asarray(ii[None, :] < ii[:, None], BF16)
    return pl.pallas_call(
        _proj_out_kernel,
        out_shape=(jax.ShapeDtypeStruct((n, D_MODEL), F32),
                   jax.ShapeDtypeStruct((n, D_MODEL), F32),
                   jax.ShapeDtypeStruct((n, LANES), F32),
                   jax.ShapeDtypeStruct((1, LANES), F32)),
        grid=(n // tm,),
        in_specs=[rows(W_MLSTM), rows(W_RET), rows(W_NA), rows(D_MODEL),
                  full((D_MODEL, D_MODEL)), full((1, D_MODEL)),
                  pl.BlockSpec((2, D_MODEL, LANES), lambda i: (0, 0, 0)),
                  full((1, LANES)), full((tm, tm))],
        out_specs=(rows(D_MODEL), rows(D_MODEL), rows(LANES), full((1, LANES))),
        compiler_params=_cparams(("arbitrary",)),
        name="proj_out_router",
    )(ym, yr, yn, x2, wo, g, rw, rb, lstrict)


def _dispatch_kernel(pend_ref, cnt_ref, nused_ref, meta_hbm, xn_ref, xs_hbm,
                     meta0, meta1, zero_buf, sem_meta, sem_rows, sem_zero):
    i = pl.program_id(0)
    nsteps = pl.num_programs(0)
    slot = lax.rem(i, 2)
    n_blocks = xs_hbm.shape[0] // MOE_BLK

    metas = (meta0, meta1)

    def meta_copy(step, s):
        return pltpu.make_async_copy(meta_hbm.at[step], metas[s], sem_meta.at[s])

    def zero_copy(blk):
        start = pl.multiple_of(blk * MOE_BLK, MOE_BLK)
        return pltpu.make_async_copy(zero_buf, xs_hbm.at[pl.ds(start, MOE_BLK), :], sem_zero)

    @pl.when(i == 0)
    def _():
        meta_copy(0, 0).start()
        zero_buf[...] = jnp.zeros_like(zero_buf)

        def zstart(e, carry):
            @pl.when(cnt_ref[e] > 0)
            def _():
                zero_copy(pend_ref[e] // MOE_BLK - 1).start()
            return carry

        def zwait(e, carry):
            @pl.when(cnt_ref[e] > 0)
            def _():
                zero_copy(0).wait()
            return carry

        lax.fori_loop(0, N_EXPERTS, zstart, 0)
        lax.fori_loop(nused_ref[0], n_blocks, lambda b, c: (zero_copy(b).start(), c)[1], 0)
        lax.fori_loop(0, N_EXPERTS, zwait, 0)
        lax.fori_loop(nused_ref[0], n_blocks, lambda b, c: (zero_copy(0).wait(), c)[1], 0)

    def row_copy(r, dst):
        return pltpu.make_async_copy(xn_ref.at[pl.ds(r, 1), :], xs_hbm.at[pl.ds(dst, 1), :],
                                     sem_rows)

    for s in (0, 1):
        @pl.when(slot == s)
        def _(s=s):
            meta_copy(i, s).wait()

            @pl.when(i + 1 < nsteps)
            def _():
                meta_copy(i + 1, 1 - s).start()

            def issue(r, carry):
                for kk in range(TOP_K):
                    row_copy(r, metas[s][TOP_K * r + kk]).start()
                return carry

            lax.fori_loop(0, COMB_TM, issue, 0, unroll=8)

    def drain(r, carry):
        for kk in range(TOP_K):
            row_copy(r, 0).wait()
        return carry

    lax.fori_loop(0, COMB_TM, drain, 0, unroll=8)


def _dispatch(pend, counts, n_used, meta, xn, n_slots):
    n = xn.shape[0]
    tm = COMB_TM
    grid_spec = pltpu.PrefetchScalarGridSpec(
        num_scalar_prefetch=3,
        grid=(n // tm,),
        in_specs=[pl.BlockSpec(memory_space=pl.ANY),
                  pl.BlockSpec((tm, D_MODEL), lambda i, *_: (i, 0))],
        out_specs=pl.BlockSpec(memory_space=pl.ANY),
        scratch_shapes=[pltpu.SMEM((TOP_K * tm,), jnp.int32),
                        pltpu.SMEM((TOP_K * tm,), jnp.int32),
                        pltpu.VMEM((MOE_BLK, D_MODEL), F32),
                        pltpu.SemaphoreType.DMA((2,)),
                        pltpu.SemaphoreType.DMA,
                        pltpu.SemaphoreType.DMA],
    )
    return pl.pallas_call(
        _dispatch_kernel,
        out_shape=jax.ShapeDtypeStruct((n_slots, D_MODEL), F32),
        grid_spec=grid_spec,
        compiler_params=_cparams(("arbitrary",)),
        name="dispatch",
    )(pend, counts, n_used, meta, xn)


def _expert_kernel(be_ref, nused_ref, xs_ref, w1_ref, w3_ref, w2_ref, y_ref, wb1, wb3, wb2):
    i = pl.program_id(0)

    @pl.when((i == 0) | (be_ref[i] != be_ref[jnp.maximum(i - 1, 0)]))
    def _():
        wb1[...] = w1_ref[...].astype(BF16)
        wb3[...] = w3_ref[...].astype(BF16)
        wb2[...] = w2_ref[...].astype(BF16)

    @pl.when(i < nused_ref[0])
    def _():
        xb = xs_ref[...].astype(BF16)
        h1 = _dot(xb, wb1[...])
        h3 = _dot(xb, wb3[...])
        hb = (h1 * _sigmoid(h1) * h3).astype(BF16)
        y_ref[...] = _dot(hb, wb2[...])

    @pl.when(i >= nused_ref[0])
    def _():
        y_ref[...] = jnp.zeros_like(y_ref)


def _experts(block_expert, n_used, xs, w1, w3, w2, layer):
    n_blocks = xs.shape[0] // MOE_BLK

    def wspec(shape):
        return pl.BlockSpec(shape, lambda i, be, nu: (layer, be[i], 0, 0))

    grid_spec = pltpu.PrefetchScalarGridSpec(
        num_scalar_prefetch=2,
        grid=(n_blocks,),
        in_specs=[pl.BlockSpec((MOE_BLK, D_MODEL),
                               lambda i, be, nu: (jnp.minimum(i, nu[0] - 1), 0)),
                  wspec((None, None, D_MODEL, EXPERT_FF)), wspec((None, None, D_MODEL, EXPERT_FF)),
                  wspec((None, None, EXPERT_FF, D_MODEL))],
        out_specs=pl.BlockSpec((MOE_BLK, D_MODEL), lambda i, be, nu: (i, 0)),
        scratch_shapes=[pltpu.VMEM((D_MODEL, EXPERT_FF), BF16),
                        pltpu.VMEM((D_MODEL, EXPERT_FF), BF16),
                        pltpu.VMEM((EXPERT_FF, D_MODEL), BF16)],
    )
    return pl.pallas_call(
        _expert_kernel,
        out_shape=jax.ShapeDtypeStruct((n_blocks * MOE_BLK, D_MODEL), F32),
        grid_spec=grid_spec,
        compiler_params=_cparams(("arbitrary",)),
        name="experts",
    )(block_expert, n_used, xs, w1, w3, w2)


def _combine_kernel(meta_hbm, yb_hbm, x_ref, route_ref, g_ref, out_ref,
                    meta0, meta1, ybuf, sem_meta, sem_rows, *, final_norm):
    i = pl.program_id(0)
    nsteps = pl.num_programs(0)
    slot = lax.rem(i, 2)
    metas = (meta0, meta1)

    def meta_copy(step, s):
        return pltpu.make_async_copy(meta_hbm.at[step], metas[s], sem_meta.at[s])

    def row_copy(src, s, kk, r):
        return pltpu.make_async_copy(yb_hbm.at[pl.ds(src, 1), :],
                                     ybuf.at[s, kk, pl.ds(r, 1), :], sem_rows.at[s])

    def issue_tile(s):
        def issue(r, carry):
            for kk in range(TOP_K):
                row_copy(metas[s][TOP_K * r + kk], s, kk, r).start()
            return carry

        lax.fori_loop(0, COMB_TM, issue, 0, unroll=8)

    @pl.when(i == 0)
    def _():
        meta_copy(0, 0).start()
        meta_copy(0, 0).wait()
        issue_tile(0)

        @pl.when(nsteps > 1)
        def _():
            meta_copy(1, 1).start()

    for s in (0, 1):
        @pl.when(slot == s)
        def _(s=s):
            @pl.when(i + 1 < nsteps)
            def _():
                meta_copy(i + 1, 1 - s).wait()
                issue_tile(1 - s)

            @pl.when(i + 2 < nsteps)
            def _():
                meta_copy(i + 2, s).start()

            def drain(r, carry):
                for kk in range(TOP_K):
                    row_copy(0, s, kk, r).wait()
                return carry

            lax.fori_loop(0, COMB_TM, drain, 0, unroll=8)
            route = route_ref[...]
            y = x_ref[...]
            for kk in range(TOP_K):
                y = y + route[:, 2 + kk:3 + kk] * ybuf[s, kk]
            if final_norm:
                ms = jnp.mean(y * y, axis=-1, keepdims=True)
                y = y * lax.rsqrt(ms + RMS_EPS) * g_ref[...]
            out_ref[...] = y


def _combine(meta, yb, x2, route, g, final_norm):
    n = x2.shape[0]
    tm = COMB_TM
    kern = functools.partial(_combine_kernel, final_norm=final_norm)
    return pl.pallas_call(
        kern,
        out_shape=jax.ShapeDtypeStruct((n, D_MODEL), F32),
        grid=(n // tm,),
        in_specs=[pl.BlockSpec(memory_space=pl.ANY),
                  pl.BlockSpec(memory_space=pl.ANY),
                  pl.BlockSpec((tm, D_MODEL), lambda i: (i, 0)),
                  pl.BlockSpec((tm, LANES), lambda i: (i, 0)),
                  pl.BlockSpec((1, D_MODEL), lambda i: (0, 0))],
        out_specs=pl.BlockSpec((tm, D_MODEL), lambda i: (i, 0)),
        scratch_shapes=[pltpu.SMEM((TOP_K * tm,), jnp.int32),
                        pltpu.SMEM((TOP_K * tm,), jnp.int32),
                        pltpu.VMEM((2, TOP_K, tm, D_MODEL), F32),
                        pltpu.SemaphoreType.DMA((2,)),
                        pltpu.SemaphoreType.DMA((2,))],
        compiler_params=_cparams(("arbitrary",)),
        name="combine",
    )(meta, yb, x2, route, g)


def _dispatch_layout(route, counts_f, n):
    counts = counts_f[0, :N_EXPERTS].astype(jnp.int32)
    n_blocks = n * TOP_K // MOE_BLK + N_EXPERTS
    padded = (counts + MOE_BLK - 1) // MOE_BLK * MOE_BLK
    pend = jnp.cumsum(padded).astype(jnp.int32)
    pstart = pend - padded
    n_used = pend[-1] // MOE_BLK
    blk_start = jnp.arange(n_blocks, dtype=jnp.int32) * MOE_BLK
    blk_start = jnp.minimum(blk_start, (n_used - 1) * MOE_BLK)
    block_expert = jnp.sum((pend[None, :] <= blk_start[:, None]).astype(jnp.int32), axis=1)
    block_expert = jnp.clip(block_expert, 0, N_EXPERTS - 1).astype(jnp.int32)
    expert = route[:, 0:TOP_K].astype(jnp.int32)
    rank = route[:, 4:4 + TOP_K].astype(jnp.int32)
    seg = jnp.sum(jnp.where(expert[..., None] == jnp.arange(N_EXPERTS, dtype=jnp.int32),
                            pstart, 0), axis=-1)
    meta = (seg + rank).reshape(n // COMB_TM, TOP_K * COMB_TM)
    return block_expert, n_used.reshape(1), pend, counts, meta, n_blocks * MOE_BLK


def _const_tables():
    lane = np.arange(LANES)
    grp = lane // HEAD_DIM
    xc = np.zeros((2, 24, 8 * LANES), np.float32)
    for kk in range(4):
        xc[:, kk, kk * LANES:(kk + 1) * LANES] = 1.0
    for di in range(2):
        for j in range(2):
            for blk, base in ((4 + di, 8), (6 + di, 16)):
                lo = blk * LANES + j * HEAD_DIM
                xc[:, base + 2 * di + j, lo:lo + HEAD_DIM] = 1.0
    ii = np.arange(CHUNK)
    incl_f = (ii[:, None] <= ii[None, :]).astype(np.float32)
    incl_b = (ii[:, None] >= ii[None, :]).astype(np.float32)
    bd = (grp[:, None] == grp[None, :]).astype(np.float32)
    perm = np.zeros((LANES, LANES), np.float32)
    half = HEAD_DIM // 2
    src = (lane // HEAD_DIM) * HEAD_DIM + (lane % HEAD_DIM + half) % HEAD_DIM
    perm[src, lane] = 1.0
    sgn = np.where(lane % HEAD_DIM < half, -1.0, 1.0).astype(np.float32)[None, :]
    return {
        "xc": jnp.asarray(xc.reshape(48, 8 * LANES), BF16),
        "trif": jnp.asarray(np.concatenate([incl_f, incl_f], 0), BF16),
        "trib": jnp.asarray(np.concatenate([incl_b, incl_b], 0), BF16),
        "bdm": jnp.asarray(bd, F32), "bdm2": jnp.asarray(np.concatenate([bd, bd], 1), F32),
        "bd2": jnp.asarray(np.concatenate([bd, bd], 0), BF16),
        "perm": jnp.asarray(perm, BF16), "sgn": jnp.asarray(sgn, F32),
    }


def _rope_tables(seq):
    half = HEAD_DIM // 2
    nc = seq // CHUNK
    inv_freq = ROPE_BASE ** (-np.arange(half, dtype=np.float64) / half)
    freq = inv_freq[np.arange(LANES) % half]
    ang_a = np.arange(CHUNK, dtype=np.float64)[:, None] * freq[None, :]
    ang_b = (np.arange(nc, dtype=np.float64) * CHUNK)[:, None] * freq[None, :]
    return {"ca": jnp.asarray(np.cos(ang_a), F32), "sa": jnp.asarray(np.sin(ang_a), F32),
            "cb": jnp.asarray(np.cos(ang_b)[:, None, :], F32),
            "sb": jnp.asarray(np.sin(ang_b)[:, None, :], F32)}


def _layer_weights(l, w_in, mlstm_gate_bias, ret_decay, w_out, router_group_w, router_group_b,
                   router_expert_w, router_expert_b):
    sizes = (W_MLSTM,) * 4 + (4 * H_MLSTM,) + (W_RET,) * 4 + (W_NA,) * 3
    offs = np.concatenate([[0], np.cumsum(sizes)])
    col = lambda i: w_in[l][:, int(offs[i]):int(offs[i + 1])]
    mq, mk, mv, mo, mg, rq, rk, rv, rg, nq, nk, nv = [col(i) for i in range(12)]
    scale = HEAD_DIM ** -0.5
    w_all = jnp.concatenate([mq, mk, mv, mo, rq, rk * scale, rv, rg, nq * scale, nk, nv],
                            axis=1).astype(BF16)
    sel = np.zeros((GATE_ROWS, 4 * H_MLSTM), np.float32)
    for kind in range(2):
        for p in range(H_MLSTM // 2):
            for k in range(4):
                sel[(kind * (H_MLSTM // 2) + p) * 8 + k,
                    (2 * kind + k // 2) * H_MLSTM + 2 * p + k % 2] = 1.0
    sel = jnp.asarray(sel)
    wg = jnp.dot(sel, mg.T, precision=lax.Precision.HIGHEST).astype(BF16)
    gbias = jnp.dot(sel, mlstm_gate_bias[l].astype(F32).reshape(-1, 1),
                    precision=lax.Precision.HIGHEST)
    rd = ret_decay[l].astype(F32)
    rd = jnp.repeat(rd.reshape(2, H_RET // 2, 2), HEAD_DIM, axis=2)
    rd = rd.transpose(1, 0, 2)[:, :, None, :]
    wo = w_out[l].astype(BF16)
    rw = jnp.concatenate([router_group_w[l], router_expert_w[l],
                          jnp.zeros((D_MODEL, LANES - N_GROUPS - N_EXPERTS), F32)], axis=1)
    rb = jnp.concatenate([router_group_b[l].astype(F32), router_expert_b[l].astype(F32),
                          jnp.zeros((LANES - N_GROUPS - N_EXPERTS,), F32)])[None, :]
    rw_hi = rw.astype(BF16)
    rw2 = jnp.stack([rw_hi, (rw - rw_hi.astype(F32)).astype(BF16)])
    return w_all, wg, gbias, rd, wo, rw2, rb


def _encoder(x, consts, norm_mix, w_in, mlstm_conv_w, mlstm_conv_b, mlstm_gate_bias, mlstm_norm,
             ret_decay, ret_norm, na_rpb, w_out, norm_ffn, router_group_w, router_group_b,
             router_expert_w, router_expert_b, expert_w1, expert_w3, expert_w2, norm_final):
    b, seq, _ = x.shape
    n = b * seq
    depth = w_in.shape[0]
    rope = _rope_tables(seq)
    x2 = x.reshape(n, D_MODEL).astype(F32)
    for l in range(depth):
        w_all, wg, gbias, rd, wo, rw, rb = _layer_weights(
            l, w_in, mlstm_gate_bias, ret_decay, w_out, router_group_w, router_group_b,
            router_expert_w, router_expert_b)
        z, gates_t = _proj_in(x2, norm_mix[l].astype(F32)[None, :], w_all, wg)
        z3 = z.reshape(b, seq, Z_WIDTH)
        y_m = _mlstm(z3, gates_t, gbias, mlstm_conv_w[l].astype(F32),
                     mlstm_conv_b[l].astype(F32)[None, :], mlstm_norm[l].astype(F32)[None, :], consts)
        y_r = _retention(z3, rd, ret_norm[l].astype(F32)[None, :], consts, rope)
        y_n = _na(z3, _na_bias(na_rpb[l], seq))
        x2, xn, route, counts_f = _proj_out(y_m.reshape(n, W_MLSTM), y_r.reshape(n, W_RET),
                                            y_n.reshape(n, W_NA), x2, wo,
                                            norm_ffn[l].astype(F32)[None, :], rw, rb)
        block_expert, n_used, pend, counts, meta, n_slots = _dispatch_layout(route, counts_f, n)
        xs = _dispatch(pend, counts, n_used, meta, xn, n_slots)
        yb = _experts(block_expert, n_used, xs, expert_w1.astype(F32), expert_w3.astype(F32),
                      expert_w2.astype(F32), l)
        x2 = _combine(meta, yb, x2, route, norm_final.astype(F32)[None, :],
                      final_norm=(l == depth - 1))
    return x2.reshape(b, seq, D_MODEL)


def kernel(x_prompt, x_sample, norm_mix, w_in, mlstm_conv_w, mlstm_conv_b, mlstm_gate_bias,
           mlstm_norm, ret_decay, ret_norm, na_rpb, w_out, norm_ffn, router_group_w,
           router_group_b, router_expert_w, router_expert_b, expert_w1, expert_w3, expert_w2,
           norm_final):
    consts = _const_tables()
    weights = (norm_mix, w_in, mlstm_conv_w, mlstm_conv_b, mlstm_gate_bias, mlstm_norm,
               ret_decay, ret_norm, na_rpb, w_out, norm_ffn, router_group_w, router_group_b,
               router_expert_w, router_expert_b, expert_w1, expert_w3, expert_w2, norm_final)
    return (_encoder(x_prompt, consts, *weights), _encoder(x_sample, consts, *weights))
```

```python
import functools

import numpy as np
import jax
import jax.numpy as jnp
from jax import lax
from jax.experimental import pallas as pl
from jax.experimental.pallas import tpu as pltpu

F32 = jnp.float32
BF16 = jnp.bfloat16

D_MODEL = 1024
HEAD_DIM = 64
LANES = 128
CHUNK = 128
H_MLSTM = 6
H_RET = 6
H_NA = 4
W_MLSTM = H_MLSTM * HEAD_DIM
W_RET = H_RET * HEAD_DIM
W_NA = H_NA * HEAD_DIM
Z_WIDTH = 4 * W_MLSTM + 4 * W_RET + 3 * W_NA
GATE_ROWS = 2 * (H_MLSTM // 2) * 8
ROPE_BASE = 10000.0
GRID_W = 64
NA_ROWS = 8
NA_COLS = 16
NA_QROWS = 4
NA_KROWS = NA_QROWS + NA_ROWS
N_GROUPS = 4
EXPERTS_PER_GROUP = 8
N_EXPERTS = 32
TOP_K = 2
EXPERT_FF = 512
RMS_EPS = 1e-6
NEG = -1e30
VMEM_LIMIT = 56 * 1024 * 1024

PROJ_TM = 1024
MOE_BLK = 256
COMB_TM = 256
MIX_UNROLL = 2
RET_UNROLL = 8
GATE_GROUP = 8

ZB_MQ, ZB_MK, ZB_MV, ZB_MO = 0, 3, 6, 9
ZB_RQ, ZB_RK, ZB_RV, ZB_RG = 12, 15, 18, 21
ZB_NQ, ZB_NK, ZB_NV = 24, 26, 28


def _dot(a, b):
    return jnp.dot(a, b, preferred_element_type=F32)


def _dot_nt(a, b):
    return lax.dot_general(a, b, (((1,), (1,)), ((), ())), preferred_element_type=F32)


def _dot_tn(a, b):
    return lax.dot_general(a, b, (((0,), (0,)), ((), ())), preferred_element_type=F32)


def _split(x, axis):
    hi = x.astype(BF16).astype(F32)
    return jnp.concatenate([hi, x - hi], axis=axis).astype(BF16)


def _sigmoid(x):
    return 1.0 / (1.0 + jnp.exp(-x))


def _log_sigmoid(x):
    return -(jnp.maximum(-x, 0.0) + jnp.log(1.0 + jnp.exp(-jnp.abs(x))))


def _cparams(sem):
    return pltpu.CompilerParams(dimension_semantics=sem, vmem_limit_bytes=VMEM_LIMIT)


def _proj_in_kernel(x_ref, g_ref, w_ref, wg_ref, z_ref, gate_ref):
    x = x_ref[...]
    ms = jnp.mean(x * x, axis=-1, keepdims=True)
    hx = (x * lax.rsqrt(ms + RMS_EPS) * g_ref[...]).astype(BF16)
    cw = 768
    for j in range(0, Z_WIDTH, cw):
        z_ref[:, j:j + cw] = _dot(hx, w_ref[:, j:j + cw]).astype(BF16)
    gate_ref[...] = _dot_nt(wg_ref[...], hx)


def _proj_in(x2, g, w, wg):
    n = x2.shape[0]
    return pl.pallas_call(
        _proj_in_kernel,
        out_shape=(jax.ShapeDtypeStruct((n, Z_WIDTH), BF16),
                   jax.ShapeDtypeStruct((GATE_ROWS, n), F32)),
        grid=(n // PROJ_TM,),
        in_specs=[pl.BlockSpec((PROJ_TM, D_MODEL), lambda i: (i, 0)),
                  pl.BlockSpec((1, D_MODEL), lambda i: (0, 0)),
                  pl.BlockSpec((D_MODEL, Z_WIDTH), lambda i: (0, 0)),
                  pl.BlockSpec((GATE_ROWS, D_MODEL), lambda i: (0, 0))],
        out_specs=(pl.BlockSpec((PROJ_TM, Z_WIDTH), lambda i: (i, 0)),
                   pl.BlockSpec((GATE_ROWS, PROJ_TM), lambda i: (0, i))),
        compiler_params=_cparams(("arbitrary",)),
        name="proj_in",
    )(x2, g, w, wg)


def _lane_lt64():
    return lax.broadcasted_iota(jnp.int32, (CHUNK, LANES), 1) < HEAD_DIM


def _tri_mask(d):
    r = lax.broadcasted_iota(jnp.int32, (CHUNK, CHUNK), 0)
    c = lax.broadcasted_iota(jnp.int32, (CHUNK, CHUNK), 1)
    return (c <= r) if d == 0 else (c >= r)


def _head_norm(h, bd2_ref):
    ms = _dot(_split(h * h, 1), bd2_ref[...]) * (1.0 / HEAD_DIM)
    return h * lax.rsqrt(ms + RMS_EPS)


def _mlstm_kernel(q_ref, k_ref, v_ref, o_ref, li_ref, lf_ref, bli_ref, blf_ref, cwq_ref, cwk_ref,
                  cbq_ref, cbk_ref, nw_ref, trif_ref, trib_ref, xc_ref, bdm_ref, bd2_ref,
                  out_ref, qc_ref, kc_ref, kvf_ref, kvb_ref, st_ref, rr_ref, aa_ref, cc_ref,
                  atot_ref, mloc_ref, mpf_ref, mpb_ref, ee_ref, *, seq):
    nc = seq // CHUNK
    lane_lo = _lane_lt64()
    row = lax.broadcasted_iota(jnp.int32, (CHUNK, LANES), 0)
    grow = lax.broadcasted_iota(jnp.int32, (8, CHUNK), 0)
    glane = lax.broadcasted_iota(jnp.int32, (8, CHUNK), 1)
    is_fwd = grow < 2
    ones_aug = jnp.ones((CHUNK, LANES), BF16)
    tri_f = _tri_mask(0)
    tri_b = _tri_mask(1)

    def conv_chunk(src_ref, w_ref, b_ref, c, scale):
        t0 = pl.multiple_of(c * CHUNK, CHUNK)
        x = src_ref[pl.ds(t0, CHUNK), :].astype(F32)
        tp = pl.multiple_of(jnp.maximum(t0 - 16, 0), 16)
        tn = pl.multiple_of(jnp.minimum(t0 + CHUNK, seq - 16), 16)
        prev_last = src_ref[pl.ds(tp, 16), :].astype(F32)[15:16, :]
        next_first = src_ref[pl.ds(tn, 16), :].astype(F32)[0:1, :]
        prev_last = prev_last * jnp.where(c > 0, 1.0, 0.0)
        next_first = next_first * jnp.where(c < nc - 1, 1.0, 0.0)
        xm1 = jnp.where(row == 0, prev_last, pltpu.roll(x, 1, 0))
        xp1 = jnp.where(row == CHUNK - 1, next_first, pltpu.roll(x, CHUNK - 1, 0))
        y = w_ref[0:1, :] * xm1 + w_ref[1:2, :] * x + w_ref[2:3, :] * xp1 + b_ref[...]
        return y * _sigmoid(y) * scale

    def v_aug_at(t0):
        return jnp.concatenate([v_ref[pl.ds(t0, CHUNK), :], ones_aug], axis=1)

    rows_g = GATE_GROUP * 8
    grow_g = lax.broadcasted_iota(jnp.int32, (rows_g, CHUNK), 0)
    glane_g = lax.broadcasted_iota(jnp.int32, (rows_g, CHUNK), 1)
    is_fwd_g = lax.rem(grow_g, 8) < 2

    def stack_chunks(w):
        return jnp.concatenate([w[:, i * CHUNK:(i + 1) * CHUNK] for i in range(GATE_GROUP)],
                               axis=0)

    def gate_body(gi, carry):
        c0 = pl.multiple_of(gi * GATE_GROUP, GATE_GROUP)
        t0 = pl.multiple_of(gi * (GATE_GROUP * CHUNK), GATE_GROUP * CHUNK)
        li = stack_chunks(li_ref[:, pl.ds(t0, GATE_GROUP * CHUNK)] + bli_ref[...])
        lf = stack_chunks(_log_sigmoid(lf_ref[:, pl.ds(t0, GATE_GROUP * CHUNK)] + blf_ref[...]))
        lfs = _split(lf, 1)
        a = jnp.where(is_fwd_g, _dot(lfs, trif_ref[...]), _dot(lfs, trib_ref[...]))
        r = a - li
        xf = -r
        xb = -r
        s = 1
        while s < CHUNK:
            xf = jnp.maximum(xf, jnp.where(glane_g >= s, pltpu.roll(xf, s, 1), NEG))
            xb = jnp.maximum(xb, jnp.where(glane_g < CHUNK - s, pltpu.roll(xb, CHUNK - s, 1), NEG))
            s *= 2
        a_tot = jnp.where(is_fwd_g, a[:, CHUNK - 1:CHUNK], a[:, 0:1])
        w_loc = a_tot - r
        m_loc = jnp.max(w_loc, axis=1, keepdims=True)
        tiles = ((rr_ref, r), (aa_ref, a), (cc_ref, jnp.where(is_fwd_g, xf, xb)),
                 (atot_ref, a_tot), (mloc_ref, jnp.broadcast_to(m_loc, (rows_g, CHUNK))),
                 (ee_ref, jnp.exp(w_loc - m_loc)))
        for ref, val in tiles:
            ref[pl.ds(c0, GATE_GROUP)] = val.reshape(GATE_GROUP, 8, CHUNK)
        return carry

    lax.fori_loop(0, nc // GATE_GROUP, gate_body, 0, unroll=2)

    def pre_body(c, carry):
        t0 = pl.multiple_of(c * CHUNK, CHUNK)
        qc_ref[pl.ds(t0, CHUNK), :] = conv_chunk(q_ref, cwq_ref, cbq_ref, c, 1.0).astype(BF16)
        kc = conv_chunk(k_ref, cwk_ref, cbk_ref, c, HEAD_DIM ** -0.5).astype(BF16)
        kc_ref[pl.ds(t0, CHUNK), :] = kc
        k_t = kc.astype(F32).T
        e = ee_ref[c]
        v_aug = v_aug_at(t0)
        for di, kv_ref in enumerate((kvf_ref, kvb_ref)):
            e_rows = jnp.where(row < HEAD_DIM, e[2 * di:2 * di + 1, :], e[2 * di + 1:2 * di + 2, :])
            kv = _dot((k_t * e_rows).astype(BF16), v_aug) * bdm_ref[...]
            kv_ref[c] = kv.astype(BF16)
        return carry

    lax.fori_loop(0, nc, pre_body, 0, unroll=4)

    st_ref[...] = jnp.zeros_like(st_ref)

    def lane_pair(t, k0):
        v = jnp.where(lane_lo[0:1, :], t[k0:k0 + 1, :], t[k0 + 1:k0 + 2, :])
        return jnp.concatenate([v, v], axis=1)

    def scan_body(t, m):
        cf = t
        cb = nc - 1 - t
        atot = jnp.where(is_fwd, atot_ref[cf], atot_ref[cb])
        mloc = jnp.where(is_fwd, mloc_ref[cf], mloc_ref[cb])
        mpf_ref[cf] = m
        mpb_ref[cb] = m
        m_new = jnp.maximum(atot + m, mloc)
        s_old = jnp.exp(atot + m - m_new)
        s_new = jnp.exp(mloc - m_new)
        for di, (kv_ref, cidx) in enumerate(((kvf_ref, cf), (kvb_ref, cb))):
            kv = kv_ref[cidx].astype(F32)
            s_st = st_ref[di]
            kv_ref[cidx] = s_st.astype(BF16)
            st_ref[di] = lane_pair(s_old, 2 * di) * s_st + lane_pair(s_new, 2 * di) * kv
        return m_new

    lax.fori_loop(0, nc, scan_body, jnp.zeros((8, CHUNK), F32), unroll=MIX_UNROLL)

    def out_body(c, carry):
        t0 = pl.multiple_of(c * CHUNK, CHUNK)
        q = qc_ref[pl.ds(t0, CHUNK), :]
        k = kc_ref[pl.ds(t0, CHUNK), :]
        v_aug = v_aug_at(t0)
        mp = jnp.where(is_fwd, mpf_ref[c], mpb_ref[c])
        u = -jnp.maximum(cc_ref[c], mp)
        cols = jnp.concatenate([u, jnp.exp(mp + u), jnp.exp(u - aa_ref[c])], axis=0)
        bc = _dot_tn(_split(cols, 0), xc_ref[...])
        r = rr_ref[c]
        scores = []
        for j in (0, 1):
            qj = jnp.where(lane_lo if j == 0 else jnp.logical_not(lane_lo), q, jnp.zeros_like(q))
            scores.append(_dot_nt(qj, k))
        h = None
        for di, (kv_ref, tri) in enumerate(((kvf_ref, tri_f), (kvb_ref, tri_b))):
            q_s = _dot(q, kv_ref[c])
            pvs = []
            for j in (0, 1):
                kk = 2 * di + j
                arg = jnp.where(tri, bc[:, kk * LANES:(kk + 1) * LANES] - r[kk:kk + 1, :], NEG)
                pvs.append(_dot((scores[j] * jnp.exp(arg)).astype(BF16), v_aug))
            sint = bc[:, (4 + di) * LANES:(5 + di) * LANES]
            em = bc[:, (6 + di) * LANES:(7 + di) * LANES]
            num = jnp.where(lane_lo, pvs[0][:, :LANES], pvs[1][:, :LANES]) + sint * q_s[:, :LANES]
            den = jnp.where(lane_lo, pvs[0][:, LANES:], pvs[1][:, LANES:]) + sint * q_s[:, LANES:]
            hd = num / jnp.maximum(jnp.abs(den), em)
            h = hd if h is None else h + hd
        y = _head_norm(h, bd2_ref) * nw_ref[...]
        y = y * _sigmoid(o_ref[pl.ds(t0, CHUNK), :].astype(F32))
        out_ref[pl.ds(t0, CHUNK), :] = y.astype(BF16)
        return carry

    lax.fori_loop(0, nc, out_body, 0, unroll=4)


def _mlstm(z3, gates_t, gbias, conv_w, conv_b, norm_w, consts):
    b, seq, _ = z3.shape
    nc = seq // CHUNK
    npair = H_MLSTM // 2

    def zspec(blk0):
        return pl.BlockSpec((None, seq, LANES), lambda bi, p, blk0=blk0: (bi, 0, blk0 + p))

    def full2(shape):
        return pl.BlockSpec(shape, lambda bi, p: (0, 0))

    gate_tile = pltpu.VMEM((nc, 8, CHUNK), F32)
    kern = functools.partial(_mlstm_kernel, seq=seq)
    return pl.pallas_call(
        kern,
        out_shape=jax.ShapeDtypeStruct((b, seq, W_MLSTM), BF16),
        grid=(b, npair),
        in_specs=[zspec(ZB_MQ), zspec(ZB_MK), zspec(ZB_MV), zspec(ZB_MO),
                  pl.BlockSpec((8, seq), lambda bi, p: (p, bi)),
                  pl.BlockSpec((8, seq), lambda bi, p: (npair + p, bi)),
                  pl.BlockSpec((8, 1), lambda bi, p: (p, 0)),
                  pl.BlockSpec((8, 1), lambda bi, p: (npair + p, 0)),
                  pl.BlockSpec((3, LANES), lambda bi, p: (0, p)),
                  pl.BlockSpec((3, LANES), lambda bi, p: (0, npair + p)),
                  pl.BlockSpec((1, LANES), lambda bi, p: (0, p)),
                  pl.BlockSpec((1, LANES), lambda bi, p: (0, npair + p)),
                  pl.BlockSpec((1, LANES), lambda bi, p: (0, p)),
                  full2((2 * CHUNK, CHUNK)), full2((2 * CHUNK, CHUNK)),
                  full2((48, 8 * LANES)),
                  full2((LANES, 2 * LANES)), full2((2 * LANES, LANES))],
        out_specs=pl.BlockSpec((None, seq, LANES), lambda bi, p: (bi, 0, p)),
        scratch_shapes=[pltpu.VMEM((seq, LANES), BF16),
                        pltpu.VMEM((seq, LANES), BF16),
                        pltpu.VMEM((nc, CHUNK, 2 * LANES), BF16),
                        pltpu.VMEM((nc, CHUNK, 2 * LANES), BF16),
                        pltpu.VMEM((2, CHUNK, 2 * LANES), F32)] + [gate_tile] * 8,
        compiler_params=_cparams(("arbitrary", "arbitrary")),
        name="mlstm",
    )(z3, z3, z3, z3, gates_t, gates_t, gbias, gbias, conv_w, conv_w, conv_b, conv_b, norm_w,
      consts["trif"], consts["trib"], consts["xc"], consts["bdm2"], consts["bd2"])


def _ret_kernel(q_ref, k_ref, v_ref, gt_ref, rd_ref, nw_ref, ca_ref, sa_ref, cb_ref, sb_ref,
                sgn_ref, perm_ref, bdm_ref, bd2_ref, out_ref,
                qr_ref, kr_ref, kvf_ref, kvb_ref, st_ref, tab_ref, *, seq):
    nc = seq // CHUNK
    lane_lo = _lane_lt64()
    rowf = lax.broadcasted_iota(jnp.int32, (CHUNK, LANES), 0).astype(F32)
    ri = lax.broadcasted_iota(jnp.int32, (CHUNK, CHUNK), 0)
    ci = lax.broadcasted_iota(jnp.int32, (CHUNK, CHUNK), 1)
    dist = (ri - ci).astype(F32)
    lg_f = -jnp.exp(rd_ref[0])
    lg_b = -jnp.exp(rd_ref[1])
    tab_ref[0] = jnp.exp(lg_f * (CHUNK - 1.0 - rowf))
    tab_ref[1] = jnp.exp(lg_b * rowf)
    tab_ref[2] = jnp.exp(lg_f * (rowf + 1.0))
    tab_ref[3] = jnp.exp(lg_b * (CHUNK - rowf))
    for j in (0, 1):
        lgf_j = lg_f[:, HEAD_DIM * j:HEAD_DIM * j + 1]
        lgb_j = lg_b[:, HEAD_DIM * j:HEAD_DIM * j + 1]
        tab_ref[4 + j] = (jnp.where(dist >= 0.0, jnp.exp(lgf_j * jnp.maximum(dist, 0.0)), 0.0)
                          + jnp.where(dist <= 0.0, jnp.exp(lgb_j * jnp.maximum(-dist, 0.0)), 0.0))
    cdec_f = jnp.exp(lg_f * float(CHUNK))
    cdec_b = jnp.exp(lg_b * float(CHUNK))

    def pre_body(c, carry):
        t0 = pl.multiple_of(c * CHUNK, CHUNK)
        cb = cb_ref[c]
        sb = sb_ref[c]
        cos = ca_ref[...] * cb - sa_ref[...] * sb
        sin = (sa_ref[...] * cb + ca_ref[...] * sb) * sgn_ref[...]
        rot = []
        for src, dst in ((q_ref, qr_ref), (k_ref, kr_ref)):
            x = src[pl.ds(t0, CHUNK), :]
            xs = _dot(x, perm_ref[...])
            xr = (x.astype(F32) * cos + xs * sin).astype(BF16)
            dst[pl.ds(t0, CHUNK), :] = xr
            rot.append(xr)
        kf = rot[1].astype(F32)
        v = v_ref[pl.ds(t0, CHUNK), :]
        kvf_ref[c] = (_dot_tn((kf * tab_ref[0]).astype(BF16), v) * bdm_ref[...]).astype(BF16)
        kvb_ref[c] = (_dot_tn((kf * tab_ref[1]).astype(BF16), v) * bdm_ref[...]).astype(BF16)
        return carry

    lax.fori_loop(0, nc, pre_body, 0, unroll=RET_UNROLL)

    st_ref[...] = jnp.zeros_like(st_ref)

    def scan_body(t, carry):
        for di, (kv_ref, cidx, cdec) in enumerate(((kvf_ref, t, cdec_f),
                                                   (kvb_ref, nc - 1 - t, cdec_b))):
            kv = kv_ref[cidx].astype(F32)
            s_st = st_ref[di]
            kv_ref[cidx] = s_st.astype(BF16)
            st_ref[di] = cdec * s_st + kv
        return carry

    lax.fori_loop(0, nc, scan_body, 0, unroll=RET_UNROLL)

    def out_body(c, carry):
        t0 = pl.multiple_of(c * CHUNK, CHUNK)
        q = qr_ref[pl.ds(t0, CHUNK), :]
        k = kr_ref[pl.ds(t0, CHUNK), :]
        v = v_ref[pl.ds(t0, CHUNK), :]
        qf = q.astype(F32)
        q_dec = jnp.concatenate([(qf * tab_ref[2]).astype(BF16), (qf * tab_ref[3]).astype(BF16)],
                                axis=1)
        states = jnp.concatenate([kvf_ref[c], kvb_ref[c]], axis=0)
        ys = []
        for j in (0, 1):
            qj = jnp.where(lane_lo if j == 0 else jnp.logical_not(lane_lo), q, jnp.zeros_like(q))
            s = _dot_nt(qj, k) * tab_ref[4 + j]
            ys.append(_dot(s.astype(BF16), v))
        h = jnp.where(lane_lo, ys[0], ys[1]) + _dot(q_dec, states)
        y = _head_norm(h, bd2_ref) * nw_ref[...]
        gt = gt_ref[pl.ds(t0, CHUNK), :].astype(F32)
        y = y * (gt * _sigmoid(gt))
        out_ref[pl.ds(t0, CHUNK), :] = y.astype(BF16)
        return carry

    lax.fori_loop(0, nc, out_body, 0, unroll=RET_UNROLL)


def _retention(z3, rd, norm_w, consts, rope):
    b, seq, _ = z3.shape
    nc = seq // CHUNK
    npair = H_RET // 2

    def zspec(blk0):
        return pl.BlockSpec((None, seq, LANES), lambda bi, p, blk0=blk0: (bi, 0, blk0 + p))

    def full2(shape):
        return pl.BlockSpec(shape, lambda bi, p: (0, 0))

    kern = functools.partial(_ret_kernel, seq=seq)
    return pl.pallas_call(
        kern,
        out_shape=jax.ShapeDtypeStruct((b, seq, W_RET), BF16),
        grid=(b, npair),
        in_specs=[zspec(ZB_RQ), zspec(ZB_RK), zspec(ZB_RV), zspec(ZB_RG),
                  pl.BlockSpec((None, 2, 1, LANES), lambda bi, p: (p, 0, 0, 0)),
                  pl.BlockSpec((1, LANES), lambda bi, p: (0, p)),
                  full2((CHUNK, LANES)), full2((CHUNK, LANES)),
                  pl.BlockSpec((nc, 1, LANES), lambda bi, p: (0, 0, 0)),
                  pl.BlockSpec((nc, 1, LANES), lambda bi, p: (0, 0, 0)),
                  full2((1, LANES)), full2((LANES, LANES)), full2((LANES, LANES)),
                  full2((2 * LANES, LANES))],
        out_specs=pl.BlockSpec((None, seq, LANES), lambda bi, p: (bi, 0, p)),
        scratch_shapes=[pltpu.VMEM((seq, LANES), BF16),
                        pltpu.VMEM((seq, LANES), BF16),
                        pltpu.VMEM((nc, CHUNK, LANES), BF16),
                        pltpu.VMEM((nc, CHUNK, LANES), BF16),
                        pltpu.VMEM((2, CHUNK, LANES), F32),
                        pltpu.VMEM((6, CHUNK, LANES), F32)],
        compiler_params=_cparams(("arbitrary", "arbitrary")),
        name="retention",
    )(z3, z3, z3, z3, rd, norm_w, rope["ca"], rope["sa"], rope["cb"], rope["sb"],
      consts["sgn"], consts["perm"], consts["bdm"], consts["bd2"])


def _na_kernel(q_ref, k_ref, v_ref, bias_ref, out_ref, *, seq):
    rows = seq // GRID_W
    ng = rows // NA_QROWS
    nq = NA_QROWS * GRID_W
    nk = NA_KROWS * GRID_W
    lane_lo = lax.broadcasted_iota(jnp.int32, (nq, LANES), 1) < HEAD_DIM
    ones_aug = jnp.ones((nk, LANES), BF16)

    def body(g, carry):
        t0 = pl.multiple_of(g * nq, nq)
        base = jnp.clip(g * NA_QROWS - NA_ROWS // 2, 0, rows - NA_KROWS)
        k0 = pl.multiple_of(base * GRID_W, GRID_W)
        case = jnp.where(g == 0, 0, jnp.where(g == ng - 1, 2, 1))
        q = q_ref[pl.ds(t0, nq), :]
        kk = k_ref[pl.ds(k0, nk), :]
        v_aug = jnp.concatenate([v_ref[pl.ds(k0, nk), :], ones_aug], axis=1)
        outs = []
        for j in (0, 1):
            qj = jnp.where(lane_lo if j == 0 else jnp.logical_not(lane_lo), q,
                           jnp.zeros_like(q))
            s = _dot_nt(qj, kk) + bias_ref[case, j]
            m = jnp.max(s, axis=1, keepdims=True)
            e = jnp.exp(s - m)
            pv = _dot(e.astype(BF16), v_aug)
            outs.append(pv[:, :LANES] / pv[:, LANES:])
        out_ref[pl.ds(t0, nq), :] = jnp.where(lane_lo, outs[0], outs[1]).astype(BF16)
        return carry

    lax.fori_loop(0, ng, body, 0, unroll=2)


def _na(z3, bias):
    b, seq, _ = z3.shape
    npair = H_NA // 2
    nq = NA_QROWS * GRID_W
    nk = NA_KROWS * GRID_W

    def zspec(blk0):
        return pl.BlockSpec((None, seq, LANES), lambda bi, p, blk0=blk0: (bi, 0, blk0 + p))

    kern = functools.partial(_na_kernel, seq=seq)
    return pl.pallas_call(
        kern,
        out_shape=jax.ShapeDtypeStruct((b, seq, W_NA), BF16),
        grid=(b, npair),
        in_specs=[zspec(ZB_NQ), zspec(ZB_NK), zspec(ZB_NV),
                  pl.BlockSpec((None, 3, 2, nq, nk), lambda bi, p: (p, 0, 0, 0, 0))],
        out_specs=pl.BlockSpec((None, seq, LANES), lambda bi, p: (bi, 0, p)),
        compiler_params=_cparams(("arbitrary", "arbitrary")),
        name="natten",
    )(z3, z3, z3, bias)


def _na_index_tables(seq):
    rows = seq // GRID_W
    ng = rows // NA_QROWS
    tabs = []
    for g in (0, 1, ng - 1):
        base = int(np.clip(g * NA_QROWS - NA_ROWS // 2, 0, rows - NA_KROWS))
        qr = g * NA_QROWS + np.arange(NA_QROWS)[:, None, None, None]
        qc = np.arange(GRID_W)[None, :, None, None]
        kr = base + np.arange(NA_KROWS)[None, None, :, None]
        kc = np.arange(GRID_W)[None, None, None, :]
        rstart = np.clip(qr - NA_ROWS // 2, 0, rows - NA_ROWS)
        cstart = np.clip(qc - NA_COLS // 2, 0, GRID_W - NA_COLS)
        ok = (kr >= rstart) & (kr < rstart + NA_ROWS) & (kc >= cstart) & (kc < cstart + NA_COLS)
        rel_r = np.clip(kr - qr + NA_ROWS - 1, 0, 2 * NA_ROWS - 2)
        rel_c = np.clip(kc - qc + NA_COLS - 1, 0, 2 * NA_COLS - 2)
        shp = (NA_QROWS * GRID_W, NA_KROWS * GRID_W)
        full = (NA_QROWS, GRID_W, NA_KROWS, GRID_W)
        tabs.append((np.broadcast_to(ok, full).reshape(shp),
                     np.broadcast_to(rel_r, full).reshape(shp),
                     np.broadcast_to(rel_c, full).reshape(shp)))
    ok = np.stack([t[0] for t in tabs])
    rr = np.stack([t[1] for t in tabs])
    rc = np.stack([t[2] for t in tabs])
    return ok, rr, rc


def _na_bias(rpb, seq):
    ok, rr, rc = _na_index_tables(seq)
    h = rpb.shape[0]
    n_r, n_c = 2 * NA_ROWS - 1, 2 * NA_COLS - 1
    full = (3, NA_QROWS, GRID_W, NA_KROWS, GRID_W)
    rr5, rc5 = rr.reshape(full), rc.reshape(full)
    oh_r = (rr5[:, :, 0, :, 0][..., None] == np.arange(n_r)).astype(np.float32)
    oh_c = (rc5[0, 0, :, 0, :][..., None] == np.arange(n_c)).astype(np.float32)
    cols = jnp.einsum("hab,qkb->haqk", rpb.astype(F32), jnp.asarray(oh_c),
                      precision=lax.Precision.HIGHEST)
    bias = jnp.einsum("ciea,haqk->hciqek", jnp.asarray(oh_r), cols,
                      precision=lax.Precision.HIGHEST)
    bias = bias.reshape(h, 3, NA_QROWS * GRID_W, NA_KROWS * GRID_W)
    bias = jnp.where(ok[None], bias, NEG)
    return bias.reshape(h // 2, 2, 3, bias.shape[2], bias.shape[3]).transpose(0, 2, 1, 3, 4)


def _proj_out_kernel(ym_ref, yr_ref, yn_ref, x_ref, wo_ref, g_ref, rw_ref,
                     rb_ref, lst_ref, x2_ref, xn_ref, route_ref, cnt_ref):
    @pl.when(pl.program_id(0) == 0)
    def _():
        cnt_ref[...] = jnp.zeros_like(cnt_ref)

    y_cat = jnp.concatenate([ym_ref[...], yr_ref[...], yn_ref[...]], axis=1)
    x2 = x_ref[...] + _dot(y_cat, wo_ref[...])
    x2_ref[...] = x2
    ms = jnp.mean(x2 * x2, axis=-1, keepdims=True)
    xn = x2 * lax.rsqrt(ms + RMS_EPS) * g_ref[...]
    xn_ref[...] = xn
    xh = xn.astype(BF16)
    xl = (xn - xh.astype(F32)).astype(BF16)
    logits = (_dot(xh, rw_ref[0]) + _dot(xl, rw_ref[0]) + _dot(xh, rw_ref[1])) + rb_ref[...]
    tm = logits.shape[0]
    lane = lax.broadcasted_iota(jnp.int32, (tm, LANES), 1).astype(F32)
    glog = jnp.where(lane < N_GROUPS, logits, NEG)
    gmax = jnp.max(glog, axis=1, keepdims=True)
    grp = jnp.min(jnp.where(glog == gmax, lane, float(LANES)), axis=1, keepdims=True)
    p_grp = 1.0 / jnp.sum(jnp.exp(glog - gmax), axis=1, keepdims=True)
    lo = N_GROUPS + grp * EXPERTS_PER_GROUP
    ein = jnp.where((lane >= lo) & (lane < lo + EXPERTS_PER_GROUP), logits, NEG)
    v1 = jnp.max(ein, axis=1, keepdims=True)
    i1 = jnp.min(jnp.where(ein == v1, lane, float(LANES)), axis=1, keepdims=True)
    ein2 = jnp.where(lane == i1, NEG, ein)
    v2 = jnp.max(ein2, axis=1, keepdims=True)
    i2 = jnp.min(jnp.where(ein2 == v2, lane, float(LANES)), axis=1, keepdims=True)
    e21 = jnp.exp(v2 - v1)
    g1 = p_grp / (1.0 + e21)
    g2 = p_grp * e21 / (1.0 + e21)
    oh1 = jnp.where(lane == i1 - N_GROUPS, 1.0, 0.0)
    oh2 = jnp.where(lane == i2 - N_GROUPS, 1.0, 0.0)
    both = oh1 + oh2
    before = _dot(lst_ref[...], both.astype(BF16)) + cnt_ref[...]
    rk1 = jnp.sum(oh1 * before, axis=1, keepdims=True)
    rk2 = jnp.sum(oh2 * before, axis=1, keepdims=True)
    cnt_ref[...] += jnp.sum(both, axis=0, keepdims=True)
    vals = (i1 - N_GROUPS, i2 - N_GROUPS, g1, g2, rk1, rk2)
    route = jnp.zeros_like(logits)
    for c, val in enumerate(vals):
        route = jnp.where(lane == c, val, route)
    route_ref[...] = route


def _proj_out(ym, yr, yn, x2, wo, g, rw, rb):
    n = x2.shape[0]
    tm = PROJ_TM

    def rows(w):
        return pl.BlockSpec((tm, w), lambda i: (i, 0))

    def full(shape):
        return pl.BlockSpec(shape, lambda i: (0, 0))

    ii = np.arange(tm)
    lstrict = jnp.asarray(ii[None, :] < ii[:, None], BF16)
    return pl.pallas_call(
        _proj_out_kernel,
        out_shape=(jax.ShapeDtypeStruct((n, D_MODEL), F32),
                   jax.ShapeDtypeStruct((n, D_MODEL), F32),
                   jax.ShapeDtypeStruct((n, LANES), F32),
                   jax.ShapeDtypeStruct((1, LANES), F32)),
        grid=(n // tm,),
        in_specs=[rows(W_MLSTM), rows(W_RET), rows(W_NA), rows(D_MODEL),
                  full((D_MODEL, D_MODEL)), full((1, D_MODEL)),
                  pl.BlockSpec((2, D_MODEL, LANES), lambda i: (0, 0, 0)),
                  full((1, LANES)), full((tm, tm))],
        out_specs=(rows(D_MODEL), rows(D_MODEL), rows(LANES), full((1, LANES))),
        compiler_params=_cparams(("arbitrary",)),
        name="proj_out_router",
    )(ym, yr, yn, x2, wo, g, rw, rb, lstrict)


def _dispatch_kernel(pend_ref, cnt_ref, nused_ref, meta_hbm, xn_ref, xs_hbm,
                     meta0, meta1, zero_buf, sem_meta, sem_rows, sem_zero):
    i = pl.program_id(0)
    nsteps = pl.num_programs(0)
    slot = lax.rem(i, 2)
    n_blocks = xs_hbm.shape[0] // MOE_BLK

    metas = (meta0, meta1)

    def meta_copy(step, s):
        return pltpu.make_async_copy(meta_hbm.at[step], metas[s], sem_meta.at[s])

    def zero_copy(blk):
        start = pl.multiple_of(blk * MOE_BLK, MOE_BLK)
        return pltpu.make_async_copy(zero_buf, xs_hbm.at[pl.ds(start, MOE_BLK), :], sem_zero)

    @pl.when(i == 0)
    def _():
        meta_copy(0, 0).start()
        zero_buf[...] = jnp.zeros_like(zero_buf)

        def zstart(e, carry):
            @pl.when(cnt_ref[e] > 0)
            def _():
                zero_copy(pend_ref[e] // MOE_BLK - 1).start()
            return carry

        def zwait(e, carry):
            @pl.when(cnt_ref[e] > 0)
            def _():
                zero_copy(0).wait()
            return carry

        lax.fori_loop(0, N_EXPERTS, zstart, 0)
        lax.fori_loop(nused_ref[0], n_blocks, lambda b, c: (zero_copy(b).start(), c)[1], 0)
        lax.fori_loop(0, N_EXPERTS, zwait, 0)
        lax.fori_loop(nused_ref[0], n_blocks, lambda b, c: (zero_copy(0).wait(), c)[1], 0)

    def row_copy(r, dst):
        return pltpu.make_async_copy(xn_ref.at[pl.ds(r, 1), :], xs_hbm.at[pl.ds(dst, 1), :],
                                     sem_rows)

    for s in (0, 1):
        @pl.when(slot == s)
        def _(s=s):
            meta_copy(i, s).wait()

            @pl.when(i + 1 < nsteps)
            def _():
                meta_copy(i + 1, 1 - s).start()

            def issue(r, carry):
                for kk in range(TOP_K):
                    row_copy(r, metas[s][TOP_K * r + kk]).start()
                return carry

            lax.fori_loop(0, COMB_TM, issue, 0, unroll=8)

    def drain(r, carry):
        for kk in range(TOP_K):
            row_copy(r, 0).wait()
        return carry

    lax.fori_loop(0, COMB_TM, drain, 0, unroll=8)


def _dispatch(pend, counts, n_used, meta, xn, n_slots):
    n = xn.shape[0]
    tm = COMB_TM
    grid_spec = pltpu.PrefetchScalarGridSpec(
        num_scalar_prefetch=3,
        grid=(n // tm,),
        in_specs=[pl.BlockSpec(memory_space=pl.ANY),
                  pl.BlockSpec((tm, D_MODEL), lambda i, *_: (i, 0))],
        out_specs=pl.BlockSpec(memory_space=pl.ANY),
        scratch_shapes=[pltpu.SMEM((TOP_K * tm,), jnp.int32),
                        pltpu.SMEM((TOP_K * tm,), jnp.int32),
                        pltpu.VMEM((MOE_BLK, D_MODEL), F32),
                        pltpu.SemaphoreType.DMA((2,)),
                        pltpu.SemaphoreType.DMA,
                        pltpu.SemaphoreType.DMA],
    )
    return pl.pallas_call(
        _dispatch_kernel,
        out_shape=jax.ShapeDtypeStruct((n_slots, D_MODEL), F32),
        grid_spec=grid_spec,
        compiler_params=_cparams(("arbitrary",)),
        name="dispatch",
    )(pend, counts, n_used, meta, xn)


def _expert_kernel(be_ref, nused_ref, xs_ref, w1_ref, w3_ref, w2_ref, y_ref, wb1, wb3, wb2):
    i = pl.program_id(0)

    @pl.when((i == 0) | (be_ref[i] != be_ref[jnp.maximum(i - 1, 0)]))
    def _():
        wb1[...] = w1_ref[...].astype(BF16)
        wb3[...] = w3_ref[...].astype(BF16)
        wb2[...] = w2_ref[...].astype(BF16)

    @pl.when(i < nused_ref[0])
    def _():
        xb = xs_ref[...].astype(BF16)
        h1 = _dot(xb, wb1[...])
        h3 = _dot(xb, wb3[...])
        hb = (h1 * _sigmoid(h1) * h3).astype(BF16)
        y_ref[...] = _dot(hb, wb2[...])

    @pl.when(i >= nused_ref[0])
    def _():
        y_ref[...] = jnp.zeros_like(y_ref)


def _experts(block_expert, n_used, xs, w1, w3, w2, layer):
    n_blocks = xs.shape[0] // MOE_BLK

    def wspec(shape):
        return pl.BlockSpec(shape, lambda i, be, nu: (layer, be[i], 0, 0))

    grid_spec = pltpu.PrefetchScalarGridSpec(
        num_scalar_prefetch=2,
        grid=(n_blocks,),
        in_specs=[pl.BlockSpec((MOE_BLK, D_MODEL),
                               lambda i, be, nu: (jnp.minimum(i, nu[0] - 1), 0)),
                  wspec((None, None, D_MODEL, EXPERT_FF)), wspec((None, None, D_MODEL, EXPERT_FF)),
                  wspec((None, None, EXPERT_FF, D_MODEL))],
        out_specs=pl.BlockSpec((MOE_BLK, D_MODEL), lambda i, be, nu: (i, 0)),
        scratch_shapes=[pltpu.VMEM((D_MODEL, EXPERT_FF), BF16),
                        pltpu.VMEM((D_MODEL, EXPERT_FF), BF16),
                        pltpu.VMEM((EXPERT_FF, D_MODEL), BF16)],
    )
    return pl.pallas_call(
        _expert_kernel,
        out_shape=jax.ShapeDtypeStruct((n_blocks * MOE_BLK, D_MODEL), F32),
        grid_spec=grid_spec,
        compiler_params=_cparams(("arbitrary",)),
        name="experts",
    )(block_expert, n_used, xs, w1, w3, w2)


def _combine_kernel(meta_hbm, yb_hbm, x_ref, route_ref, g_ref, out_ref,
                    meta0, meta1, ybuf, sem_meta, sem_rows, *, final_norm):
    i = pl.program_id(0)
    nsteps = pl.num_programs(0)
    slot = lax.rem(i, 2)
    metas = (meta0, meta1)

    def meta_copy(step, s):
        return pltpu.make_async_copy(meta_hbm.at[step], metas[s], sem_meta.at[s])

    def row_copy(src, s, kk, r):
        return pltpu.make_async_copy(yb_hbm.at[pl.ds(src, 1), :],
                                     ybuf.at[s, kk, pl.ds(r, 1), :], sem_rows.at[s])

    def issue_tile(s):
        def issue(r, carry):
            for kk in range(TOP_K):
                row_copy(metas[s][TOP_K * r + kk], s, kk, r).start()
            return carry

        lax.fori_loop(0, COMB_TM, issue, 0, unroll=8)

    @pl.when(i == 0)
    def _():
        meta_copy(0, 0).start()
        meta_copy(0, 0).wait()
        issue_tile(0)

        @pl.when(nsteps > 1)
        def _():
            meta_copy(1, 1).start()

    for s in (0, 1):
        @pl.when(slot == s)
        def _(s=s):
            @pl.when(i + 1 < nsteps)
            def _():
                meta_copy(i + 1, 1 - s).wait()
                issue_tile(1 - s)

            @pl.when(i + 2 < nsteps)
            def _():
                meta_copy(i + 2, s).start()

            def drain(r, carry):
                for kk in range(TOP_K):
                    row_copy(0, s, kk, r).wait()
                return carry

            lax.fori_loop(0, COMB_TM, drain, 0, unroll=8)
            route = route_ref[...]
            y = x_ref[...]
            for kk in range(TOP_K):
                y = y + route[:, 2 + kk:3 + kk] * ybuf[s, kk]
            if final_norm:
                ms = jnp.mean(y * y, axis=-1, keepdims=True)
                y = y * lax.rsqrt(ms + RMS_EPS) * g_ref[...]
            out_ref[...] = y


def _combine(meta, yb, x2, route, g, final_norm):
    n = x2.shape[0]
    tm = COMB_TM
    kern = functools.partial(_combine_kernel, final_norm=final_norm)
    return pl.pallas_call(
        kern,
        out_shape=jax.ShapeDtypeStruct((n, D_MODEL), F32),
        grid=(n // tm,),
        in_specs=[pl.BlockSpec(memory_space=pl.ANY),
                  pl.BlockSpec(memory_space=pl.ANY),
                  pl.BlockSpec((tm, D_MODEL), lambda i: (i, 0)),
                  pl.BlockSpec((tm, LANES), lambda i: (i, 0)),
                  pl.BlockSpec((1, D_MODEL), lambda i: (0, 0))],
        out_specs=pl.BlockSpec((tm, D_MODEL), lambda i: (i, 0)),
        scratch_shapes=[pltpu.SMEM((TOP_K * tm,), jnp.int32),
                        pltpu.SMEM((TOP_K * tm,), jnp.int32),
                        pltpu.VMEM((2, TOP_K, tm, D_MODEL), F32),
                        pltpu.SemaphoreType.DMA((2,)),
                        pltpu.SemaphoreType.DMA((2,))],
        compiler_params=_cparams(("arbitrary",)),
        name="combine",
    )(meta, yb, x2, route, g)


def _dispatch_layout(route, counts_f, n):
    counts = counts_f[0, :N_EXPERTS].astype(jnp.int32)
    n_blocks = n * TOP_K // MOE_BLK + N_EXPERTS
    padded = (counts + MOE_BLK - 1) // MOE_BLK * MOE_BLK
    pend = jnp.cumsum(padded).astype(jnp.int32)
    pstart = pend - padded
    n_used = pend[-1] // MOE_BLK
    blk_start = jnp.arange(n_blocks, dtype=jnp.int32) * MOE_BLK
    blk_start = jnp.minimum(blk_start, (n_used - 1) * MOE_BLK)
    block_expert = jnp.sum((pend[None, :] <= blk_start[:, None]).astype(jnp.int32), axis=1)
    block_expert = jnp.clip(block_expert, 0, N_EXPERTS - 1).astype(jnp.int32)
    expert = route[:, 0:TOP_K].astype(jnp.int32)
    rank = route[:, 4:4 + TOP_K].astype(jnp.int32)
    seg = jnp.sum(jnp.where(expert[..., None] == jnp.arange(N_EXPERTS, dtype=jnp.int32),
                            pstart, 0), axis=-1)
    meta = (seg + rank).reshape(n // COMB_TM, TOP_K * COMB_TM)
    return block_expert, n_used.reshape(1), pend, counts, meta, n_blocks * MOE_BLK


def _const_tables():
    lane = np.arange(LANES)
    grp = lane // HEAD_DIM
    xc = np.zeros((2, 24, 8 * LANES), np.float32)
    for kk in range(4):
        xc[:, kk, kk * LANES:(kk + 1) * LANES] = 1.0
    for di in range(2):
        for j in range(2):
            for blk, base in ((4 + di, 8), (6 + di, 16)):
                lo = blk * LANES + j * HEAD_DIM
                xc[:, base + 2 * di + j, lo:lo + HEAD_DIM] = 1.0
    ii = np.arange(CHUNK)
    incl_f = (ii[:, None] <= ii[None, :]).astype(np.float32)
    incl_b = (ii[:, None] >= ii[None, :]).astype(np.float32)
    bd = (grp[:, None] == grp[None, :]).astype(np.float32)
    perm = np.zeros((LANES, LANES), np.float32)
    half = HEAD_DIM // 2
    src = (lane // HEAD_DIM) * HEAD_DIM + (lane % HEAD_DIM + half) % HEAD_DIM
    perm[src, lane] = 1.0
    sgn = np.where(lane % HEAD_DIM < half, -1.0, 1.0).astype(np.float32)[None, :]
    return {
        "xc": jnp.asarray(xc.reshape(48, 8 * LANES), BF16),
        "trif": jnp.asarray(np.concatenate([incl_f, incl_f], 0), BF16),
        "trib": jnp.asarray(np.concatenate([incl_b, incl_b], 0), BF16),
        "bdm": jnp.asarray(bd, F32), "bdm2": jnp.asarray(np.concatenate([bd, bd], 1), F32),
        "bd2": jnp.asarray(np.concatenate([bd, bd], 0), BF16),
        "perm": jnp.asarray(perm, BF16), "sgn": jnp.asarray(sgn, F32),
    }


def _rope_tables(seq):
    half = HEAD_DIM // 2
    nc = seq // CHUNK
    inv_freq = ROPE_BASE ** (-np.arange(half, dtype=np.float64) / half)
    freq = inv_freq[np.arange(LANES) % half]
    ang_a = np.arange(CHUNK, dtype=np.float64)[:, None] * freq[None, :]
    ang_b = (np.arange(nc, dtype=np.float64) * CHUNK)[:, None] * freq[None, :]
    return {"ca": jnp.asarray(np.cos(ang_a), F32), "sa": jnp.asarray(np.sin(ang_a), F32),
            "cb": jnp.asarray(np.cos(ang_b)[:, None, :], F32),
            "sb": jnp.asarray(np.sin(ang_b)[:, None, :], F32)}


def _layer_weights(l, w_in, mlstm_gate_bias, ret_decay, w_out, router_group_w, router_group_b,
                   router_expert_w, router_expert_b):
    sizes = (W_MLSTM,) * 4 + (4 * H_MLSTM,) + (W_RET,) * 4 + (W_NA,) * 3
    offs = np.concatenate([[0], np.cumsum(sizes)])
    col = lambda i: w_in[l][:, int(offs[i]):int(offs[i + 1])]
    mq, mk, mv, mo, mg, rq, rk, rv, rg, nq, nk, nv = [col(i) for i in range(12)]
    scale = HEAD_DIM ** -0.5
    w_all = jnp.concatenate([mq, mk, mv, mo, rq, rk * scale, rv, rg, nq * scale, nk, nv],
                            axis=1).astype(BF16)
    sel = np.zeros((GATE_ROWS, 4 * H_MLSTM), np.float32)
    for kind in range(2):
        for p in range(H_MLSTM // 2):
            for k in range(4):
                sel[(kind * (H_MLSTM // 2) + p) * 8 + k,
                    (2 * kind + k // 2) * H_MLSTM + 2 * p + k % 2] = 1.0
    sel = jnp.asarray(sel)
    wg = jnp.dot(sel, mg.T, precision=lax.Precision.HIGHEST).astype(BF16)
    gbias = jnp.dot(sel, mlstm_gate_bias[l].astype(F32).reshape(-1, 1),
                    precision=lax.Precision.HIGHEST)
    rd = ret_decay[l].astype(F32)
    rd = jnp.repeat(rd.reshape(2, H_RET // 2, 2), HEAD_DIM, axis=2)
    rd = rd.transpose(1, 0, 2)[:, :, None, :]
    wo = w_out[l].astype(BF16)
    rw = jnp.concatenate([router_group_w[l], router_expert_w[l],
                          jnp.zeros((D_MODEL, LANES - N_GROUPS - N_EXPERTS), F32)], axis=1)
    rb = jnp.concatenate([router_group_b[l].astype(F32), router_expert_b[l].astype(F32),
                          jnp.zeros((LANES - N_GROUPS - N_EXPERTS,), F32)])[None, :]
    rw_hi = rw.astype(BF16)
    rw2 = jnp.stack([rw_hi, (rw - rw_hi.astype(F32)).astype(BF16)])
    return w_all, wg, gbias, rd, wo, rw2, rb


def _encoder(x, consts, norm_mix, w_in, mlstm_conv_w, mlstm_conv_b, mlstm_gate_bias, mlstm_norm,
             ret_decay, ret_norm, na_rpb, w_out, norm_ffn, router_group_w, router_group_b,
             router_expert_w, router_expert_b, expert_w1, expert_w3, expert_w2, norm_final):
    b, seq, _ = x.shape
    n = b * seq
    depth = w_in.shape[0]
    rope = _rope_tables(seq)
    x2 = x.reshape(n, D_MODEL).astype(F32)
    for l in range(depth):
        w_all, wg, gbias, rd, wo, rw, rb = _layer_weights(
            l, w_in, mlstm_gate_bias, ret_decay, w_out, router_group_w, router_group_b,
            router_expert_w, router_expert_b)
        z, gates_t = _proj_in(x2, norm_mix[l].astype(F32)[None, :], w_all, wg)
        z3 = z.reshape(b, seq, Z_WIDTH)
        y_m = _mlstm(z3, gates_t, gbias, mlstm_conv_w[l].astype(F32),
                     mlstm_conv_b[l].astype(F32)[None, :], mlstm_norm[l].astype(F32)[None, :], consts)
        y_r = _retention(z3, rd, ret_norm[l].astype(F32)[None, :], consts, rope)
        y_n = _na(z3, _na_bias(na_rpb[l], seq))
        x2, xn, route, counts_f = _proj_out(y_m.reshape(n, W_MLSTM), y_r.reshape(n, W_RET),
                                            y_n.reshape(n, W_NA), x2, wo,
                                            norm_ffn[l].astype(F32)[None, :], rw, rb)
        block_expert, n_used, pend, counts, meta, n_slots = _dispatch_layout(route, counts_f, n)
        xs = _dispatch(pend, counts, n_used, meta, xn, n_slots)
        yb = _experts(block_expert, n_used, xs, expert_w1.astype(F32), expert_w3.astype(F32),
                      expert_w2.astype(F32), l)
        x2 = _combine(meta, yb, x2, route, norm_final.astype(F32)[None, :],
                      final_norm=(l == depth - 1))
    return x2.reshape(b, seq, D_MODEL)


def kernel(x_prompt, x_sample, norm_mix, w_in, mlstm_conv_w, mlstm_conv_b, mlstm_gate_bias,
           mlstm_norm, ret_decay, ret_norm, na_rpb, w_out, norm_ffn, router_group_w,
           router_group_b, router_expert_w, router_expert_b, expert_w1, expert_w3, expert_w2,
           norm_final):
    consts = _const_tables()
    weights = (norm_mix, w_in, mlstm_conv_w, mlstm_conv_b, mlstm_gate_bias, mlstm_norm,
               ret_decay, ret_norm, na_rpb, w_out, norm_ffn, router_group_w, router_group_b,
               router_expert_w, router_expert_b, expert_w1, expert_w3, expert_w2, norm_final)
    return (_encoder(x_prompt, consts, *weights), _encoder(x_sample, consts, *weights))
```

```python
import functools

import numpy as np
import jax
import jax.numpy as jnp
from jax import lax
from jax.experimental import pallas as pl
from jax.experimental.pallas import tpu as pltpu

F32 = jnp.float32
BF16 = jnp.bfloat16

D_MODEL = 1024
HEAD_DIM = 64
LANES = 128
CHUNK = 128
H_MLSTM = 6
H_RET = 6
H_NA = 4
W_MLSTM = H_MLSTM * HEAD_DIM
W_RET = H_RET * HEAD_DIM
W_NA = H_NA * HEAD_DIM
Z_WIDTH = 4 * W_MLSTM + 4 * W_RET + 3 * W_NA
GATE_ROWS = 2 * (H_MLSTM // 2) * 8
ROPE_BASE = 10000.0
GRID_W = 64
NA_ROWS = 8
NA_COLS = 16
NA_QROWS = 4
NA_KROWS = NA_QROWS + NA_ROWS
N_GROUPS = 4
EXPERTS_PER_GROUP = 8
N_EXPERTS = 32
TOP_K = 2
EXPERT_FF = 512
RMS_EPS = 1e-6
NEG = -1e30
VMEM_LIMIT = 56 * 1024 * 1024

PROJ_TM = 1024
MOE_BLK = 512
COMB_TM = 512
MLSTM_GATE_UNROLL = 2
MLSTM_PRE_UNROLL = 4
MLSTM_SCAN_UNROLL = 2
MLSTM_OUT_UNROLL = 8
RET_UNROLL = 16
NA_UNROLL = 4
DMA_ISSUE_UNROLL = 8
GATE_GROUP = 8

ZB_MQ, ZB_MK, ZB_MV, ZB_MO = 0, 3, 6, 9
ZB_RQ, ZB_RK, ZB_RV, ZB_RG = 12, 15, 18, 21
ZB_NQ, ZB_NK, ZB_NV = 24, 26, 28


def _dot(a, b):
    return jnp.dot(a, b, preferred_element_type=F32)


def _dot_nt(a, b):
    return lax.dot_general(a, b, (((1,), (1,)), ((), ())), preferred_element_type=F32)


def _dot_tn(a, b):
    return lax.dot_general(a, b, (((0,), (0,)), ((), ())), preferred_element_type=F32)


def _split(x, axis):
    hi = x.astype(BF16).astype(F32)
    return jnp.concatenate([hi, x - hi], axis=axis).astype(BF16)


def _sigmoid(x):
    return 1.0 / (1.0 + jnp.exp(-x))


def _log_sigmoid(x):
    return -(jnp.maximum(-x, 0.0) + jnp.log(1.0 + jnp.exp(-jnp.abs(x))))


def _cparams(sem):
    return pltpu.CompilerParams(dimension_semantics=sem, vmem_limit_bytes=VMEM_LIMIT)


def _proj_in_kernel(x_ref, g_ref, w_ref, wg_ref, z_ref, gate_ref):
    x = x_ref[...]
    ms = jnp.mean(x * x, axis=-1, keepdims=True)
    hx = (x * lax.rsqrt(ms + RMS_EPS) * g_ref[...]).astype(BF16)
    cw = 768
    for j in range(0, Z_WIDTH, cw):
        z_ref[:, j:j + cw] = _dot(hx, w_ref[:, j:j + cw]).astype(BF16)
    gate_ref[...] = _dot_nt(wg_ref[...], hx)


def _proj_in(x2, g, w, wg):
    n = x2.shape[0]
    return pl.pallas_call(
        _proj_in_kernel,
        out_shape=(jax.ShapeDtypeStruct((n, Z_WIDTH), BF16),
                   jax.ShapeDtypeStruct((GATE_ROWS, n), F32)),
        grid=(n // PROJ_TM,),
        in_specs=[pl.BlockSpec((PROJ_TM, D_MODEL), lambda i: (i, 0)),
                  pl.BlockSpec((1, D_MODEL), lambda i: (0, 0)),
                  pl.BlockSpec((D_MODEL, Z_WIDTH), lambda i: (0, 0)),
                  pl.BlockSpec((GATE_ROWS, D_MODEL), lambda i: (0, 0))],
        out_specs=(pl.BlockSpec((PROJ_TM, Z_WIDTH), lambda i: (i, 0)),
                   pl.BlockSpec((GATE_ROWS, PROJ_TM), lambda i: (0, i))),
        compiler_params=_cparams(("arbitrary",)),
        name="proj_in",
    )(x2, g, w, wg)


def _lane_lt64():
    return lax.broadcasted_iota(jnp.int32, (CHUNK, LANES), 1) < HEAD_DIM


def _tri_mask(d):
    r = lax.broadcasted_iota(jnp.int32, (CHUNK, CHUNK), 0)
    c = lax.broadcasted_iota(jnp.int32, (CHUNK, CHUNK), 1)
    return (c <= r) if d == 0 else (c >= r)


def _head_norm(h, bd2_ref):
    ms = _dot(_split(h * h, 1), bd2_ref[...]) * (1.0 / HEAD_DIM)
    return h * lax.rsqrt(ms + RMS_EPS)


def _mlstm_kernel(q_ref, k_ref, v_ref, o_ref, li_ref, lf_ref, bli_ref, blf_ref, cwq_ref, cwk_ref,
                  cbq_ref, cbk_ref, nw_ref, trif_ref, trib_ref, xc_ref, bdm_ref, bd2_ref,
                  out_ref, qc_ref, kc_ref, kvf_ref, kvb_ref, st_ref, rr_ref, aa_ref, cc_ref,
                  atot_ref, mloc_ref, mpf_ref, mpb_ref, ee_ref, *, seq):
    nc = seq // CHUNK
    lane_lo = _lane_lt64()
    row = lax.broadcasted_iota(jnp.int32, (CHUNK, LANES), 0)
    grow = lax.broadcasted_iota(jnp.int32, (8, CHUNK), 0)
    is_fwd = grow < 2
    ones_aug = jnp.ones((CHUNK, LANES), BF16)
    tri_f = _tri_mask(0)
    tri_b = _tri_mask(1)

    def conv_chunk(src_ref, w_ref, b_ref, c, scale):
        t0 = pl.multiple_of(c * CHUNK, CHUNK)
        x = src_ref[pl.ds(t0, CHUNK), :].astype(F32)
        tp = pl.multiple_of(jnp.maximum(t0 - 16, 0), 16)
        tn = pl.multiple_of(jnp.minimum(t0 + CHUNK, seq - 16), 16)
        prev_last = src_ref[pl.ds(tp, 16), :].astype(F32)[15:16, :]
        next_first = src_ref[pl.ds(tn, 16), :].astype(F32)[0:1, :]
        prev_last = prev_last * jnp.where(c > 0, 1.0, 0.0)
        next_first = next_first * jnp.where(c < nc - 1, 1.0, 0.0)
        xm1 = jnp.where(row == 0, prev_last, pltpu.roll(x, 1, 0))
        xp1 = jnp.where(row == CHUNK - 1, next_first, pltpu.roll(x, CHUNK - 1, 0))
        y = w_ref[0:1, :] * xm1 + w_ref[1:2, :] * x + w_ref[2:3, :] * xp1 + b_ref[...]
        return y * _sigmoid(y) * scale

    def v_aug_at(t0):
        return jnp.concatenate([v_ref[pl.ds(t0, CHUNK), :], ones_aug], axis=1)

    rows_g = GATE_GROUP * 8
    grow_g = lax.broadcasted_iota(jnp.int32, (rows_g, CHUNK), 0)
    glane_g = lax.broadcasted_iota(jnp.int32, (rows_g, CHUNK), 1)
    is_fwd_g = lax.rem(grow_g, 8) < 2

    def stack_chunks(w):
        return jnp.concatenate([w[:, i * CHUNK:(i + 1) * CHUNK] for i in range(GATE_GROUP)],
                               axis=0)

    def gate_body(gi, carry):
        c0 = pl.multiple_of(gi * GATE_GROUP, GATE_GROUP)
        t0 = pl.multiple_of(gi * (GATE_GROUP * CHUNK), GATE_GROUP * CHUNK)
        li = stack_chunks(li_ref[:, pl.ds(t0, GATE_GROUP * CHUNK)] + bli_ref[...])
        lf = stack_chunks(_log_sigmoid(lf_ref[:, pl.ds(t0, GATE_GROUP * CHUNK)] + blf_ref[...]))
        lfs = _split(lf, 1)
        a = jnp.where(is_fwd_g, _dot(lfs, trif_ref[...]), _dot(lfs, trib_ref[...]))
        r = a - li
        xf = -r
        xb = -r
        s = 1
        while s < CHUNK:
            xf = jnp.maximum(xf, jnp.where(glane_g >= s, pltpu.roll(xf, s, 1), NEG))
            xb = jnp.maximum(xb, jnp.where(glane_g < CHUNK - s, pltpu.roll(xb, CHUNK - s, 1), NEG))
            s *= 2
        a_tot = jnp.where(is_fwd_g, a[:, CHUNK - 1:CHUNK], a[:, 0:1])
        w_loc = a_tot - r
        m_loc = jnp.max(w_loc, axis=1, keepdims=True)
        tiles = ((rr_ref, r), (aa_ref, a), (cc_ref, jnp.where(is_fwd_g, xf, xb)),
                 (atot_ref, a_tot), (mloc_ref, jnp.broadcast_to(m_loc, (rows_g, CHUNK))),
                 (ee_ref, jnp.exp(w_loc - m_loc)))
        for ref, val in tiles:
            ref[pl.ds(c0, GATE_GROUP)] = val.reshape(GATE_GROUP, 8, CHUNK)
        return carry

    lax.fori_loop(0, nc // GATE_GROUP, gate_body, 0, unroll=MLSTM_GATE_UNROLL)

    def pre_body(c, carry):
        t0 = pl.multiple_of(c * CHUNK, CHUNK)
        qc_ref[pl.ds(t0, CHUNK), :] = conv_chunk(q_ref, cwq_ref, cbq_ref, c, 1.0).astype(BF16)
        kc = conv_chunk(k_ref, cwk_ref, cbk_ref, c, HEAD_DIM ** -0.5).astype(BF16)
        kc_ref[pl.ds(t0, CHUNK), :] = kc
        k_t = kc.astype(F32).T
        e = ee_ref[c]
        v_aug = v_aug_at(t0)
        for di, kv_ref in enumerate((kvf_ref, kvb_ref)):
            e_rows = jnp.where(row < HEAD_DIM, e[2 * di:2 * di + 1, :], e[2 * di + 1:2 * di + 2, :])
            kv = _dot((k_t * e_rows).astype(BF16), v_aug) * bdm_ref[...]
            kv_ref[c] = kv.astype(BF16)
        return carry

    lax.fori_loop(0, nc, pre_body, 0, unroll=MLSTM_PRE_UNROLL)

    st_ref[...] = jnp.zeros_like(st_ref)

    def lane_pair(t, k0):
        v = jnp.where(lane_lo[0:1, :], t[k0:k0 + 1, :], t[k0 + 1:k0 + 2, :])
        return jnp.concatenate([v, v], axis=1)

    def scan_body(t, m):
        cf = t
        cb = nc - 1 - t
        atot = jnp.where(is_fwd, atot_ref[cf], atot_ref[cb])
        mloc = jnp.where(is_fwd, mloc_ref[cf], mloc_ref[cb])
        mpf_ref[cf] = m
        mpb_ref[cb] = m
        m_new = jnp.maximum(atot + m, mloc)
        s_old = jnp.exp(atot + m - m_new)
        s_new = jnp.exp(mloc - m_new)
        for di, (kv_ref, cidx) in enumerate(((kvf_ref, cf), (kvb_ref, cb))):
            kv = kv_ref[cidx].astype(F32)
            s_st = st_ref[di]
            kv_ref[cidx] = s_st.astype(BF16)
            st_ref[di] = lane_pair(s_old, 2 * di) * s_st + lane_pair(s_new, 2 * di) * kv
        return m_new

    lax.fori_loop(0, nc, scan_body, jnp.zeros((8, CHUNK), F32),
                  unroll=MLSTM_SCAN_UNROLL)

    def out_body(c, carry):
        t0 = pl.multiple_of(c * CHUNK, CHUNK)
        q = qc_ref[pl.ds(t0, CHUNK), :]
        k = kc_ref[pl.ds(t0, CHUNK), :]
        v_aug = v_aug_at(t0)
        mp = jnp.where(is_fwd, mpf_ref[c], mpb_ref[c])
        u = -jnp.maximum(cc_ref[c], mp)
        cols = jnp.concatenate([u, jnp.exp(mp + u), jnp.exp(u - aa_ref[c])], axis=0)
        bc = _dot_tn(_split(cols, 0), xc_ref[...])
        r = rr_ref[c]
        scores = []
        for j in (0, 1):
            qj = jnp.where(lane_lo if j == 0 else jnp.logical_not(lane_lo), q, jnp.zeros_like(q))
            scores.append(_dot_nt(qj, k))
        h = None
        for di, (kv_ref, tri) in enumerate(((kvf_ref, tri_f), (kvb_ref, tri_b))):
            q_s = _dot(q, kv_ref[c])
            pvs = []
            for j in (0, 1):
                kk = 2 * di + j
                arg = jnp.where(tri, bc[:, kk * LANES:(kk + 1) * LANES] - r[kk:kk + 1, :], NEG)
                pvs.append(_dot((scores[j] * jnp.exp(arg)).astype(BF16), v_aug))
            sint = bc[:, (4 + di) * LANES:(5 + di) * LANES]
            em = bc[:, (6 + di) * LANES:(7 + di) * LANES]
            num = jnp.where(lane_lo, pvs[0][:, :LANES], pvs[1][:, :LANES]) + sint * q_s[:, :LANES]
            den = jnp.where(lane_lo, pvs[0][:, LANES:], pvs[1][:, LANES:]) + sint * q_s[:, LANES:]
            hd = num / jnp.maximum(jnp.abs(den), em)
            h = hd if h is None else h + hd
        y = _head_norm(h, bd2_ref) * nw_ref[...]
        y = y * _sigmoid(o_ref[pl.ds(t0, CHUNK), :].astype(F32))
        out_ref[pl.ds(t0, CHUNK), :] = y.astype(BF16)
        return carry

    lax.fori_loop(0, nc, out_body, 0, unroll=MLSTM_OUT_UNROLL)


def _mlstm(z3, gates_t, gbias, conv_w, conv_b, norm_w, consts):
    b, seq, _ = z3.shape
    nc = seq // CHUNK
    npair = H_MLSTM // 2

    def zspec(blk0):
        return pl.BlockSpec((None, seq, LANES), lambda bi, p, blk0=blk0: (bi, 0, blk0 + p))

    def full2(shape):
        return pl.BlockSpec(shape, lambda bi, p: (0, 0))

    gate_tile = pltpu.VMEM((nc, 8, CHUNK), F32)
    kern = functools.partial(_mlstm_kernel, seq=seq)
    return pl.pallas_call(
        kern,
        out_shape=jax.ShapeDtypeStruct((b, seq, W_MLSTM), BF16),
        grid=(b, npair),
        in_specs=[zspec(ZB_MQ), zspec(ZB_MK), zspec(ZB_MV), zspec(ZB_MO),
                  pl.BlockSpec((8, seq), lambda bi, p: (p, bi)),
                  pl.BlockSpec((8, seq), lambda bi, p: (npair + p, bi)),
                  pl.BlockSpec((8, 1), lambda bi, p: (p, 0)),
                  pl.BlockSpec((8, 1), lambda bi, p: (npair + p, 0)),
                  pl.BlockSpec((3, LANES), lambda bi, p: (0, p)),
                  pl.BlockSpec((3, LANES), lambda bi, p: (0, npair + p)),
                  pl.BlockSpec((1, LANES), lambda bi, p: (0, p)),
                  pl.BlockSpec((1, LANES), lambda bi, p: (0, npair + p)),
                  pl.BlockSpec((1, LANES), lambda bi, p: (0, p)),
                  full2((2 * CHUNK, CHUNK)), full2((2 * CHUNK, CHUNK)),
                  full2((48, 8 * LANES)),
                  full2((LANES, 2 * LANES)), full2((2 * LANES, LANES))],
        out_specs=pl.BlockSpec((None, seq, LANES), lambda bi, p: (bi, 0, p)),
        scratch_shapes=[pltpu.VMEM((seq, LANES), BF16),
                        pltpu.VMEM((seq, LANES), BF16),
                        pltpu.VMEM((nc, CHUNK, 2 * LANES), BF16),
                        pltpu.VMEM((nc, CHUNK, 2 * LANES), BF16),
                        pltpu.VMEM((2, CHUNK, 2 * LANES), F32)] + [gate_tile] * 8,
        compiler_params=_cparams(("arbitrary", "arbitrary")),
        name="mlstm",
    )(z3, z3, z3, z3, gates_t, gates_t, gbias, gbias, conv_w, conv_w, conv_b, conv_b, norm_w,
      consts["trif"], consts["trib"], consts["xc"], consts["bdm2"], consts["bd2"])


def _ret_kernel(q_ref, k_ref, v_ref, gt_ref, rd_ref, nw_ref, ca_ref, sa_ref, cb_ref, sb_ref,
                sgn_ref, perm_ref, bdm_ref, bd2_ref, out_ref,
                qr_ref, kr_ref, kvf_ref, kvb_ref, st_ref, tab_ref, *, seq):
    nc = seq // CHUNK
    lane_lo = _lane_lt64()
    rowf = lax.broadcasted_iota(jnp.int32, (CHUNK, LANES), 0).astype(F32)
    ri = lax.broadcasted_iota(jnp.int32, (CHUNK, CHUNK), 0)
    ci = lax.broadcasted_iota(jnp.int32, (CHUNK, CHUNK), 1)
    dist = (ri - ci).astype(F32)
    lg_f = -jnp.exp(rd_ref[0])
    lg_b = -jnp.exp(rd_ref[1])
    tab_ref[0] = jnp.exp(lg_f * (CHUNK - 1.0 - rowf))
    tab_ref[1] = jnp.exp(lg_b * rowf)
    tab_ref[2] = jnp.exp(lg_f * (rowf + 1.0))
    tab_ref[3] = jnp.exp(lg_b * (CHUNK - rowf))
    for j in (0, 1):
        lgf_j = lg_f[:, HEAD_DIM * j:HEAD_DIM * j + 1]
        lgb_j = lg_b[:, HEAD_DIM * j:HEAD_DIM * j + 1]
        tab_ref[4 + j] = (jnp.where(dist >= 0.0, jnp.exp(lgf_j * jnp.maximum(dist, 0.0)), 0.0)
                          + jnp.where(dist <= 0.0, jnp.exp(lgb_j * jnp.maximum(-dist, 0.0)), 0.0))
    cdec_f = jnp.exp(lg_f * float(CHUNK))
    cdec_b = jnp.exp(lg_b * float(CHUNK))

    def pre_body(c, carry):
        t0 = pl.multiple_of(c * CHUNK, CHUNK)
        cb = cb_ref[c]
        sb = sb_ref[c]
        cos = ca_ref[...] * cb - sa_ref[...] * sb
        sin = (sa_ref[...] * cb + ca_ref[...] * sb) * sgn_ref[...]
        rot = []
        for src, dst in ((q_ref, qr_ref), (k_ref, kr_ref)):
            x = src[pl.ds(t0, CHUNK), :]
            xs = _dot(x, perm_ref[...])
            xr = (x.astype(F32) * cos + xs * sin).astype(BF16)
            dst[pl.ds(t0, CHUNK), :] = xr
            rot.append(xr)
        kf = rot[1].astype(F32)
        v = v_ref[pl.ds(t0, CHUNK), :]
        kvf_ref[c] = (_dot_tn((kf * tab_ref[0]).astype(BF16), v) * bdm_ref[...]).astype(BF16)
        kvb_ref[c] = (_dot_tn((kf * tab_ref[1]).astype(BF16), v) * bdm_ref[...]).astype(BF16)
        return carry

    lax.fori_loop(0, nc, pre_body, 0, unroll=RET_UNROLL)

    st_ref[...] = jnp.zeros_like(st_ref)

    def scan_body(t, carry):
        for di, (kv_ref, cidx, cdec) in enumerate(((kvf_ref, t, cdec_f),
                                                   (kvb_ref, nc - 1 - t, cdec_b))):
            kv = kv_ref[cidx].astype(F32)
            s_st = st_ref[di]
            kv_ref[cidx] = s_st.astype(BF16)
            st_ref[di] = cdec * s_st + kv
        return carry

    lax.fori_loop(0, nc, scan_body, 0, unroll=RET_UNROLL)

    def out_body(c, carry):
        t0 = pl.multiple_of(c * CHUNK, CHUNK)
        q = qr_ref[pl.ds(t0, CHUNK), :]
        k = kr_ref[pl.ds(t0, CHUNK), :]
        v = v_ref[pl.ds(t0, CHUNK), :]
        qf = q.astype(F32)
        q_dec = jnp.concatenate([(qf * tab_ref[2]).astype(BF16), (qf * tab_ref[3]).astype(BF16)],
                                axis=1)
        states = jnp.concatenate([kvf_ref[c], kvb_ref[c]], axis=0)
        ys = []
        for j in (0, 1):
            qj = jnp.where(lane_lo if j == 0 else jnp.logical_not(lane_lo), q, jnp.zeros_like(q))
            s = _dot_nt(qj, k) * tab_ref[4 + j]
            ys.append(_dot(s.astype(BF16), v))
        h = jnp.where(lane_lo, ys[0], ys[1]) + _dot(q_dec, states)
        y = _head_norm(h, bd2_ref) * nw_ref[...]
        gt = gt_ref[pl.ds(t0, CHUNK), :].astype(F32)
        y = y * (gt * _sigmoid(gt))
        out_ref[pl.ds(t0, CHUNK), :] = y.astype(BF16)
        return carry

    lax.fori_loop(0, nc, out_body, 0, unroll=RET_UNROLL)


def _retention(z3, rd, norm_w, consts, rope):
    b, seq, _ = z3.shape
    nc = seq // CHUNK
    npair = H_RET // 2

    def zspec(blk0):
        return pl.BlockSpec((None, seq, LANES), lambda bi, p, blk0=blk0: (bi, 0, blk0 + p))

    def full2(shape):
        return pl.BlockSpec(shape, lambda bi, p: (0, 0))

    kern = functools.partial(_ret_kernel, seq=seq)
    return pl.pallas_call(
        kern,
        out_shape=jax.ShapeDtypeStruct((b, seq, W_RET), BF16),
        grid=(b, npair),
        in_specs=[zspec(ZB_RQ), zspec(ZB_RK), zspec(ZB_RV), zspec(ZB_RG),
                  pl.BlockSpec((None, 2, 1, LANES), lambda bi, p: (p, 0, 0, 0)),
                  pl.BlockSpec((1, LANES), lambda bi, p: (0, p)),
                  full2((CHUNK, LANES)), full2((CHUNK, LANES)),
                  pl.BlockSpec((nc, 1, LANES), lambda bi, p: (0, 0, 0)),
                  pl.BlockSpec((nc, 1, LANES), lambda bi, p: (0, 0, 0)),
                  full2((1, LANES)), full2((LANES, LANES)), full2((LANES, LANES)),
                  full2((2 * LANES, LANES))],
        out_specs=pl.BlockSpec((None, seq, LANES), lambda bi, p: (bi, 0, p)),
        scratch_shapes=[pltpu.VMEM((seq, LANES), BF16),
                        pltpu.VMEM((seq, LANES), BF16),
                        pltpu.VMEM((nc, CHUNK, LANES), BF16),
                        pltpu.VMEM((nc, CHUNK, LANES), BF16),
                        pltpu.VMEM((2, CHUNK, LANES), F32),
                        pltpu.VMEM((6, CHUNK, LANES), F32)],
        compiler_params=_cparams(("arbitrary", "arbitrary")),
        name="retention",
    )(z3, z3, z3, z3, rd, norm_w, rope["ca"], rope["sa"], rope["cb"], rope["sb"],
      consts["sgn"], consts["perm"], consts["bdm"], consts["bd2"])


def _na_kernel(q_ref, k_ref, v_ref, bias_ref, out_ref, *, seq):
    rows = seq // GRID_W
    ng = rows // NA_QROWS
    nq = NA_QROWS * GRID_W
    nk = NA_KROWS * GRID_W
    lane_lo = lax.broadcasted_iota(jnp.int32, (nq, LANES), 1) < HEAD_DIM
    ones_aug = jnp.ones((nk, LANES), BF16)

    def body(g, carry):
        t0 = pl.multiple_of(g * nq, nq)
        base = jnp.clip(g * NA_QROWS - NA_ROWS // 2, 0, rows - NA_KROWS)
        k0 = pl.multiple_of(base * GRID_W, GRID_W)
        case = jnp.where(g == 0, 0, jnp.where(g == ng - 1, 2, 1))
        q = q_ref[pl.ds(t0, nq), :]
        kk = k_ref[pl.ds(k0, nk), :]
        v_aug = jnp.concatenate([v_ref[pl.ds(k0, nk), :], ones_aug], axis=1)
        outs = []
        for j in (0, 1):
            qj = jnp.where(lane_lo if j == 0 else jnp.logical_not(lane_lo), q,
                           jnp.zeros_like(q))
            s = _dot_nt(qj, kk) + bias_ref[case, j]
            m = jnp.max(s, axis=1, keepdims=True)
            e = jnp.exp(s - m)
            pv = _dot(e.astype(BF16), v_aug)
            outs.append(pv[:, :LANES] / pv[:, LANES:])
        out_ref[pl.ds(t0, nq), :] = jnp.where(lane_lo, outs[0], outs[1]).astype(BF16)
        return carry

    lax.fori_loop(0, ng, body, 0, unroll=NA_UNROLL)


def _na(z3, bias):
    b, seq, _ = z3.shape
    npair = H_NA // 2
    nq = NA_QROWS * GRID_W
    nk = NA_KROWS * GRID_W

    def zspec(blk0):
        return pl.BlockSpec((None, seq, LANES), lambda bi, p, blk0=blk0: (bi, 0, blk0 + p))

    kern = functools.partial(_na_kernel, seq=seq)
    return pl.pallas_call(
        kern,
        out_shape=jax.ShapeDtypeStruct((b, seq, W_NA), BF16),
        grid=(b, npair),
        in_specs=[zspec(ZB_NQ), zspec(ZB_NK), zspec(ZB_NV),
                  pl.BlockSpec((None, 3, 2, nq, nk), lambda bi, p: (p, 0, 0, 0, 0))],
        out_specs=pl.BlockSpec((None, seq, LANES), lambda bi, p: (bi, 0, p)),
        compiler_params=_cparams(("arbitrary", "arbitrary")),
        name="natten",
    )(z3, z3, z3, bias)


def _na_index_tables(seq):
    rows = seq // GRID_W
    ng = rows // NA_QROWS
    tabs = []
    for g in (0, 1, ng - 1):
        base = int(np.clip(g * NA_QROWS - NA_ROWS // 2, 0, rows - NA_KROWS))
        qr = g * NA_QROWS + np.arange(NA_QROWS)[:, None, None, None]
        qc = np.arange(GRID_W)[None, :, None, None]
        kr = base + np.arange(NA_KROWS)[None, None, :, None]
        kc = np.arange(GRID_W)[None, None, None, :]
        rstart = np.clip(qr - NA_ROWS // 2, 0, rows - NA_ROWS)
        cstart = np.clip(qc - NA_COLS // 2, 0, GRID_W - NA_COLS)
        ok = (kr >= rstart) & (kr < rstart + NA_ROWS) & (kc >= cstart) & (kc < cstart + NA_COLS)
        rel_r = np.clip(kr - qr + NA_ROWS - 1, 0, 2 * NA_ROWS - 2)
        rel_c = np.clip(kc - qc + NA_COLS - 1, 0, 2 * NA_COLS - 2)
        shp = (NA_QROWS * GRID_W, NA_KROWS * GRID_W)
        full = (NA_QROWS, GRID_W, NA_KROWS, GRID_W)
        tabs.append((np.broadcast_to(ok, full).reshape(shp),
                     np.broadcast_to(rel_r, full).reshape(shp),
                     np.broadcast_to(rel_c, full).reshape(shp)))
    ok = np.stack([t[0] for t in tabs])
    rr = np.stack([t[1] for t in tabs])
    rc = np.stack([t[2] for t in tabs])
    return ok, rr, rc


def _na_bias(rpb, seq):
    ok, rr, rc = _na_index_tables(seq)
    h = rpb.shape[0]
    n_r, n_c = 2 * NA_ROWS - 1, 2 * NA_COLS - 1
    full = (3, NA_QROWS, GRID_W, NA_KROWS, GRID_W)
    rr5, rc5 = rr.reshape(full), rc.reshape(full)
    oh_r = (rr5[:, :, 0, :, 0][..., None] == np.arange(n_r)).astype(np.float32)
    oh_c = (rc5[0, 0, :, 0, :][..., None] == np.arange(n_c)).astype(np.float32)
    cols = jnp.einsum("hab,qkb->haqk", rpb.astype(F32), jnp.asarray(oh_c),
                      precision=lax.Precision.HIGHEST)
    bias = jnp.einsum("ciea,haqk->hciqek", jnp.asarray(oh_r), cols,
                      precision=lax.Precision.HIGHEST)
    bias = bias.reshape(h, 3, NA_QROWS * GRID_W, NA_KROWS * GRID_W)
    bias = jnp.where(ok[None], bias, NEG)
    return bias.reshape(h // 2, 2, 3, bias.shape[2], bias.shape[3]).transpose(0, 2, 1, 3, 4)


def _proj_out_kernel(ym_ref, yr_ref, yn_ref, x_ref, wo_ref, g_ref, rw_ref,
                     rb_ref, lst_ref, x2_ref, xn_ref, route_ref, cnt_ref):
    @pl.when(pl.program_id(0) == 0)
    def _():
        cnt_ref[...] = jnp.zeros_like(cnt_ref)

    y_cat = jnp.concatenate([ym_ref[...], yr_ref[...], yn_ref[...]], axis=1)
    x2 = x_ref[...] + _dot(y_cat, wo_ref[...])
    x2_ref[...] = x2
    ms = jnp.mean(x2 * x2, axis=-1, keepdims=True)
    xn = x2 * lax.rsqrt(ms + RMS_EPS) * g_ref[...]
    xn_ref[...] = xn
    xh = xn.astype(BF16)
    xl = (xn - xh.astype(F32)).astype(BF16)
    logits = (_dot(xh, rw_ref[0]) + _dot(xl, rw_ref[0]) + _dot(xh, rw_ref[1])) + rb_ref[...]
    tm = logits.shape[0]
    lane = lax.broadcasted_iota(jnp.int32, (tm, LANES), 1).astype(F32)
    glog = jnp.where(lane < N_GROUPS, logits, NEG)
    gmax = jnp.max(glog, axis=1, keepdims=True)
    grp = jnp.min(jnp.where(glog == gmax, lane, float(LANES)), axis=1, keepdims=True)
    p_grp = 1.0 / jnp.sum(jnp.exp(glog - gmax), axis=1, keepdims=True)
    lo = N_GROUPS + grp * EXPERTS_PER_GROUP
    ein = jnp.where((lane >= lo) & (lane < lo + EXPERTS_PER_GROUP), logits, NEG)
    v1 = jnp.max(ein, axis=1, keepdims=True)
    i1 = jnp.min(jnp.where(ein == v1, lane, float(LANES)), axis=1, keepdims=True)
    ein2 = jnp.where(lane == i1, NEG, ein)
    v2 = jnp.max(ein2, axis=1, keepdims=True)
    i2 = jnp.min(jnp.where(ein2 == v2, lane, float(LANES)), axis=1, keepdims=True)
    e21 = jnp.exp(v2 - v1)
    g1 = p_grp / (1.0 + e21)
    g2 = p_grp * e21 / (1.0 + e21)
    oh1 = jnp.where(lane == i1 - N_GROUPS, 1.0, 0.0)
    oh2 = jnp.where(lane == i2 - N_GROUPS, 1.0, 0.0)
    both = oh1 + oh2
    before = _dot(lst_ref[...], both.astype(BF16)) + cnt_ref[...]
    rk1 = jnp.sum(oh1 * before, axis=1, keepdims=True)
    rk2 = jnp.sum(oh2 * before, axis=1, keepdims=True)
    cnt_ref[...] += jnp.sum(both, axis=0, keepdims=True)
    vals = (i1 - N_GROUPS, i2 - N_GROUPS, g1, g2, rk1, rk2)
    route = jnp.zeros_like(logits)
    for c, val in enumerate(vals):
        route = jnp.where(lane == c, val, route)
    route_ref[...] = route


def _proj_out(ym, yr, yn, x2, wo, g, rw, rb):
    n = x2.shape[0]
    tm = PROJ_TM

    def rows(w):
        return pl.BlockSpec((tm, w), lambda i: (i, 0))

    def full(shape):
        return pl.BlockSpec(shape, lambda i: (0, 0))

    ii = np.arange(tm)
    lstrict = jnp.asarray(ii[None, :] < ii[:, None], BF16)
    return pl.pallas_call(
        _proj_out_kernel,
        out_shape=(jax.ShapeDtypeStruct((n, D_MODEL), F32),
                   jax.ShapeDtypeStruct((n, D_MODEL), F32),
                   jax.ShapeDtypeStruct((n, LANES), F32),
                   jax.ShapeDtypeStruct((1, LANES), F32)),
        grid=(n // tm,),
        in_specs=[rows(W_MLSTM), rows(W_RET), rows(W_NA), rows(D_MODEL),
                  full((D_MODEL, D_MODEL)), full((1, D_MODEL)),
                  pl.BlockSpec((2, D_MODEL, LANES), lambda i: (0, 0, 0)),
                  full((1, LANES)), full((tm, tm))],
        out_specs=(rows(D_MODEL), rows(D_MODEL), rows(LANES), full((1, LANES))),
        compiler_params=_cparams(("arbitrary",)),
        name="proj_out_router",
    )(ym, yr, yn, x2, wo, g, rw, rb, lstrict)


def _dispatch_kernel(pend_ref, cnt_ref, nused_ref, meta_hbm, xn_ref, xs_hbm,
                     meta0, meta1, zero_buf, sem_meta, sem_rows, sem_zero):
    i = pl.program_id(0)
    nsteps = pl.num_programs(0)
    slot = lax.rem(i, 2)
    n_blocks = xs_hbm.shape[0] // MOE_BLK

    metas = (meta0, meta1)

    def meta_copy(step, s):
        return pltpu.make_async_copy(meta_hbm.at[step], metas[s], sem_meta.at[s])

    def zero_copy(blk):
        start = pl.multiple_of(blk * MOE_BLK, MOE_BLK)
        return pltpu.make_async_copy(zero_buf, xs_hbm.at[pl.ds(start, MOE_BLK), :], sem_zero)

    @pl.when(i == 0)
    def _():
        meta_copy(0, 0).start()
        zero_buf[...] = jnp.zeros_like(zero_buf)

        def zstart(e, carry):
            @pl.when(cnt_ref[e] > 0)
            def _():
                zero_copy(pend_ref[e] // MOE_BLK - 1).start()
            return carry

        def zwait(e, carry):
            @pl.when(cnt_ref[e] > 0)
            def _():
                zero_copy(0).wait()
            return carry

        lax.fori_loop(0, N_EXPERTS, zstart, 0)
        lax.fori_loop(nused_ref[0], n_blocks, lambda b, c: (zero_copy(b).start(), c)[1], 0)
        lax.fori_loop(0, N_EXPERTS, zwait, 0)
        lax.fori_loop(nused_ref[0], n_blocks, lambda b, c: (zero_copy(0).wait(), c)[1], 0)

    def row_copy(r, dst):
        return pltpu.make_async_copy(xn_ref.at[pl.ds(r, 1), :], xs_hbm.at[pl.ds(dst, 1), :],
                                     sem_rows)

    for s in (0, 1):
        @pl.when(slot == s)
        def _(s=s):
            meta_copy(i, s).wait()

            @pl.when(i + 1 < nsteps)
            def _():
                meta_copy(i + 1, 1 - s).start()

            def issue(r, carry):
                for kk in range(TOP_K):
                    row_copy(r, metas[s][TOP_K * r + kk]).start()
                return carry

            lax.fori_loop(0, COMB_TM, issue, 0, unroll=DMA_ISSUE_UNROLL)

    def drain(r, carry):
        for kk in range(TOP_K):
            row_copy(r, 0).wait()
        return carry

    lax.fori_loop(0, COMB_TM, drain, 0, unroll=DMA_ISSUE_UNROLL)


def _dispatch(pend, counts, n_used, meta, xn, n_slots):
    n = xn.shape[0]
    tm = COMB_TM
    grid_spec = pltpu.PrefetchScalarGridSpec(
        num_scalar_prefetch=3,
        grid=(n // tm,),
        in_specs=[pl.BlockSpec(memory_space=pl.ANY),
                  pl.BlockSpec((tm, D_MODEL), lambda i, *_: (i, 0))],
        out_specs=pl.BlockSpec(memory_space=pl.ANY),
        scratch_shapes=[pltpu.SMEM((TOP_K * tm,), jnp.int32),
                        pltpu.SMEM((TOP_K * tm,), jnp.int32),
                        pltpu.VMEM((MOE_BLK, D_MODEL), F32),
                        pltpu.SemaphoreType.DMA((2,)),
                        pltpu.SemaphoreType.DMA,
                        pltpu.SemaphoreType.DMA],
    )
    return pl.pallas_call(
        _dispatch_kernel,
        out_shape=jax.ShapeDtypeStruct((n_slots, D_MODEL), F32),
        grid_spec=grid_spec,
        compiler_params=_cparams(("arbitrary",)),
        name="dispatch",
    )(pend, counts, n_used, meta, xn)


def _expert_kernel(be_ref, nused_ref, xs_ref, w1_ref, w3_ref, w2_ref, y_ref, wb1, wb3, wb2):
    i = pl.program_id(0)

    @pl.when((i == 0) | (be_ref[i] != be_ref[jnp.maximum(i - 1, 0)]))
    def _():
        wb1[...] = w1_ref[...].astype(BF16)
        wb3[...] = w3_ref[...].astype(BF16)
        wb2[...] = w2_ref[...].astype(BF16)

    @pl.when(i < nused_ref[0])
    def _():
        xb = xs_ref[...].astype(BF16)
        h1 = _dot(xb, wb1[...])
        h3 = _dot(xb, wb3[...])
        hb = (h1 * _sigmoid(h1) * h3).astype(BF16)
        y_ref[...] = _dot(hb, wb2[...])

    @pl.when(i >= nused_ref[0])
    def _():
        y_ref[...] = jnp.zeros_like(y_ref)


def _experts(block_expert, n_used, xs, w1, w3, w2, layer):
    n_blocks = xs.shape[0] // MOE_BLK

    def wspec(shape):
        return pl.BlockSpec(shape, lambda i, be, nu: (layer, be[i], 0, 0))

    grid_spec = pltpu.PrefetchScalarGridSpec(
        num_scalar_prefetch=2,
        grid=(n_blocks,),
        in_specs=[pl.BlockSpec((MOE_BLK, D_MODEL),
                               lambda i, be, nu: (jnp.minimum(i, nu[0] - 1), 0)),
                  wspec((None, None, D_MODEL, EXPERT_FF)), wspec((None, None, D_MODEL, EXPERT_FF)),
                  wspec((None, None, EXPERT_FF, D_MODEL))],
        out_specs=pl.BlockSpec((MOE_BLK, D_MODEL), lambda i, be, nu: (i, 0)),
        scratch_shapes=[pltpu.VMEM((D_MODEL, EXPERT_FF), BF16),
                        pltpu.VMEM((D_MODEL, EXPERT_FF), BF16),
                        pltpu.VMEM((EXPERT_FF, D_MODEL), BF16)],
    )
    return pl.pallas_call(
        _expert_kernel,
        out_shape=jax.ShapeDtypeStruct((n_blocks * MOE_BLK, D_MODEL), F32),
        grid_spec=grid_spec,
        compiler_params=_cparams(("arbitrary",)),
        name="experts",
    )(block_expert, n_used, xs, w1, w3, w2)


def _combine_kernel(meta_hbm, yb_hbm, x_ref, route_ref, g_ref, out_ref,
                    meta0, meta1, ybuf, sem_meta, sem_rows, *, final_norm):
    i = pl.program_id(0)
    nsteps = pl.num_programs(0)
    slot = lax.rem(i, 2)
    metas = (meta0, meta1)

    def meta_copy(step, s):
        return pltpu.make_async_copy(meta_hbm.at[step], metas[s], sem_meta.at[s])

    def row_copy(src, s, kk, r):
        return pltpu.make_async_copy(yb_hbm.at[pl.ds(src, 1), :],
                                     ybuf.at[s, kk, pl.ds(r, 1), :], sem_rows.at[s])

    def issue_tile(s):
        def issue(r, carry):
            for kk in range(TOP_K):
                row_copy(metas[s][TOP_K * r + kk], s, kk, r).start()
            return carry

        lax.fori_loop(0, COMB_TM, issue, 0, unroll=DMA_ISSUE_UNROLL)

    @pl.when(i == 0)
    def _():
        meta_copy(0, 0).start()
        meta_copy(0, 0).wait()
        issue_tile(0)

        @pl.when(nsteps > 1)
        def _():
            meta_copy(1, 1).start()

    for s in (0, 1):
        @pl.when(slot == s)
        def _(s=s):
            @pl.when(i + 1 < nsteps)
            def _():
                meta_copy(i + 1, 1 - s).wait()
                issue_tile(1 - s)

            @pl.when(i + 2 < nsteps)
            def _():
                meta_copy(i + 2, s).start()

            def drain(r, carry):
                for kk in range(TOP_K):
                    row_copy(0, s, kk, r).wait()
                return carry

            lax.fori_loop(0, COMB_TM, drain, 0, unroll=DMA_ISSUE_UNROLL)
            route = route_ref[...]
            y = x_ref[...]
            for kk in range(TOP_K):
                y = y + route[:, 2 + kk:3 + kk] * ybuf[s, kk]
            if final_norm:
                ms = jnp.mean(y * y, axis=-1, keepdims=True)
                y = y * lax.rsqrt(ms + RMS_EPS) * g_ref[...]
            out_ref[...] = y


def _combine(meta, yb, x2, route, g, final_norm):
    n = x2.shape[0]
    tm = COMB_TM
    kern = functools.partial(_combine_kernel, final_norm=final_norm)
    return pl.pallas_call(
        kern,
        out_shape=jax.ShapeDtypeStruct((n, D_MODEL), F32),
        grid=(n // tm,),
        in_specs=[pl.BlockSpec(memory_space=pl.ANY),
                  pl.BlockSpec(memory_space=pl.ANY),
                  pl.BlockSpec((tm, D_MODEL), lambda i: (i, 0)),
                  pl.BlockSpec((tm, LANES), lambda i: (i, 0)),
                  pl.BlockSpec((1, D_MODEL), lambda i: (0, 0))],
        out_specs=pl.BlockSpec((tm, D_MODEL), lambda i: (i, 0)),
        scratch_shapes=[pltpu.SMEM((TOP_K * tm,), jnp.int32),
                        pltpu.SMEM((TOP_K * tm,), jnp.int32),
                        pltpu.VMEM((2, TOP_K, tm, D_MODEL), F32),
                        pltpu.SemaphoreType.DMA((2,)),
                        pltpu.SemaphoreType.DMA((2,))],
        compiler_params=_cparams(("arbitrary",)),
        name="combine",
    )(meta, yb, x2, route, g)


def _dispatch_layout(route, counts_f, n):
    counts = counts_f[0, :N_EXPERTS].astype(jnp.int32)
    n_blocks = n * TOP_K // MOE_BLK + N_EXPERTS
    padded = (counts + MOE_BLK - 1) // MOE_BLK * MOE_BLK
    pend = jnp.cumsum(padded).astype(jnp.int32)
    pstart = pend - padded
    n_used = pend[-1] // MOE_BLK
    blk_start = jnp.arange(n_blocks, dtype=jnp.int32) * MOE_BLK
    blk_start = jnp.minimum(blk_start, (n_used - 1) * MOE_BLK)
    block_expert = jnp.sum((pend[None, :] <= blk_start[:, None]).astype(jnp.int32), axis=1)
    block_expert = jnp.clip(block_expert, 0, N_EXPERTS - 1).astype(jnp.int32)
    expert = route[:, 0:TOP_K].astype(jnp.int32)
    rank = route[:, 4:4 + TOP_K].astype(jnp.int32)
    seg = jnp.sum(jnp.where(expert[..., None] == jnp.arange(N_EXPERTS, dtype=jnp.int32),
                            pstart, 0), axis=-1)
    meta = (seg + rank).reshape(n // COMB_TM, TOP_K * COMB_TM)
    return block_expert, n_used.reshape(1), pend, counts, meta, n_blocks * MOE_BLK


def _const_tables():
    lane = np.arange(LANES)
    grp = lane // HEAD_DIM
    xc = np.zeros((2, 24, 8 * LANES), np.float32)
    for kk in range(4):
        xc[:, kk, kk * LANES:(kk + 1) * LANES] = 1.0
    for di in range(2):
        for j in range(2):
            for blk, base in ((4 + di, 8), (6 + di, 16)):
                lo = blk * LANES + j * HEAD_DIM
                xc[:, base + 2 * di + j, lo:lo + HEAD_DIM] = 1.0
    ii = np.arange(CHUNK)
    incl_f = (ii[:, None] <= ii[None, :]).astype(np.float32)
    incl_b = (ii[:, None] >= ii[None, :]).astype(np.float32)
    bd = (grp[:, None] == grp[None, :]).astype(np.float32)
    perm = np.zeros((LANES, LANES), np.float32)
    half = HEAD_DIM // 2
    src = (lane // HEAD_DIM) * HEAD_DIM + (lane % HEAD_DIM + half) % HEAD_DIM
    perm[src, lane] = 1.0
    sgn = np.where(lane % HEAD_DIM < half, -1.0, 1.0).astype(np.float32)[None, :]
    return {
        "xc": jnp.asarray(xc.reshape(48, 8 * LANES), BF16),
        "trif": jnp.asarray(np.concatenate([incl_f, incl_f], 0), BF16),
        "trib": jnp.asarray(np.concatenate([incl_b, incl_b], 0), BF16),
        "bdm": jnp.asarray(bd, F32), "bdm2": jnp.asarray(np.concatenate([bd, bd], 1), F32),
        "bd2": jnp.asarray(np.concatenate([bd, bd], 0), BF16),
        "perm": jnp.asarray(perm, BF16), "sgn": jnp.asarray(sgn, F32),
    }


def _rope_tables(seq):
    half = HEAD_DIM // 2
    nc = seq // CHUNK
    inv_freq = ROPE_BASE ** (-np.arange(half, dtype=np.float64) / half)
    freq = inv_freq[np.arange(LANES) % half]
    ang_a = np.arange(CHUNK, dtype=np.float64)[:, None] * freq[None, :]
    ang_b = (np.arange(nc, dtype=np.float64) * CHUNK)[:, None] * freq[None, :]
    return {"ca": jnp.asarray(np.cos(ang_a), F32), "sa": jnp.asarray(np.sin(ang_a), F32),
            "cb": jnp.asarray(np.cos(ang_b)[:, None, :], F32),
            "sb": jnp.asarray(np.sin(ang_b)[:, None, :], F32)}


def _layer_weights(l, w_in, mlstm_gate_bias, ret_decay, w_out, router_group_w, router_group_b,
                   router_expert_w, router_expert_b):
    sizes = (W_MLSTM,) * 4 + (4 * H_MLSTM,) + (W_RET,) * 4 + (W_NA,) * 3
    offs = np.concatenate([[0], np.cumsum(sizes)])
    col = lambda i: w_in[l][:, int(offs[i]):int(offs[i + 1])]
    mq, mk, mv, mo, mg, rq, rk, rv, rg, nq, nk, nv = [col(i) for i in range(12)]
    scale = HEAD_DIM ** -0.5
    w_all = jnp.concatenate([mq, mk, mv, mo, rq, rk * scale, rv, rg, nq * scale, nk, nv],
                            axis=1).astype(BF16)
    sel = np.zeros((GATE_ROWS, 4 * H_MLSTM), np.float32)
    for kind in range(2):
        for p in range(H_MLSTM // 2):
            for k in range(4):
                sel[(kind * (H_MLSTM // 2) + p) * 8 + k,
                    (2 * kind + k // 2) * H_MLSTM + 2 * p + k % 2] = 1.0
    sel = jnp.asarray(sel)
    wg = jnp.dot(sel, mg.T, precision=lax.Precision.HIGHEST).astype(BF16)
    gbias = jnp.dot(sel, mlstm_gate_bias[l].astype(F32).reshape(-1, 1),
                    precision=lax.Precision.HIGHEST)
    rd = ret_decay[l].astype(F32)
    rd = jnp.repeat(rd.reshape(2, H_RET // 2, 2), HEAD_DIM, axis=2)
    rd = rd.transpose(1, 0, 2)[:, :, None, :]
    wo = w_out[l].astype(BF16)
    rw = jnp.concatenate([router_group_w[l], router_expert_w[l],
                          jnp.zeros((D_MODEL, LANES - N_GROUPS - N_EXPERTS), F32)], axis=1)
    rb = jnp.concatenate([router_group_b[l].astype(F32), router_expert_b[l].astype(F32),
                          jnp.zeros((LANES - N_GROUPS - N_EXPERTS,), F32)])[None, :]
    rw_hi = rw.astype(BF16)
    rw2 = jnp.stack([rw_hi, (rw - rw_hi.astype(F32)).astype(BF16)])
    return w_all, wg, gbias, rd, wo, rw2, rb


def _encoder(x, consts, norm_mix, w_in, mlstm_conv_w, mlstm_conv_b, mlstm_gate_bias, mlstm_norm,
             ret_decay, ret_norm, na_rpb, w_out, norm_ffn, router_group_w, router_group_b,
             router_expert_w, router_expert_b, expert_w1, expert_w3, expert_w2, norm_final):
    b, seq, _ = x.shape
    n = b * seq
    depth = w_in.shape[0]
    rope = _rope_tables(seq)
    x2 = x.reshape(n, D_MODEL).astype(F32)
    for l in range(depth):
        w_all, wg, gbias, rd, wo, rw, rb = _layer_weights(
            l, w_in, mlstm_gate_bias, ret_decay, w_out, router_group_w, router_group_b,
            router_expert_w, router_expert_b)
        z, gates_t = _proj_in(x2, norm_mix[l].astype(F32)[None, :], w_all, wg)
        z3 = z.reshape(b, seq, Z_WIDTH)
        y_m = _mlstm(z3, gates_t, gbias, mlstm_conv_w[l].astype(F32),
                     mlstm_conv_b[l].astype(F32)[None, :], mlstm_norm[l].astype(F32)[None, :], consts)
        y_r = _retention(z3, rd, ret_norm[l].astype(F32)[None, :], consts, rope)
        y_n = _na(z3, _na_bias(na_rpb[l], seq))
        x2, xn, route, counts_f = _proj_out(y_m.reshape(n, W_MLSTM), y_r.reshape(n, W_RET),
                                            y_n.reshape(n, W_NA), x2, wo,
                                            norm_ffn[l].astype(F32)[None, :], rw, rb)
        block_expert, n_used, pend, counts, meta, n_slots = _dispatch_layout(route, counts_f, n)
        xs = _dispatch(pend, counts, n_used, meta, xn, n_slots)
        yb = _experts(block_expert, n_used, xs, expert_w1.astype(F32), expert_w3.astype(F32),
                      expert_w2.astype(F32), l)
        x2 = _combine(meta, yb, x2, route, norm_final.astype(F32)[None, :],
                      final_norm=(l == depth - 1))
    return x2.reshape(b, seq, D_MODEL)


def kernel(x_prompt, x_sample, norm_mix, w_in, mlstm_conv_w, mlstm_conv_b, mlstm_gate_bias,
           mlstm_norm, ret_decay, ret_norm, na_rpb, w_out, norm_ffn, router_group_w,
           router_group_b, router_expert_w, router_expert_b, expert_w1, expert_w3, expert_w2,
           norm_final):
    consts = _const_tables()
    weights = (norm_mix, w_in, mlstm_conv_w, mlstm_conv_b, mlstm_gate_bias, mlstm_norm,
               ret_decay, ret_norm, na_rpb, w_out, norm_ffn, router_group_w, router_group_b,
               router_expert_w, router_expert_b, expert_w1, expert_w3, expert_w2, norm_final)
    return (_encoder(x_prompt, consts, *weights), _encoder(x_sample, consts, *weights))
```

```python
import functools

import numpy as np
import jax
import jax.numpy as jnp
from jax import lax
from jax.experimental import pallas as pl
from jax.experimental.pallas import tpu as pltpu

F32 = jnp.float32
BF16 = jnp.bfloat16

D_MODEL = 1024
HEAD_DIM = 64
LANES = 128
CHUNK = 128
H_MLSTM = 6
H_RET = 6
H_NA = 4
W_MLSTM = H_MLSTM * HEAD_DIM
W_RET = H_RET * HEAD_DIM
W_NA = H_NA * HEAD_DIM
Z_WIDTH = 4 * W_MLSTM + 4 * W_RET + 3 * W_NA
GATE_ROWS = 2 * (H_MLSTM // 2) * 8
ROPE_BASE = 10000.0
GRID_W = 64
NA_ROWS = 8
NA_COLS = 16
NA_QROWS = 4
NA_KROWS = NA_QROWS + NA_ROWS
N_GROUPS = 4
EXPERTS_PER_GROUP = 8
N_EXPERTS = 32
TOP_K = 2
EXPERT_FF = 512
RMS_EPS = 1e-6
NEG = -1e30
VMEM_LIMIT = 56 * 1024 * 1024

PROJ_TM = 1024
MOE_BLK = 512
COMB_TM = 512
MLSTM_GATE_UNROLL = 2
MLSTM_PRE_UNROLL = 4
MLSTM_SCAN_UNROLL = 2
MLSTM_OUT_UNROLL = 8
MLSTM_GROUP = 4
RET_UNROLL = 16
RET_GROUP = 4
NA_UNROLL = 4
DMA_ISSUE_UNROLL = 8
GATE_GROUP = 8

ZB_MQ, ZB_MK, ZB_MV, ZB_MO = 0, 3, 6, 9
ZB_RQ, ZB_RK, ZB_RV, ZB_RG = 12, 15, 18, 21
ZB_NQ, ZB_NK, ZB_NV = 24, 26, 28


def _dot(a, b):
    return jnp.dot(a, b, preferred_element_type=F32)


def _dot_nt(a, b):
    return lax.dot_general(a, b, (((1,), (1,)), ((), ())), preferred_element_type=F32)


def _dot_tn(a, b):
    return lax.dot_general(a, b, (((0,), (0,)), ((), ())), preferred_element_type=F32)


def _split(x, axis):
    hi = x.astype(BF16).astype(F32)
    return jnp.concatenate([hi, x - hi], axis=axis).astype(BF16)


def _sigmoid(x):
    return 1.0 / (1.0 + jnp.exp(-x))


def _log_sigmoid(x):
    return -(jnp.maximum(-x, 0.0) + jnp.log(1.0 + jnp.exp(-jnp.abs(x))))


def _cparams(sem):
    return pltpu.CompilerParams(dimension_semantics=sem, vmem_limit_bytes=VMEM_LIMIT)


def _proj_in_kernel(x_ref, g_ref, w_ref, wg_ref, z_ref, gate_ref):
    x = x_ref[...]
    ms = jnp.mean(x * x, axis=-1, keepdims=True)
    hx = (x * lax.rsqrt(ms + RMS_EPS) * g_ref[...]).astype(BF16)
    cw = 768
    for j in range(0, Z_WIDTH, cw):
        z_ref[:, j:j + cw] = _dot(hx, w_ref[:, j:j + cw]).astype(BF16)
    gate_ref[...] = _dot_nt(wg_ref[...], hx)


def _proj_in(x2, g, w, wg):
    n = x2.shape[0]
    return pl.pallas_call(
        _proj_in_kernel,
        out_shape=(jax.ShapeDtypeStruct((n, Z_WIDTH), BF16),
                   jax.ShapeDtypeStruct((GATE_ROWS, n), F32)),
        grid=(n // PROJ_TM,),
        in_specs=[pl.BlockSpec((PROJ_TM, D_MODEL), lambda i: (i, 0)),
                  pl.BlockSpec((1, D_MODEL), lambda i: (0, 0)),
                  pl.BlockSpec((D_MODEL, Z_WIDTH), lambda i: (0, 0)),
                  pl.BlockSpec((GATE_ROWS, D_MODEL), lambda i: (0, 0))],
        out_specs=(pl.BlockSpec((PROJ_TM, Z_WIDTH), lambda i: (i, 0)),
                   pl.BlockSpec((GATE_ROWS, PROJ_TM), lambda i: (0, i))),
        compiler_params=_cparams(("arbitrary",)),
        name="proj_in",
    )(x2, g, w, wg)


def _lane_lt64():
    return lax.broadcasted_iota(jnp.int32, (CHUNK, LANES), 1) < HEAD_DIM


def _tri_mask(d):
    r = lax.broadcasted_iota(jnp.int32, (CHUNK, CHUNK), 0)
    c = lax.broadcasted_iota(jnp.int32, (CHUNK, CHUNK), 1)
    return (c <= r) if d == 0 else (c >= r)


def _head_norm(h, bd2_ref):
    ms = _dot(_split(h * h, 1), bd2_ref[...]) * (1.0 / HEAD_DIM)
    return h * lax.rsqrt(ms + RMS_EPS)


def _mlstm_kernel(q_ref, k_ref, v_ref, o_ref, li_ref, lf_ref, bli_ref, blf_ref, cwq_ref, cwk_ref,
                  cbq_ref, cbk_ref, nw_ref, trif_ref, trib_ref, xc_ref, bdm_ref, bd2_ref,
                  out_ref, qc_ref, kc_ref, kvf_ref, kvb_ref, st_ref, rr_ref, aa_ref, cc_ref,
                  atot_ref, mloc_ref, mpf_ref, mpb_ref, ee_ref, *, seq):
    nc = seq // CHUNK
    lane_lo = _lane_lt64()
    row = lax.broadcasted_iota(jnp.int32, (CHUNK, LANES), 0)
    grow = lax.broadcasted_iota(jnp.int32, (8, CHUNK), 0)
    is_fwd = grow < 2
    ones_aug = jnp.ones((CHUNK, LANES), BF16)
    tri_f = _tri_mask(0)
    tri_b = _tri_mask(1)

    def conv_chunk(src_ref, w_ref, b_ref, c, scale):
        t0 = pl.multiple_of(c * CHUNK, CHUNK)
        x = src_ref[pl.ds(t0, CHUNK), :].astype(F32)
        tp = pl.multiple_of(jnp.maximum(t0 - 16, 0), 16)
        tn = pl.multiple_of(jnp.minimum(t0 + CHUNK, seq - 16), 16)
        prev_last = src_ref[pl.ds(tp, 16), :].astype(F32)[15:16, :]
        next_first = src_ref[pl.ds(tn, 16), :].astype(F32)[0:1, :]
        prev_last = prev_last * jnp.where(c > 0, 1.0, 0.0)
        next_first = next_first * jnp.where(c < nc - 1, 1.0, 0.0)
        xm1 = jnp.where(row == 0, prev_last, pltpu.roll(x, 1, 0))
        xp1 = jnp.where(row == CHUNK - 1, next_first, pltpu.roll(x, CHUNK - 1, 0))
        y = w_ref[0:1, :] * xm1 + w_ref[1:2, :] * x + w_ref[2:3, :] * xp1 + b_ref[...]
        return y * _sigmoid(y) * scale

    def v_aug_at(t0):
        return jnp.concatenate([v_ref[pl.ds(t0, CHUNK), :], ones_aug], axis=1)

    rows_g = GATE_GROUP * 8
    grow_g = lax.broadcasted_iota(jnp.int32, (rows_g, CHUNK), 0)
    glane_g = lax.broadcasted_iota(jnp.int32, (rows_g, CHUNK), 1)
    is_fwd_g = lax.rem(grow_g, 8) < 2

    def stack_chunks(w):
        return jnp.concatenate([w[:, i * CHUNK:(i + 1) * CHUNK] for i in range(GATE_GROUP)],
                               axis=0)

    def gate_body(gi, carry):
        c0 = pl.multiple_of(gi * GATE_GROUP, GATE_GROUP)
        t0 = pl.multiple_of(gi * (GATE_GROUP * CHUNK), GATE_GROUP * CHUNK)
        li = stack_chunks(li_ref[:, pl.ds(t0, GATE_GROUP * CHUNK)] + bli_ref[...])
        lf = stack_chunks(_log_sigmoid(lf_ref[:, pl.ds(t0, GATE_GROUP * CHUNK)] + blf_ref[...]))
        lfs = _split(lf, 1)
        a = jnp.where(is_fwd_g, _dot(lfs, trif_ref[...]), _dot(lfs, trib_ref[...]))
        r = a - li
        xf = -r
        xb = -r
        s = 1
        while s < CHUNK:
            xf = jnp.maximum(xf, jnp.where(glane_g >= s, pltpu.roll(xf, s, 1), NEG))
            xb = jnp.maximum(xb, jnp.where(glane_g < CHUNK - s, pltpu.roll(xb, CHUNK - s, 1), NEG))
            s *= 2
        a_tot = jnp.where(is_fwd_g, a[:, CHUNK - 1:CHUNK], a[:, 0:1])
        w_loc = a_tot - r
        m_loc = jnp.max(w_loc, axis=1, keepdims=True)
        tiles = ((rr_ref, r), (aa_ref, a), (cc_ref, jnp.where(is_fwd_g, xf, xb)),
                 (atot_ref, a_tot), (mloc_ref, jnp.broadcast_to(m_loc, (rows_g, CHUNK))),
                 (ee_ref, jnp.exp(w_loc - m_loc)))
        for ref, val in tiles:
            ref[pl.ds(c0, GATE_GROUP)] = val.reshape(GATE_GROUP, 8, CHUNK)
        return carry

    lax.fori_loop(0, nc // GATE_GROUP, gate_body, 0, unroll=MLSTM_GATE_UNROLL)

    def pre_body(c, carry):
        t0 = pl.multiple_of(c * CHUNK, CHUNK)
        qc_ref[pl.ds(t0, CHUNK), :] = conv_chunk(q_ref, cwq_ref, cbq_ref, c, 1.0).astype(BF16)
        kc = conv_chunk(k_ref, cwk_ref, cbk_ref, c, HEAD_DIM ** -0.5).astype(BF16)
        kc_ref[pl.ds(t0, CHUNK), :] = kc
        k_t = kc.astype(F32).T
        e = ee_ref[c]
        v_aug = v_aug_at(t0)
        for di, kv_ref in enumerate((kvf_ref, kvb_ref)):
            e_rows = jnp.where(row < HEAD_DIM, e[2 * di:2 * di + 1, :], e[2 * di + 1:2 * di + 2, :])
            kv = _dot((k_t * e_rows).astype(BF16), v_aug) * bdm_ref[...]
            kv_ref[c] = kv.astype(BF16)
        return carry

    lax.fori_loop(0, nc, pre_body, 0, unroll=MLSTM_PRE_UNROLL)

    st_ref[...] = jnp.zeros_like(st_ref)

    def lane_pair(t, k0):
        v = jnp.where(lane_lo[0:1, :], t[k0:k0 + 1, :], t[k0 + 1:k0 + 2, :])
        return jnp.concatenate([v, v], axis=1)

    def scan_body(t, m):
        cf = t
        cb = nc - 1 - t
        atot = jnp.where(is_fwd, atot_ref[cf], atot_ref[cb])
        mloc = jnp.where(is_fwd, mloc_ref[cf], mloc_ref[cb])
        mpf_ref[cf] = m
        mpb_ref[cb] = m
        m_new = jnp.maximum(atot + m, mloc)
        s_old = jnp.exp(atot + m - m_new)
        s_new = jnp.exp(mloc - m_new)
        for di, (kv_ref, cidx) in enumerate(((kvf_ref, cf), (kvb_ref, cb))):
            kv = kv_ref[cidx].astype(F32)
            s_st = st_ref[di]
            kv_ref[cidx] = s_st.astype(BF16)
            st_ref[di] = lane_pair(s_old, 2 * di) * s_st + lane_pair(s_new, 2 * di) * kv
        return m_new

    lax.fori_loop(0, nc, scan_body, jnp.zeros((8, CHUNK), F32),
                  unroll=MLSTM_SCAN_UNROLL)

    def out_body(g, carry):
        cs = [g * MLSTM_GROUP + i for i in range(MLSTM_GROUP)]
        t0s = [pl.multiple_of(c * CHUNK, CHUNK) for c in cs]
        qs = [qc_ref[pl.ds(t0, CHUNK), :] for t0 in t0s]
        ks = [kc_ref[pl.ds(t0, CHUNK), :] for t0 in t0s]
        v_augs = [v_aug_at(t0) for t0 in t0s]
        bcs = []
        for c in cs:
            mp = jnp.where(is_fwd, mpf_ref[c], mpb_ref[c])
            u = -jnp.maximum(cc_ref[c], mp)
            cols = jnp.concatenate([u, jnp.exp(mp + u), jnp.exp(u - aa_ref[c])], axis=0)
            bcs.append(_dot_tn(_split(cols, 0), xc_ref[...]))
        scores = [[_dot_nt(jnp.where(lane_lo if j == 0 else jnp.logical_not(lane_lo), q,
                                     jnp.zeros_like(q)), k) for j in (0, 1)]
                  for q, k in zip(qs, ks)]
        q_ss = [[_dot(q, kv_ref[c]) for kv_ref in (kvf_ref, kvb_ref)] for q, c in zip(qs, cs)]
        ps = []
        for c, bc, sc in zip(cs, bcs, scores):
            r = rr_ref[c]
            row_p = []
            for di, tri in enumerate((tri_f, tri_b)):
                for j in (0, 1):
                    kk = 2 * di + j
                    arg = jnp.where(tri, bc[:, kk * LANES:(kk + 1) * LANES] - r[kk:kk + 1, :], NEG)
                    row_p.append((sc[j] * jnp.exp(arg)).astype(BF16))
            ps.append(row_p)
        pvs = [[_dot(p, v_aug) for p in row_p] for row_p, v_aug in zip(ps, v_augs)]
        hs = []
        for bc, pv, q_s in zip(bcs, pvs, q_ss):
            h = None
            for di in (0, 1):
                p0, p1 = pv[2 * di], pv[2 * di + 1]
                sint = bc[:, (4 + di) * LANES:(5 + di) * LANES]
                em = bc[:, (6 + di) * LANES:(7 + di) * LANES]
                num = jnp.where(lane_lo, p0[:, :LANES], p1[:, :LANES]) + sint * q_s[di][:, :LANES]
                den = jnp.where(lane_lo, p0[:, LANES:], p1[:, LANES:]) + sint * q_s[di][:, LANES:]
                hd = num / jnp.maximum(jnp.abs(den), em)
                h = hd if h is None else h + hd
            hs.append(h)
        ms = [_dot(_split(h * h, 1), bd2_ref[...]) for h in hs]
        for h, m, t0 in zip(hs, ms, t0s):
            y = h * lax.rsqrt(m * (1.0 / HEAD_DIM) + RMS_EPS) * nw_ref[...]
            y = y * _sigmoid(o_ref[pl.ds(t0, CHUNK), :].astype(F32))
            out_ref[pl.ds(t0, CHUNK), :] = y.astype(BF16)
        return carry

    lax.fori_loop(0, nc // MLSTM_GROUP, out_body, 0, unroll=MLSTM_OUT_UNROLL // MLSTM_GROUP)


def _mlstm(z3, gates_t, gbias, conv_w, conv_b, norm_w, consts):
    b, seq, _ = z3.shape
    nc = seq // CHUNK
    npair = H_MLSTM // 2

    def zspec(blk0):
        return pl.BlockSpec((None, seq, LANES), lambda bi, p, blk0=blk0: (bi, 0, blk0 + p))

    def full2(shape):
        return pl.BlockSpec(shape, lambda bi, p: (0, 0))

    gate_tile = pltpu.VMEM((nc, 8, CHUNK), F32)
    kern = functools.partial(_mlstm_kernel, seq=seq)
    return pl.pallas_call(
        kern,
        out_shape=jax.ShapeDtypeStruct((b, seq, W_MLSTM), BF16),
        grid=(b, npair),
        in_specs=[zspec(ZB_MQ), zspec(ZB_MK), zspec(ZB_MV), zspec(ZB_MO),
                  pl.BlockSpec((8, seq), lambda bi, p: (p, bi)),
                  pl.BlockSpec((8, seq), lambda bi, p: (npair + p, bi)),
                  pl.BlockSpec((8, 1), lambda bi, p: (p, 0)),
                  pl.BlockSpec((8, 1), lambda bi, p: (npair + p, 0)),
                  pl.BlockSpec((3, LANES), lambda bi, p: (0, p)),
                  pl.BlockSpec((3, LANES), lambda bi, p: (0, npair + p)),
                  pl.BlockSpec((1, LANES), lambda bi, p: (0, p)),
                  pl.BlockSpec((1, LANES), lambda bi, p: (0, npair + p)),
                  pl.BlockSpec((1, LANES), lambda bi, p: (0, p)),
                  full2((2 * CHUNK, CHUNK)), full2((2 * CHUNK, CHUNK)),
                  full2((48, 8 * LANES)),
                  full2((LANES, 2 * LANES)), full2((2 * LANES, LANES))],
        out_specs=pl.BlockSpec((None, seq, LANES), lambda bi, p: (bi, 0, p)),
        scratch_shapes=[pltpu.VMEM((seq, LANES), BF16),
                        pltpu.VMEM((seq, LANES), BF16),
                        pltpu.VMEM((nc, CHUNK, 2 * LANES), BF16),
                        pltpu.VMEM((nc, CHUNK, 2 * LANES), BF16),
                        pltpu.VMEM((2, CHUNK, 2 * LANES), F32)] + [gate_tile] * 8,
        compiler_params=_cparams(("arbitrary", "arbitrary")),
        name="mlstm",
    )(z3, z3, z3, z3, gates_t, gates_t, gbias, gbias, conv_w, conv_w, conv_b, conv_b, norm_w,
      consts["trif"], consts["trib"], consts["xc"], consts["bdm2"], consts["bd2"])


def _ret_kernel(q_ref, k_ref, v_ref, gt_ref, rd_ref, nw_ref, ca_ref, sa_ref, cb_ref, sb_ref,
                sgn_ref, perm_ref, bdm_ref, bd2_ref, out_ref,
                qr_ref, kr_ref, kvf_ref, kvb_ref, st_ref, tab_ref, *, seq):
    nc = seq // CHUNK
    lane_lo = _lane_lt64()
    rowf = lax.broadcasted_iota(jnp.int32, (CHUNK, LANES), 0).astype(F32)
    ri = lax.broadcasted_iota(jnp.int32, (CHUNK, CHUNK), 0)
    ci = lax.broadcasted_iota(jnp.int32, (CHUNK, CHUNK), 1)
    dist = (ri - ci).astype(F32)
    lg_f = -jnp.exp(rd_ref[0])
    lg_b = -jnp.exp(rd_ref[1])
    tab_ref[0] = jnp.exp(lg_f * (CHUNK - 1.0 - rowf))
    tab_ref[1] = jnp.exp(lg_b * rowf)
    tab_ref[2] = jnp.exp(lg_f * (rowf + 1.0))
    tab_ref[3] = jnp.exp(lg_b * (CHUNK - rowf))
    for j in (0, 1):
        lgf_j = lg_f[:, HEAD_DIM * j:HEAD_DIM * j + 1]
        lgb_j = lg_b[:, HEAD_DIM * j:HEAD_DIM * j + 1]
        tab_ref[4 + j] = (jnp.where(dist >= 0.0, jnp.exp(lgf_j * jnp.maximum(dist, 0.0)), 0.0)
                          + jnp.where(dist <= 0.0, jnp.exp(lgb_j * jnp.maximum(-dist, 0.0)), 0.0))
    cdec_f = jnp.exp(lg_f * float(CHUNK))
    cdec_b = jnp.exp(lg_b * float(CHUNK))

    def pre_body(g, carry):
        cs = [g * RET_GROUP + i for i in range(RET_GROUP)]
        t0s = [pl.multiple_of(c * CHUNK, CHUNK) for c in cs]
        xq = [q_ref[pl.ds(t0, CHUNK), :] for t0 in t0s]
        xk = [k_ref[pl.ds(t0, CHUNK), :] for t0 in t0s]
        sq = [_dot(x, perm_ref[...]) for x in xq]
        sk = [_dot(x, perm_ref[...]) for x in xk]
        krs = []
        for i, (c, t0) in enumerate(zip(cs, t0s)):
            cb = cb_ref[c]
            sb = sb_ref[c]
            cos = ca_ref[...] * cb - sa_ref[...] * sb
            sin = (sa_ref[...] * cb + ca_ref[...] * sb) * sgn_ref[...]
            qr_ref[pl.ds(t0, CHUNK), :] = (xq[i].astype(F32) * cos + sq[i] * sin).astype(BF16)
            kr = (xk[i].astype(F32) * cos + sk[i] * sin).astype(BF16)
            kr_ref[pl.ds(t0, CHUNK), :] = kr
            krs.append(kr.astype(F32))
        vs = [v_ref[pl.ds(t0, CHUNK), :] for t0 in t0s]
        kvs = [[_dot_tn((kf * tab_ref[di]).astype(BF16), v) for di in (0, 1)]
               for kf, v in zip(krs, vs)]
        for c, kv in zip(cs, kvs):
            kvf_ref[c] = (kv[0] * bdm_ref[...]).astype(BF16)
            kvb_ref[c] = (kv[1] * bdm_ref[...]).astype(BF16)
        return carry

    lax.fori_loop(0, nc // RET_GROUP, pre_body, 0, unroll=RET_UNROLL // RET_GROUP)

    st_ref[...] = jnp.zeros_like(st_ref)

    def scan_body(t, carry):
        for di, (kv_ref, cidx, cdec) in enumerate(((kvf_ref, t, cdec_f),
                                                   (kvb_ref, nc - 1 - t, cdec_b))):
            kv = kv_ref[cidx].astype(F32)
            s_st = st_ref[di]
            kv_ref[cidx] = s_st.astype(BF16)
            st_ref[di] = cdec * s_st + kv
        return carry

    lax.fori_loop(0, nc, scan_body, 0, unroll=RET_UNROLL)

    def out_body(g, carry):
        cs = [g * RET_GROUP + i for i in range(RET_GROUP)]
        t0s = [pl.multiple_of(c * CHUNK, CHUNK) for c in cs]
        qs = [qr_ref[pl.ds(t0, CHUNK), :] for t0 in t0s]
        ks = [kr_ref[pl.ds(t0, CHUNK), :] for t0 in t0s]
        vs = [v_ref[pl.ds(t0, CHUNK), :] for t0 in t0s]
        scores = [[_dot_nt(jnp.where(lane_lo if j == 0 else jnp.logical_not(lane_lo), q,
                                     jnp.zeros_like(q)), k) for j in (0, 1)]
                  for q, k in zip(qs, ks)]
        inters = []
        for q, c in zip(qs, cs):
            qf = q.astype(F32)
            q_dec = jnp.concatenate([(qf * tab_ref[2]).astype(BF16),
                                     (qf * tab_ref[3]).astype(BF16)], axis=1)
            inters.append(_dot(q_dec, jnp.concatenate([kvf_ref[c], kvb_ref[c]], axis=0)))
        ps = [[(s[j] * tab_ref[4 + j]).astype(BF16) for j in (0, 1)] for s in scores]
        ys = [[_dot(p[j], v) for j in (0, 1)] for p, v in zip(ps, vs)]
        hs = [jnp.where(lane_lo, y[0], y[1]) + it for y, it in zip(ys, inters)]
        ms = [_dot(_split(h * h, 1), bd2_ref[...]) for h in hs]
        for h, m, t0 in zip(hs, ms, t0s):
            y = h * lax.rsqrt(m * (1.0 / HEAD_DIM) + RMS_EPS) * nw_ref[...]
            gt = gt_ref[pl.ds(t0, CHUNK), :].astype(F32)
            out_ref[pl.ds(t0, CHUNK), :] = (y * (gt * _sigmoid(gt))).astype(BF16)
        return carry

    lax.fori_loop(0, nc // RET_GROUP, out_body, 0, unroll=RET_UNROLL // RET_GROUP)


def _retention(z3, rd, norm_w, consts, rope):
    b, seq, _ = z3.shape
    nc = seq // CHUNK
    npair = H_RET // 2

    def zspec(blk0):
        return pl.BlockSpec((None, seq, LANES), lambda bi, p, blk0=blk0: (bi, 0, blk0 + p))

    def full2(shape):
        return pl.BlockSpec(shape, lambda bi, p: (0, 0))

    kern = functools.partial(_ret_kernel, seq=seq)
    return pl.pallas_call(
        kern,
        out_shape=jax.ShapeDtypeStruct((b, seq, W_RET), BF16),
        grid=(b, npair),
        in_specs=[zspec(ZB_RQ), zspec(ZB_RK), zspec(ZB_RV), zspec(ZB_RG),
                  pl.BlockSpec((None, 2, 1, LANES), lambda bi, p: (p, 0, 0, 0)),
                  pl.BlockSpec((1, LANES), lambda bi, p: (0, p)),
                  full2((CHUNK, LANES)), full2((CHUNK, LANES)),
                  pl.BlockSpec((nc, 1, LANES), lambda bi, p: (0, 0, 0)),
                  pl.BlockSpec((nc, 1, LANES), lambda bi, p: (0, 0, 0)),
                  full2((1, LANES)), full2((LANES, LANES)), full2((LANES, LANES)),
                  full2((2 * LANES, LANES))],
        out_specs=pl.BlockSpec((None, seq, LANES), lambda bi, p: (bi, 0, p)),
        scratch_shapes=[pltpu.VMEM((seq, LANES), BF16),
                        pltpu.VMEM((seq, LANES), BF16),
                        pltpu.VMEM((nc, CHUNK, LANES), BF16),
                        pltpu.VMEM((nc, CHUNK, LANES), BF16),
                        pltpu.VMEM((2, CHUNK, LANES), F32),
                        pltpu.VMEM((6, CHUNK, LANES), F32)],
        compiler_params=_cparams(("arbitrary", "arbitrary")),
        name="retention",
    )(z3, z3, z3, z3, rd, norm_w, rope["ca"], rope["sa"], rope["cb"], rope["sb"],
      consts["sgn"], consts["perm"], consts["bdm"], consts["bd2"])


def _na_kernel(q_ref, k_ref, v_ref, bias_ref, out_ref, *, seq):
    rows = seq // GRID_W
    ng = rows // NA_QROWS
    nq = NA_QROWS * GRID_W
    nk = NA_KROWS * GRID_W
    lane_lo = lax.broadcasted_iota(jnp.int32, (nq, LANES), 1) < HEAD_DIM
    ones_aug = jnp.ones((nk, LANES), BF16)

    def body(g, carry):
        t0 = pl.multiple_of(g * nq, nq)
        base = jnp.clip(g * NA_QROWS - NA_ROWS // 2, 0, rows - NA_KROWS)
        k0 = pl.multiple_of(base * GRID_W, GRID_W)
        case = jnp.where(g == 0, 0, jnp.where(g == ng - 1, 2, 1))
        q = q_ref[pl.ds(t0, nq), :]
        kk = k_ref[pl.ds(k0, nk), :]
        v_aug = jnp.concatenate([v_ref[pl.ds(k0, nk), :], ones_aug], axis=1)
        outs = []
        for j in (0, 1):
            qj = jnp.where(lane_lo if j == 0 else jnp.logical_not(lane_lo), q,
                           jnp.zeros_like(q))
            s = _dot_nt(qj, kk) + bias_ref[case, j]
            m = jnp.max(s, axis=1, keepdims=True)
            e = jnp.exp(s - m)
            pv = _dot(e.astype(BF16), v_aug)
            outs.append(pv[:, :LANES] / pv[:, LANES:])
        out_ref[pl.ds(t0, nq), :] = jnp.where(lane_lo, outs[0], outs[1]).astype(BF16)
        return carry

    lax.fori_loop(0, ng, body, 0, unroll=NA_UNROLL)


def _na(z3, bias):
    b, seq, _ = z3.shape
    npair = H_NA // 2
    nq = NA_QROWS * GRID_W
    nk = NA_KROWS * GRID_W

    def zspec(blk0):
        return pl.BlockSpec((None, seq, LANES), lambda bi, p, blk0=blk0: (bi, 0, blk0 + p))

    kern = functools.partial(_na_kernel, seq=seq)
    return pl.pallas_call(
        kern,
        out_shape=jax.ShapeDtypeStruct((b, seq, W_NA), BF16),
        grid=(b, npair),
        in_specs=[zspec(ZB_NQ), zspec(ZB_NK), zspec(ZB_NV),
                  pl.BlockSpec((None, 3, 2, nq, nk), lambda bi, p: (p, 0, 0, 0, 0))],
        out_specs=pl.BlockSpec((None, seq, LANES), lambda bi, p: (bi, 0, p)),
        compiler_params=_cparams(("arbitrary", "arbitrary")),
        name="natten",
    )(z3, z3, z3, bias)


def _na_index_tables(seq):
    rows = seq // GRID_W
    ng = rows // NA_QROWS
    tabs = []
    for g in (0, 1, ng - 1):
        base = int(np.clip(g * NA_QROWS - NA_ROWS // 2, 0, rows - NA_KROWS))
        qr = g * NA_QROWS + np.arange(NA_QROWS)[:, None, None, None]
        qc = np.arange(GRID_W)[None, :, None, None]
        kr = base + np.arange(NA_KROWS)[None, None, :, None]
        kc = np.arange(GRID_W)[None, None, None, :]
        rstart = np.clip(qr - NA_ROWS // 2, 0, rows - NA_ROWS)
        cstart = np.clip(qc - NA_COLS // 2, 0, GRID_W - NA_COLS)
        ok = (kr >= rstart) & (kr < rstart + NA_ROWS) & (kc >= cstart) & (kc < cstart + NA_COLS)
        rel_r = np.clip(kr - qr + NA_ROWS - 1, 0, 2 * NA_ROWS - 2)
        rel_c = np.clip(kc - qc + NA_COLS - 1, 0, 2 * NA_COLS - 2)
        shp = (NA_QROWS * GRID_W, NA_KROWS * GRID_W)
        full = (NA_QROWS, GRID_W, NA_KROWS, GRID_W)
        tabs.append((np.broadcast_to(ok, full).reshape(shp),
                     np.broadcast_to(rel_r, full).reshape(shp),
                     np.broadcast_to(rel_c, full).reshape(shp)))
    ok = np.stack([t[0] for t in tabs])
    rr = np.stack([t[1] for t in tabs])
    rc = np.stack([t[2] for t in tabs])
    return ok, rr, rc


def _na_bias(rpb, seq):
    ok, rr, rc = _na_index_tables(seq)
    h = rpb.shape[0]
    n_r, n_c = 2 * NA_ROWS - 1, 2 * NA_COLS - 1
    full = (3, NA_QROWS, GRID_W, NA_KROWS, GRID_W)
    rr5, rc5 = rr.reshape(full), rc.reshape(full)
    oh_r = (rr5[:, :, 0, :, 0][..., None] == np.arange(n_r)).astype(np.float32)
    oh_c = (rc5[0, 0, :, 0, :][..., None] == np.arange(n_c)).astype(np.float32)
    cols = jnp.einsum("hab,qkb->haqk", rpb.astype(F32), jnp.asarray(oh_c),
                      precision=lax.Precision.HIGHEST)
    bias = jnp.einsum("ciea,haqk->hciqek", jnp.asarray(oh_r), cols,
                      precision=lax.Precision.HIGHEST)
    bias = bias.reshape(h, 3, NA_QROWS * GRID_W, NA_KROWS * GRID_W)
    bias = jnp.where(ok[None], bias, NEG)
    return bias.reshape(h // 2, 2, 3, bias.shape[2], bias.shape[3]).transpose(0, 2, 1, 3, 4)


def _proj_out_kernel(ym_ref, yr_ref, yn_ref, x_ref, wo_ref, g_ref, rw_ref,
                     rb_ref, lst_ref, x2_ref, xn_ref, route_ref, cnt_ref):
    @pl.when(pl.program_id(0) == 0)
    def _():
        cnt_ref[...] = jnp.zeros_like(cnt_ref)

    y_cat = jnp.concatenate([ym_ref[...], yr_ref[...], yn_ref[...]], axis=1)
    x2 = x_ref[...] + _dot(y_cat, wo_ref[...])
    x2_ref[...] = x2
    ms = jnp.mean(x2 * x2, axis=-1, keepdims=True)
    xn = x2 * lax.rsqrt(ms + RMS_EPS) * g_ref[...]
    xn_ref[...] = xn
    xh = xn.astype(BF16)
    xl = (xn - xh.astype(F32)).astype(BF16)
    logits = (_dot(xh, rw_ref[0]) + _dot(xl, rw_ref[0]) + _dot(xh, rw_ref[1])) + rb_ref[...]
    tm = logits.shape[0]
    lane = lax.broadcasted_iota(jnp.int32, (tm, LANES), 1).astype(F32)
    glog = jnp.where(lane < N_GROUPS, logits, NEG)
    gmax = jnp.max(glog, axis=1, keepdims=True)
    grp = jnp.min(jnp.where(glog == gmax, lane, float(LANES)), axis=1, keepdims=True)
    p_grp = 1.0 / jnp.sum(jnp.exp(glog - gmax), axis=1, keepdims=True)
    lo = N_GROUPS + grp * EXPERTS_PER_GROUP
    ein = jnp.where((lane >= lo) & (lane < lo + EXPERTS_PER_GROUP), logits, NEG)
    v1 = jnp.max(ein, axis=1, keepdims=True)
    i1 = jnp.min(jnp.where(ein == v1, lane, float(LANES)), axis=1, keepdims=True)
    ein2 = jnp.where(lane == i1, NEG, ein)
    v2 = jnp.max(ein2, axis=1, keepdims=True)
    i2 = jnp.min(jnp.where(ein2 == v2, lane, float(LANES)), axis=1, keepdims=True)
    e21 = jnp.exp(v2 - v1)
    g1 = p_grp / (1.0 + e21)
    g2 = p_grp * e21 / (1.0 + e21)
    oh1 = jnp.where(lane == i1 - N_GROUPS, 1.0, 0.0)
    oh2 = jnp.where(lane == i2 - N_GROUPS, 1.0, 0.0)
    both = oh1 + oh2
    before = _dot(lst_ref[...], both.astype(BF16)) + cnt_ref[...]
    rk1 = jnp.sum(oh1 * before, axis=1, keepdims=True)
    rk2 = jnp.sum(oh2 * before, axis=1, keepdims=True)
    cnt_ref[...] += jnp.sum(both, axis=0, keepdims=True)
    vals = (i1 - N_GROUPS, i2 - N_GROUPS, g1, g2, rk1, rk2)
    route = jnp.zeros_like(logits)
    for c, val in enumerate(vals):
        route = jnp.where(lane == c, val, route)
    route_ref[...] = route


def _proj_out(ym, yr, yn, x2, wo, g, rw, rb):
    n = x2.shape[0]
    tm = PROJ_TM

    def rows(w):
        return pl.BlockSpec((tm, w), lambda i: (i, 0))

    def full(shape):
        return pl.BlockSpec(shape, lambda i: (0, 0))

    ii = np.arange(tm)
    lstrict = jnp.asarray(ii[None, :] < ii[:, None], BF16)
    return pl.pallas_call(
        _proj_out_kernel,
        out_shape=(jax.ShapeDtypeStruct((n, D_MODEL), F32),
                   jax.ShapeDtypeStruct((n, D_MODEL), F32),
                   jax.ShapeDtypeStruct((n, LANES), F32),
                   jax.ShapeDtypeStruct((1, LANES), F32)),
        grid=(n // tm,),
        in_specs=[rows(W_MLSTM), rows(W_RET), rows(W_NA), rows(D_MODEL),
                  full((D_MODEL, D_MODEL)), full((1, D_MODEL)),
                  pl.BlockSpec((2, D_MODEL, LANES), lambda i: (0, 0, 0)),
                  full((1, LANES)), full((tm, tm))],
        out_specs=(rows(D_MODEL), rows(D_MODEL), rows(LANES), full((1, LANES))),
        compiler_params=_cparams(("arbitrary",)),
        name="proj_out_router",
    )(ym, yr, yn, x2, wo, g, rw, rb, lstrict)


def _dispatch_kernel(pend_ref, cnt_ref, nused_ref, meta_hbm, xn_ref, xs_hbm,
                     meta0, meta1, zero_buf, sem_meta, sem_rows, sem_zero):
    i = pl.program_id(0)
    nsteps = pl.num_programs(0)
    slot = lax.rem(i, 2)
    n_blocks = xs_hbm.shape[0] // MOE_BLK

    metas = (meta0, meta1)

    def meta_copy(step, s):
        return pltpu.make_async_copy(meta_hbm.at[step], metas[s], sem_meta.at[s])

    def zero_copy(blk):
        start = pl.multiple_of(blk * MOE_BLK, MOE_BLK)
        return pltpu.make_async_copy(zero_buf, xs_hbm.at[pl.ds(start, MOE_BLK), :], sem_zero)

    @pl.when(i == 0)
    def _():
        meta_copy(0, 0).start()
        zero_buf[...] = jnp.zeros_like(zero_buf)

        def zstart(e, carry):
            @pl.when(cnt_ref[e] > 0)
            def _():
                zero_copy(pend_ref[e] // MOE_BLK - 1).start()
            return carry

        def zwait(e, carry):
            @pl.when(cnt_ref[e] > 0)
            def _():
                zero_copy(0).wait()
            return carry

        lax.fori_loop(0, N_EXPERTS, zstart, 0)
        lax.fori_loop(nused_ref[0], n_blocks, lambda b, c: (zero_copy(b).start(), c)[1], 0)
        lax.fori_loop(0, N_EXPERTS, zwait, 0)
        lax.fori_loop(nused_ref[0], n_blocks, lambda b, c: (zero_copy(0).wait(), c)[1], 0)

    def row_copy(r, dst):
        return pltpu.make_async_copy(xn_ref.at[pl.ds(r, 1), :], xs_hbm.at[pl.ds(dst, 1), :],
                                     sem_rows)

    for s in (0, 1):
        @pl.when(slot == s)
        def _(s=s):
            meta_copy(i, s).wait()

            @pl.when(i + 1 < nsteps)
            def _():
                meta_copy(i + 1, 1 - s).start()

            def issue(r, carry):
                for kk in range(TOP_K):
                    row_copy(r, metas[s][TOP_K * r + kk]).start()
                return carry

            lax.fori_loop(0, COMB_TM, issue, 0, unroll=DMA_ISSUE_UNROLL)

    def drain(r, carry):
        for kk in range(TOP_K):
            row_copy(r, 0).wait()
        return carry

    lax.fori_loop(0, COMB_TM, drain, 0, unroll=DMA_ISSUE_UNROLL)


def _dispatch(pend, counts, n_used, meta, xn, n_slots):
    n = xn.shape[0]
    tm = COMB_TM
    grid_spec = pltpu.PrefetchScalarGridSpec(
        num_scalar_prefetch=3,
        grid=(n // tm,),
        in_specs=[pl.BlockSpec(memory_space=pl.ANY),
                  pl.BlockSpec((tm, D_MODEL), lambda i, *_: (i, 0))],
        out_specs=pl.BlockSpec(memory_space=pl.ANY),
        scratch_shapes=[pltpu.SMEM((TOP_K * tm,), jnp.int32),
                        pltpu.SMEM((TOP_K * tm,), jnp.int32),
                        pltpu.VMEM((MOE_BLK, D_MODEL), F32),
                        pltpu.SemaphoreType.DMA((2,)),
                        pltpu.SemaphoreType.DMA,
                        pltpu.SemaphoreType.DMA],
    )
    return pl.pallas_call(
        _dispatch_kernel,
        out_shape=jax.ShapeDtypeStruct((n_slots, D_MODEL), F32),
        grid_spec=grid_spec,
        compiler_params=_cparams(("arbitrary",)),
        name="dispatch",
    )(pend, counts, n_used, meta, xn)


def _expert_kernel(be_ref, nused_ref, xs_ref, w1_ref, w3_ref, w2_ref, y_ref, wb1, wb3, wb2):
    i = pl.program_id(0)

    @pl.when((i == 0) | (be_ref[i] != be_ref[jnp.maximum(i - 1, 0)]))
    def _():
        wb1[...] = w1_ref[...].astype(BF16)
        wb3[...] = w3_ref[...].astype(BF16)
        wb2[...] = w2_ref[...].astype(BF16)

    @pl.when(i < nused_ref[0])
    def _():
        xb = xs_ref[...].astype(BF16)
        h1 = _dot(xb, wb1[...])
        h3 = _dot(xb, wb3[...])
        hb = (h1 * _sigmoid(h1) * h3).astype(BF16)
        y_ref[...] = _dot(hb, wb2[...])

    @pl.when(i >= nused_ref[0])
    def _():
        y_ref[...] = jnp.zeros_like(y_ref)


def _experts(block_expert, n_used, xs, w1, w3, w2, layer):
    n_blocks = xs.shape[0] // MOE_BLK

    def wspec(shape):
        return pl.BlockSpec(shape, lambda i, be, nu: (layer, be[i], 0, 0))

    grid_spec = pltpu.PrefetchScalarGridSpec(
        num_scalar_prefetch=2,
        grid=(n_blocks,),
        in_specs=[pl.BlockSpec((MOE_BLK, D_MODEL),
                               lambda i, be, nu: (jnp.minimum(i, nu[0] - 1), 0)),
                  wspec((None, None, D_MODEL, EXPERT_FF)), wspec((None, None, D_MODEL, EXPERT_FF)),
                  wspec((None, None, EXPERT_FF, D_MODEL))],
        out_specs=pl.BlockSpec((MOE_BLK, D_MODEL), lambda i, be, nu: (i, 0)),
        scratch_shapes=[pltpu.VMEM((D_MODEL, EXPERT_FF), BF16),
                        pltpu.VMEM((D_MODEL, EXPERT_FF), BF16),
                        pltpu.VMEM((EXPERT_FF, D_MODEL), BF16)],
    )
    return pl.pallas_call(
        _expert_kernel,
        out_shape=jax.ShapeDtypeStruct((n_blocks * MOE_BLK, D_MODEL), F32),
        grid_spec=grid_spec,
        compiler_params=_cparams(("arbitrary",)),
        name="experts",
    )(block_expert, n_used, xs, w1, w3, w2)


def _combine_kernel(meta_hbm, yb_hbm, x_ref, route_ref, g_ref, out_ref,
                    meta0, meta1, ybuf, sem_meta, sem_rows, *, final_norm):
    i = pl.program_id(0)
    nsteps = pl.num_programs(0)
    slot = lax.rem(i, 2)
    metas = (meta0, meta1)

    def meta_copy(step, s):
        return pltpu.make_async_copy(meta_hbm.at[step], metas[s], sem_meta.at[s])

    def row_copy(src, s, kk, r):
        return pltpu.make_async_copy(yb_hbm.at[pl.ds(src, 1), :],
                                     ybuf.at[s, kk, pl.ds(r, 1), :], sem_rows.at[s])

    def issue_tile(s):
        def issue(r, carry):
            for kk in range(TOP_K):
                row_copy(metas[s][TOP_K * r + kk], s, kk, r).start()
            return carry

        lax.fori_loop(0, COMB_TM, issue, 0, unroll=DMA_ISSUE_UNROLL)

    @pl.when(i == 0)
    def _():
        meta_copy(0, 0).start()
        meta_copy(0, 0).wait()
        issue_tile(0)

        @pl.when(nsteps > 1)
        def _():
            meta_copy(1, 1).start()

    for s in (0, 1):
        @pl.when(slot == s)
        def _(s=s):
            @pl.when(i + 1 < nsteps)
            def _():
                meta_copy(i + 1, 1 - s).wait()
                issue_tile(1 - s)

            @pl.when(i + 2 < nsteps)
            def _():
                meta_copy(i + 2, s).start()

            def drain(r, carry):
                for kk in range(TOP_K):
                    row_copy(0, s, kk, r).wait()
                return carry

            lax.fori_loop(0, COMB_TM, drain, 0, unroll=DMA_ISSUE_UNROLL)
            route = route_ref[...]
            y = x_ref[...]
            for kk in range(TOP_K):
                y = y + route[:, 2 + kk:3 + kk] * ybuf[s, kk]
            if final_norm:
                ms = jnp.mean(y * y, axis=-1, keepdims=True)
                y = y * lax.rsqrt(ms + RMS_EPS) * g_ref[...]
            out_ref[...] = y


def _combine(meta, yb, x2, route, g, final_norm):
    n = x2.shape[0]
    tm = COMB_TM
    kern = functools.partial(_combine_kernel, final_norm=final_norm)
    return pl.pallas_call(
        kern,
        out_shape=jax.ShapeDtypeStruct((n, D_MODEL), F32),
        grid=(n // tm,),
        in_specs=[pl.BlockSpec(memory_space=pl.ANY),
                  pl.BlockSpec(memory_space=pl.ANY),
                  pl.BlockSpec((tm, D_MODEL), lambda i: (i, 0)),
                  pl.BlockSpec((tm, LANES), lambda i: (i, 0)),
                  pl.BlockSpec((1, D_MODEL), lambda i: (0, 0))],
        out_specs=pl.BlockSpec((tm, D_MODEL), lambda i: (i, 0)),
        scratch_shapes=[pltpu.SMEM((TOP_K * tm,), jnp.int32),
                        pltpu.SMEM((TOP_K * tm,), jnp.int32),
                        pltpu.VMEM((2, TOP_K, tm, D_MODEL), F32),
                        pltpu.SemaphoreType.DMA((2,)),
                        pltpu.SemaphoreType.DMA((2,))],
        compiler_params=_cparams(("arbitrary",)),
        name="combine",
    )(meta, yb, x2, route, g)


def _dispatch_layout(route, counts_f, n):
    counts = counts_f[0, :N_EXPERTS].astype(jnp.int32)
    n_blocks = n * TOP_K // MOE_BLK + N_EXPERTS
    padded = (counts + MOE_BLK - 1) // MOE_BLK * MOE_BLK
    pend = jnp.cumsum(padded).astype(jnp.int32)
    pstart = pend - padded
    n_used = pend[-1] // MOE_BLK
    blk_start = jnp.arange(n_blocks, dtype=jnp.int32) * MOE_BLK
    blk_start = jnp.minimum(blk_start, (n_used - 1) * MOE_BLK)
    block_expert = jnp.sum((pend[None, :] <= blk_start[:, None]).astype(jnp.int32), axis=1)
    block_expert = jnp.clip(block_expert, 0, N_EXPERTS - 1).astype(jnp.int32)
    expert = route[:, 0:TOP_K].astype(jnp.int32)
    rank = route[:, 4:4 + TOP_K].astype(jnp.int32)
    seg = jnp.sum(jnp.where(expert[..., None] == jnp.arange(N_EXPERTS, dtype=jnp.int32),
                            pstart, 0), axis=-1)
    meta = (seg + rank).reshape(n // COMB_TM, TOP_K * COMB_TM)
    return block_expert, n_used.reshape(1), pend, counts, meta, n_blocks * MOE_BLK


def _const_tables():
    lane = np.arange(LANES)
    grp = lane // HEAD_DIM
    xc = np.zeros((2, 24, 8 * LANES), np.float32)
    for kk in range(4):
        xc[:, kk, kk * LANES:(kk + 1) * LANES] = 1.0
    for di in range(2):
        for j in range(2):
            for blk, base in ((4 + di, 8), (6 + di, 16)):
                lo = blk * LANES + j * HEAD_DIM
                xc[:, base + 2 * di + j, lo:lo + HEAD_DIM] = 1.0
    ii = np.arange(CHUNK)
    incl_f = (ii[:, None] <= ii[None, :]).astype(np.float32)
    incl_b = (ii[:, None] >= ii[None, :]).astype(np.float32)
    bd = (grp[:, None] == grp[None, :]).astype(np.float32)
    perm = np.zeros((LANES, LANES), np.float32)
    half = HEAD_DIM // 2
    src = (lane // HEAD_DIM) * HEAD_DIM + (lane % HEAD_DIM + half) % HEAD_DIM
    perm[src, lane] = 1.0
    sgn = np.where(lane % HEAD_DIM < half, -1.0, 1.0).astype(np.float32)[None, :]
    return {
        "xc": jnp.asarray(xc.reshape(48, 8 * LANES), BF16),
        "trif": jnp.asarray(np.concatenate([incl_f, incl_f], 0), BF16),
        "trib": jnp.asarray(np.concatenate([incl_b, incl_b], 0), BF16),
        "bdm": jnp.asarray(bd, F32), "bdm2": jnp.asarray(np.concatenate([bd, bd], 1), F32),
        "bd2": jnp.asarray(np.concatenate([bd, bd], 0), BF16),
        "perm": jnp.asarray(perm, BF16), "sgn": jnp.asarray(sgn, F32),
    }


def _rope_tables(seq):
    half = HEAD_DIM // 2
    nc = seq // CHUNK
    inv_freq = ROPE_BASE ** (-np.arange(half, dtype=np.float64) / half)
    freq = inv_freq[np.arange(LANES) % half]
    ang_a = np.arange(CHUNK, dtype=np.float64)[:, None] * freq[None, :]
    ang_b = (np.arange(nc, dtype=np.float64) * CHUNK)[:, None] * freq[None, :]
    return {"ca": jnp.asarray(np.cos(ang_a), F32), "sa": jnp.asarray(np.sin(ang_a), F32),
            "cb": jnp.asarray(np.cos(ang_b)[:, None, :], F32),
            "sb": jnp.asarray(np.sin(ang_b)[:, None, :], F32)}


def _layer_weights(l, w_in, mlstm_gate_bias, ret_decay, w_out, router_group_w, router_group_b,
                   router_expert_w, router_expert_b):
    sizes = (W_MLSTM,) * 4 + (4 * H_MLSTM,) + (W_RET,) * 4 + (W_NA,) * 3
    offs = np.concatenate([[0], np.cumsum(sizes)])
    col = lambda i: w_in[l][:, int(offs[i]):int(offs[i + 1])]
    mq, mk, mv, mo, mg, rq, rk, rv, rg, nq, nk, nv = [col(i) for i in range(12)]
    scale = HEAD_DIM ** -0.5
    w_all = jnp.concatenate([mq, mk, mv, mo, rq, rk * scale, rv, rg, nq * scale, nk, nv],
                            axis=1).astype(BF16)
    sel = np.zeros((GATE_ROWS, 4 * H_MLSTM), np.float32)
    for kind in range(2):
        for p in range(H_MLSTM // 2):
            for k in range(4):
                sel[(kind * (H_MLSTM // 2) + p) * 8 + k,
                    (2 * kind + k // 2) * H_MLSTM + 2 * p + k % 2] = 1.0
    sel = jnp.asarray(sel)
    wg = jnp.dot(sel, mg.T, precision=lax.Precision.HIGHEST).astype(BF16)
    gbias = jnp.dot(sel, mlstm_gate_bias[l].astype(F32).reshape(-1, 1),
                    precision=lax.Precision.HIGHEST)
    rd = ret_decay[l].astype(F32)
    rd = jnp.repeat(rd.reshape(2, H_RET // 2, 2), HEAD_DIM, axis=2)
    rd = rd.transpose(1, 0, 2)[:, :, None, :]
    wo = w_out[l].astype(BF16)
    rw = jnp.concatenate([router_group_w[l], router_expert_w[l],
                          jnp.zeros((D_MODEL, LANES - N_GROUPS - N_EXPERTS), F32)], axis=1)
    rb = jnp.concatenate([router_group_b[l].astype(F32), router_expert_b[l].astype(F32),
                          jnp.zeros((LANES - N_GROUPS - N_EXPERTS,), F32)])[None, :]
    rw_hi = rw.astype(BF16)
    rw2 = jnp.stack([rw_hi, (rw - rw_hi.astype(F32)).astype(BF16)])
    return w_all, wg, gbias, rd, wo, rw2, rb


def _encoder(x, consts, norm_mix, w_in, mlstm_conv_w, mlstm_conv_b, mlstm_gate_bias, mlstm_norm,
             ret_decay, ret_norm, na_rpb, w_out, norm_ffn, router_group_w, router_group_b,
             router_expert_w, router_expert_b, expert_w1, expert_w3, expert_w2, norm_final):
    b, seq, _ = x.shape
    n = b * seq
    depth = w_in.shape[0]
    rope = _rope_tables(seq)
    x2 = x.reshape(n, D_MODEL).astype(F32)
    for l in range(depth):
        w_all, wg, gbias, rd, wo, rw, rb = _layer_weights(
            l, w_in, mlstm_gate_bias, ret_decay, w_out, router_group_w, router_group_b,
            router_expert_w, router_expert_b)
        z, gates_t = _proj_in(x2, norm_mix[l].astype(F32)[None, :], w_all, wg)
        z3 = z.reshape(b, seq, Z_WIDTH)
        y_m = _mlstm(z3, gates_t, gbias, mlstm_conv_w[l].astype(F32),
                     mlstm_conv_b[l].astype(F32)[None, :], mlstm_norm[l].astype(F32)[None, :], consts)
        y_r = _retention(z3, rd, ret_norm[l].astype(F32)[None, :], consts, rope)
        y_n = _na(z3, _na_bias(na_rpb[l], seq))
        x2, xn, route, counts_f = _proj_out(y_m.reshape(n, W_MLSTM), y_r.reshape(n, W_RET),
                                            y_n.reshape(n, W_NA), x2, wo,
                                            norm_ffn[l].astype(F32)[None, :], rw, rb)
        block_expert, n_used, pend, counts, meta, n_slots = _dispatch_layout(route, counts_f, n)
        xs = _dispatch(pend, counts, n_used, meta, xn, n_slots)
        yb = _experts(block_expert, n_used, xs, expert_w1.astype(F32), expert_w3.astype(F32),
                      expert_w2.astype(F32), l)
        x2 = _combine(meta, yb, x2, route, norm_final.astype(F32)[None, :],
                      final_norm=(l == depth - 1))
    return x2.reshape(b, seq, D_MODEL)


def kernel(x_prompt, x_sample, norm_mix, w_in, mlstm_conv_w, mlstm_conv_b, mlstm_gate_bias,
           mlstm_norm, ret_decay, ret_norm, na_rpb, w_out, norm_ffn, router_group_w,
           router_group_b, router_expert_w, router_expert_b, expert_w1, expert_w3, expert_w2,
           norm_final):
    consts = _const_tables()
    weights = (norm_mix, w_in, mlstm_conv_w, mlstm_conv_b, mlstm_gate_bias, mlstm_norm,
               ret_decay, ret_norm, na_rpb, w_out, norm_ffn, router_group_w, router_group_b,
               router_expert_w, router_expert_b, expert_w1, expert_w3, expert_w2, norm_final)
    return (_encoder(x_prompt, consts, *weights), _encoder(x_sample, consts, *weights))
```

```python
import functools

import numpy as np
import jax
import jax.numpy as jnp
from jax import lax
from jax.experimental import pallas as pl
from jax.experimental.pallas import tpu as pltpu

F32 = jnp.float32
BF16 = jnp.bfloat16

D_MODEL = 1024
HEAD_DIM = 64
LANES = 128
CHUNK = 128
H_MLSTM = 6
H_RET = 6
H_NA = 4
W_MLSTM = H_MLSTM * HEAD_DIM
W_RET = H_RET * HEAD_DIM
W_NA = H_NA * HEAD_DIM
Z_WIDTH = 4 * W_MLSTM + 4 * W_RET + 3 * W_NA
GATE_ROWS = 2 * (H_MLSTM // 2) * 8
ROPE_BASE = 10000.0
GRID_W = 64
NA_ROWS = 8
NA_COLS = 16
NA_QROWS = 4
NA_KROWS = NA_QROWS + NA_ROWS
N_GROUPS = 4
EXPERTS_PER_GROUP = 8
N_EXPERTS = 32
TOP_K = 2
EXPERT_FF = 512
RMS_EPS = 1e-6
NEG = -1e30
VMEM_LIMIT = 56 * 1024 * 1024

PROJ_TM = 1024
MOE_BLK = 512
COMB_TM = 512
MLSTM_GATE_UNROLL = 2
MLSTM_PRE_UNROLL = 4
MLSTM_SCAN_UNROLL = 2
MLSTM_OUT_UNROLL = 8
MLSTM_GROUP = 4
RET_UNROLL = 16
RET_GROUP = 8
NA_UNROLL = 4
DMA_ISSUE_UNROLL = 8
GATE_GROUP = 16

ZB_MQ, ZB_MK, ZB_MV, ZB_MO = 0, 3, 6, 9
ZB_RQ, ZB_RK, ZB_RV, ZB_RG = 12, 15, 18, 21
ZB_NQ, ZB_NK, ZB_NV = 24, 26, 28


def _dot(a, b):
    return jnp.dot(a, b, preferred_element_type=F32)


def _dot_nt(a, b):
    return lax.dot_general(a, b, (((1,), (1,)), ((), ())), preferred_element_type=F32)


def _dot_tn(a, b):
    return lax.dot_general(a, b, (((0,), (0,)), ((), ())), preferred_element_type=F32)


def _split(x, axis):
    hi = x.astype(BF16).astype(F32)
    return jnp.concatenate([hi, x - hi], axis=axis).astype(BF16)


def _sigmoid(x):
    return 1.0 / (1.0 + jnp.exp(-x))


def _log_sigmoid(x):
    return -(jnp.maximum(-x, 0.0) + jnp.log(1.0 + jnp.exp(-jnp.abs(x))))


def _cparams(sem):
    return pltpu.CompilerParams(dimension_semantics=sem, vmem_limit_bytes=VMEM_LIMIT)


def _proj_in_kernel(x_ref, g_ref, w_ref, wg_ref, z_ref, gate_ref):
    x = x_ref[...]
    ms = jnp.mean(x * x, axis=-1, keepdims=True)
    hx = (x * lax.rsqrt(ms + RMS_EPS) * g_ref[...]).astype(BF16)
    cw = 768
    for j in range(0, Z_WIDTH, cw):
        z_ref[:, j:j + cw] = _dot(hx, w_ref[:, j:j + cw]).astype(BF16)
    gate_ref[...] = _dot_nt(wg_ref[...], hx)


def _proj_in(x2, g, w, wg):
    n = x2.shape[0]
    return pl.pallas_call(
        _proj_in_kernel,
        out_shape=(jax.ShapeDtypeStruct((n, Z_WIDTH), BF16),
                   jax.ShapeDtypeStruct((GATE_ROWS, n), F32)),
        grid=(n // PROJ_TM,),
        in_specs=[pl.BlockSpec((PROJ_TM, D_MODEL), lambda i: (i, 0)),
                  pl.BlockSpec((1, D_MODEL), lambda i: (0, 0)),
                  pl.BlockSpec((D_MODEL, Z_WIDTH), lambda i: (0, 0)),
                  pl.BlockSpec((GATE_ROWS, D_MODEL), lambda i: (0, 0))],
        out_specs=(pl.BlockSpec((PROJ_TM, Z_WIDTH), lambda i: (i, 0)),
                   pl.BlockSpec((GATE_ROWS, PROJ_TM), lambda i: (0, i))),
        compiler_params=_cparams(("arbitrary",)),
        name="proj_in",
    )(x2, g, w, wg)


def _lane_lt64():
    return lax.broadcasted_iota(jnp.int32, (CHUNK, LANES), 1) < HEAD_DIM


def _tri_mask(d):
    r = lax.broadcasted_iota(jnp.int32, (CHUNK, CHUNK), 0)
    c = lax.broadcasted_iota(jnp.int32, (CHUNK, CHUNK), 1)
    return (c <= r) if d == 0 else (c >= r)


def _head_norm(h, bd2_ref):
    ms = _dot(_split(h * h, 1), bd2_ref[...]) * (1.0 / HEAD_DIM)
    return h * lax.rsqrt(ms + RMS_EPS)


def _mlstm_kernel(q_ref, k_ref, v_ref, o_ref, li_ref, lf_ref, bli_ref, blf_ref, cwq_ref, cwk_ref,
                  cbq_ref, cbk_ref, nw_ref, trif_ref, trib_ref, xc_ref, bdm_ref, bd2_ref,
                  out_ref, qc_ref, kc_ref, kvf_ref, kvb_ref, st_ref, rr_ref, aa_ref, cc_ref,
                  atot_ref, mloc_ref, mpf_ref, mpb_ref, ee_ref, *, seq):
    nc = seq // CHUNK
    lane_lo = _lane_lt64()
    row = lax.broadcasted_iota(jnp.int32, (CHUNK, LANES), 0)
    grow = lax.broadcasted_iota(jnp.int32, (8, CHUNK), 0)
    is_fwd = grow < 2
    ones_aug = jnp.ones((CHUNK, LANES), BF16)
    tri_f = _tri_mask(0)
    tri_b = _tri_mask(1)

    def conv_chunk(src_ref, w_ref, b_ref, c, scale):
        t0 = pl.multiple_of(c * CHUNK, CHUNK)
        x = src_ref[pl.ds(t0, CHUNK), :].astype(F32)
        tp = pl.multiple_of(jnp.maximum(t0 - 16, 0), 16)
        tn = pl.multiple_of(jnp.minimum(t0 + CHUNK, seq - 16), 16)
        prev_last = src_ref[pl.ds(tp, 16), :].astype(F32)[15:16, :]
        next_first = src_ref[pl.ds(tn, 16), :].astype(F32)[0:1, :]
        prev_last = prev_last * jnp.where(c > 0, 1.0, 0.0)
        next_first = next_first * jnp.where(c < nc - 1, 1.0, 0.0)
        xm1 = jnp.where(row == 0, prev_last, pltpu.roll(x, 1, 0))
        xp1 = jnp.where(row == CHUNK - 1, next_first, pltpu.roll(x, CHUNK - 1, 0))
        y = w_ref[0:1, :] * xm1 + w_ref[1:2, :] * x + w_ref[2:3, :] * xp1 + b_ref[...]
        return y * _sigmoid(y) * scale

    def v_aug_at(t0):
        return jnp.concatenate([v_ref[pl.ds(t0, CHUNK), :], ones_aug], axis=1)

    rows_g = GATE_GROUP * 8
    grow_g = lax.broadcasted_iota(jnp.int32, (rows_g, CHUNK), 0)
    glane_g = lax.broadcasted_iota(jnp.int32, (rows_g, CHUNK), 1)
    is_fwd_g = lax.rem(grow_g, 8) < 2

    def stack_chunks(w):
        return jnp.concatenate([w[:, i * CHUNK:(i + 1) * CHUNK] for i in range(GATE_GROUP)],
                               axis=0)

    def gate_body(gi, carry):
        c0 = pl.multiple_of(gi * GATE_GROUP, GATE_GROUP)
        t0 = pl.multiple_of(gi * (GATE_GROUP * CHUNK), GATE_GROUP * CHUNK)
        li = stack_chunks(li_ref[:, pl.ds(t0, GATE_GROUP * CHUNK)] + bli_ref[...])
        lf = stack_chunks(_log_sigmoid(lf_ref[:, pl.ds(t0, GATE_GROUP * CHUNK)] + blf_ref[...]))
        lfs = _split(lf, 1)
        a = jnp.where(is_fwd_g, _dot(lfs, trif_ref[...]), _dot(lfs, trib_ref[...]))
        r = a - li
        xf = -r
        xb = -r
        s = 1
        while s < CHUNK:
            xf = jnp.maximum(xf, jnp.where(glane_g >= s, pltpu.roll(xf, s, 1), NEG))
            xb = jnp.maximum(xb, jnp.where(glane_g < CHUNK - s, pltpu.roll(xb, CHUNK - s, 1), NEG))
            s *= 2
        a_tot = jnp.where(is_fwd_g, a[:, CHUNK - 1:CHUNK], a[:, 0:1])
        w_loc = a_tot - r
        m_loc = jnp.max(w_loc, axis=1, keepdims=True)
        tiles = ((rr_ref, r), (aa_ref, a), (cc_ref, jnp.where(is_fwd_g, xf, xb)),
                 (atot_ref, a_tot), (mloc_ref, jnp.broadcast_to(m_loc, (rows_g, CHUNK))),
                 (ee_ref, jnp.exp(w_loc - m_loc)))
        for ref, val in tiles:
            ref[pl.ds(c0, GATE_GROUP)] = val.reshape(GATE_GROUP, 8, CHUNK)
        return carry

    lax.fori_loop(0, nc // GATE_GROUP, gate_body, 0, unroll=MLSTM_GATE_UNROLL)

    def pre_body(c, carry):
        t0 = pl.multiple_of(c * CHUNK, CHUNK)
        qc_ref[pl.ds(t0, CHUNK), :] = conv_chunk(q_ref, cwq_ref, cbq_ref, c, 1.0).astype(BF16)
        kc = conv_chunk(k_ref, cwk_ref, cbk_ref, c, HEAD_DIM ** -0.5).astype(BF16)
        kc_ref[pl.ds(t0, CHUNK), :] = kc
        k_t = kc.astype(F32).T
        e = ee_ref[c]
        v_aug = v_aug_at(t0)
        for di, kv_ref in enumerate((kvf_ref, kvb_ref)):
            e_rows = jnp.where(row < HEAD_DIM, e[2 * di:2 * di + 1, :], e[2 * di + 1:2 * di + 2, :])
            kv = _dot((k_t * e_rows).astype(BF16), v_aug) * bdm_ref[...]
            kv_ref[c] = kv.astype(BF16)
        return carry

    lax.fori_loop(0, nc, pre_body, 0, unroll=MLSTM_PRE_UNROLL)

    st_ref[...] = jnp.zeros_like(st_ref)

    def lane_pair(t, k0):
        v = jnp.where(lane_lo[0:1, :], t[k0:k0 + 1, :], t[k0 + 1:k0 + 2, :])
        return jnp.concatenate([v, v], axis=1)

    def scan_body(t, m):
        cf = t
        cb = nc - 1 - t
        atot = jnp.where(is_fwd, atot_ref[cf], atot_ref[cb])
        mloc = jnp.where(is_fwd, mloc_ref[cf], mloc_ref[cb])
        mpf_ref[cf] = m
        mpb_ref[cb] = m
        m_new = jnp.maximum(atot + m, mloc)
        s_old = jnp.exp(atot + m - m_new)
        s_new = jnp.exp(mloc - m_new)
        for di, (kv_ref, cidx) in enumerate(((kvf_ref, cf), (kvb_ref, cb))):
            kv = kv_ref[cidx].astype(F32)
            s_st = st_ref[di]
            kv_ref[cidx] = s_st.astype(BF16)
            st_ref[di] = lane_pair(s_old, 2 * di) * s_st + lane_pair(s_new, 2 * di) * kv
        return m_new

    lax.fori_loop(0, nc, scan_body, jnp.zeros((8, CHUNK), F32),
                  unroll=MLSTM_SCAN_UNROLL)

    def out_body(g, carry):
        cs = [g * MLSTM_GROUP + i for i in range(MLSTM_GROUP)]
        t0s = [pl.multiple_of(c * CHUNK, CHUNK) for c in cs]
        qs = [qc_ref[pl.ds(t0, CHUNK), :] for t0 in t0s]
        ks = [kc_ref[pl.ds(t0, CHUNK), :] for t0 in t0s]
        v_augs = [v_aug_at(t0) for t0 in t0s]
        bcs = []
        for c in cs:
            mp = jnp.where(is_fwd, mpf_ref[c], mpb_ref[c])
            u = -jnp.maximum(cc_ref[c], mp)
            cols = jnp.concatenate([u, jnp.exp(mp + u), jnp.exp(u - aa_ref[c])], axis=0)
            bcs.append(_dot_tn(_split(cols, 0), xc_ref[...]))
        scores = [[_dot_nt(jnp.where(lane_lo if j == 0 else jnp.logical_not(lane_lo), q,
                                     jnp.zeros_like(q)), k) for j in (0, 1)]
                  for q, k in zip(qs, ks)]
        q_ss = [[_dot(q, kv_ref[c]) for kv_ref in (kvf_ref, kvb_ref)] for q, c in zip(qs, cs)]
        ps = []
        for c, bc, sc in zip(cs, bcs, scores):
            r = rr_ref[c]
            row_p = []
            for di, tri in enumerate((tri_f, tri_b)):
                for j in (0, 1):
                    kk = 2 * di + j
                    arg = jnp.where(tri, bc[:, kk * LANES:(kk + 1) * LANES] - r[kk:kk + 1, :], NEG)
                    row_p.append((sc[j] * jnp.exp(arg)).astype(BF16))
            ps.append(row_p)
        pvs = [[_dot(p, v_aug) for p in row_p] for row_p, v_aug in zip(ps, v_augs)]
        hs = []
        for bc, pv, q_s in zip(bcs, pvs, q_ss):
            h = None
            for di in (0, 1):
                p0, p1 = pv[2 * di], pv[2 * di + 1]
                sint = bc[:, (4 + di) * LANES:(5 + di) * LANES]
                em = bc[:, (6 + di) * LANES:(7 + di) * LANES]
                num = jnp.where(lane_lo, p0[:, :LANES], p1[:, :LANES]) + sint * q_s[di][:, :LANES]
                den = jnp.where(lane_lo, p0[:, LANES:], p1[:, LANES:]) + sint * q_s[di][:, LANES:]
                hd = num / jnp.maximum(jnp.abs(den), em)
                h = hd if h is None else h + hd
            hs.append(h)
        ms = [_dot(_split(h * h, 1), bd2_ref[...]) for h in hs]
        for h, m, t0 in zip(hs, ms, t0s):
            y = h * lax.rsqrt(m * (1.0 / HEAD_DIM) + RMS_EPS) * nw_ref[...]
            y = y * _sigmoid(o_ref[pl.ds(t0, CHUNK), :].astype(F32))
            out_ref[pl.ds(t0, CHUNK), :] = y.astype(BF16)
        return carry

    lax.fori_loop(0, nc // MLSTM_GROUP, out_body, 0, unroll=MLSTM_OUT_UNROLL // MLSTM_GROUP)


def _mlstm(z3, gates_t, gbias, conv_w, conv_b, norm_w, consts):
    b, seq, _ = z3.shape
    nc = seq // CHUNK
    npair = H_MLSTM // 2

    def zspec(blk0):
        return pl.BlockSpec((None, seq, LANES), lambda bi, p, blk0=blk0: (bi, 0, blk0 + p))

    def full2(shape):
        return pl.BlockSpec(shape, lambda bi, p: (0, 0))

    gate_tile = pltpu.VMEM((nc, 8, CHUNK), F32)
    kern = functools.partial(_mlstm_kernel, seq=seq)
    return pl.pallas_call(
        kern,
        out_shape=jax.ShapeDtypeStruct((b, seq, W_MLSTM), BF16),
        grid=(b, npair),
        in_specs=[zspec(ZB_MQ), zspec(ZB_MK), zspec(ZB_MV), zspec(ZB_MO),
                  pl.BlockSpec((8, seq), lambda bi, p: (p, bi)),
                  pl.BlockSpec((8, seq), lambda bi, p: (npair + p, bi)),
                  pl.BlockSpec((8, 1), lambda bi, p: (p, 0)),
                  pl.BlockSpec((8, 1), lambda bi, p: (npair + p, 0)),
                  pl.BlockSpec((3, LANES), lambda bi, p: (0, p)),
                  pl.BlockSpec((3, LANES), lambda bi, p: (0, npair + p)),
                  pl.BlockSpec((1, LANES), lambda bi, p: (0, p)),
                  pl.BlockSpec((1, LANES), lambda bi, p: (0, npair + p)),
                  pl.BlockSpec((1, LANES), lambda bi, p: (0, p)),
                  full2((2 * CHUNK, CHUNK)), full2((2 * CHUNK, CHUNK)),
                  full2((48, 8 * LANES)),
                  full2((LANES, 2 * LANES)), full2((2 * LANES, LANES))],
        out_specs=pl.BlockSpec((None, seq, LANES), lambda bi, p: (bi, 0, p)),
        scratch_shapes=[pltpu.VMEM((seq, LANES), BF16),
                        pltpu.VMEM((seq, LANES), BF16),
                        pltpu.VMEM((nc, CHUNK, 2 * LANES), BF16),
                        pltpu.VMEM((nc, CHUNK, 2 * LANES), BF16),
                        pltpu.VMEM((2, CHUNK, 2 * LANES), F32)] + [gate_tile] * 8,
        compiler_params=_cparams(("arbitrary", "arbitrary")),
        name="mlstm",
    )(z3, z3, z3, z3, gates_t, gates_t, gbias, gbias, conv_w, conv_w, conv_b, conv_b, norm_w,
      consts["trif"], consts["trib"], consts["xc"], consts["bdm2"], consts["bd2"])


def _ret_kernel(q_ref, k_ref, v_ref, gt_ref, rd_ref, nw_ref, ca_ref, sa_ref, cb_ref, sb_ref,
                sgn_ref, perm_ref, bdm_ref, bd2_ref, out_ref,
                qr_ref, kr_ref, kvf_ref, kvb_ref, st_ref, tab_ref, *, seq):
    nc = seq // CHUNK
    lane_lo = _lane_lt64()
    rowf = lax.broadcasted_iota(jnp.int32, (CHUNK, LANES), 0).astype(F32)
    ri = lax.broadcasted_iota(jnp.int32, (CHUNK, CHUNK), 0)
    ci = lax.broadcasted_iota(jnp.int32, (CHUNK, CHUNK), 1)
    dist = (ri - ci).astype(F32)
    lg_f = -jnp.exp(rd_ref[0])
    lg_b = -jnp.exp(rd_ref[1])
    tab_ref[0] = jnp.exp(lg_f * (CHUNK - 1.0 - rowf))
    tab_ref[1] = jnp.exp(lg_b * rowf)
    tab_ref[2] = jnp.exp(lg_f * (rowf + 1.0))
    tab_ref[3] = jnp.exp(lg_b * (CHUNK - rowf))
    for j in (0, 1):
        lgf_j = lg_f[:, HEAD_DIM * j:HEAD_DIM * j + 1]
        lgb_j = lg_b[:, HEAD_DIM * j:HEAD_DIM * j + 1]
        tab_ref[4 + j] = (jnp.where(dist >= 0.0, jnp.exp(lgf_j * jnp.maximum(dist, 0.0)), 0.0)
                          + jnp.where(dist <= 0.0, jnp.exp(lgb_j * jnp.maximum(-dist, 0.0)), 0.0))
    cdec_f = jnp.exp(lg_f * float(CHUNK))
    cdec_b = jnp.exp(lg_b * float(CHUNK))

    def pre_body(g, carry):
        cs = [g * RET_GROUP + i for i in range(RET_GROUP)]
        t0s = [pl.multiple_of(c * CHUNK, CHUNK) for c in cs]
        xq = [q_ref[pl.ds(t0, CHUNK), :] for t0 in t0s]
        xk = [k_ref[pl.ds(t0, CHUNK), :] for t0 in t0s]
        sq = [_dot(x, perm_ref[...]) for x in xq]
        sk = [_dot(x, perm_ref[...]) for x in xk]
        krs = []
        for i, (c, t0) in enumerate(zip(cs, t0s)):
            cb = cb_ref[c]
            sb = sb_ref[c]
            cos = ca_ref[...] * cb - sa_ref[...] * sb
            sin = (sa_ref[...] * cb + ca_ref[...] * sb) * sgn_ref[...]
            qr_ref[pl.ds(t0, CHUNK), :] = (xq[i].astype(F32) * cos + sq[i] * sin).astype(BF16)
            kr = (xk[i].astype(F32) * cos + sk[i] * sin).astype(BF16)
            kr_ref[pl.ds(t0, CHUNK), :] = kr
            krs.append(kr.astype(F32))
        vs = [v_ref[pl.ds(t0, CHUNK), :] for t0 in t0s]
        kvs = [[_dot_tn((kf * tab_ref[di]).astype(BF16), v) for di in (0, 1)]
               for kf, v in zip(krs, vs)]
        for c, kv in zip(cs, kvs):
            kvf_ref[c] = (kv[0] * bdm_ref[...]).astype(BF16)
            kvb_ref[c] = (kv[1] * bdm_ref[...]).astype(BF16)
        return carry

    lax.fori_loop(0, nc // RET_GROUP, pre_body, 0, unroll=RET_UNROLL // RET_GROUP)

    st_ref[...] = jnp.zeros_like(st_ref)

    def scan_body(t, carry):
        for di, (kv_ref, cidx, cdec) in enumerate(((kvf_ref, t, cdec_f),
                                                   (kvb_ref, nc - 1 - t, cdec_b))):
            kv = kv_ref[cidx].astype(F32)
            s_st = st_ref[di]
            kv_ref[cidx] = s_st.astype(BF16)
            st_ref[di] = cdec * s_st + kv
        return carry

    lax.fori_loop(0, nc, scan_body, 0, unroll=RET_UNROLL)

    def out_body(g, carry):
        cs = [g * RET_GROUP + i for i in range(RET_GROUP)]
        t0s = [pl.multiple_of(c * CHUNK, CHUNK) for c in cs]
        qs = [qr_ref[pl.ds(t0, CHUNK), :] for t0 in t0s]
        ks = [kr_ref[pl.ds(t0, CHUNK), :] for t0 in t0s]
        vs = [v_ref[pl.ds(t0, CHUNK), :] for t0 in t0s]
        scores = [[_dot_nt(jnp.where(lane_lo if j == 0 else jnp.logical_not(lane_lo), q,
                                     jnp.zeros_like(q)), k) for j in (0, 1)]
                  for q, k in zip(qs, ks)]
        inters = []
        for q, c in zip(qs, cs):
            qf = q.astype(F32)
            q_dec = jnp.concatenate([(qf * tab_ref[2]).astype(BF16),
                                     (qf * tab_ref[3]).astype(BF16)], axis=1)
            inters.append(_dot(q_dec, jnp.concatenate([kvf_ref[c], kvb_ref[c]], axis=0)))
        ps = [[(s[j] * tab_ref[4 + j]).astype(BF16) for j in (0, 1)] for s in scores]
        ys = [[_dot(p[j], v) for j in (0, 1)] for p, v in zip(ps, vs)]
        hs = [jnp.where(lane_lo, y[0], y[1]) + it for y, it in zip(ys, inters)]
        ms = [_dot(_split(h * h, 1), bd2_ref[...]) for h in hs]
        for h, m, t0 in zip(hs, ms, t0s):
            y = h * lax.rsqrt(m * (1.0 / HEAD_DIM) + RMS_EPS) * nw_ref[...]
            gt = gt_ref[pl.ds(t0, CHUNK), :].astype(F32)
            out_ref[pl.ds(t0, CHUNK), :] = (y * (gt * _sigmoid(gt))).astype(BF16)
        return carry

    lax.fori_loop(0, nc // RET_GROUP, out_body, 0, unroll=RET_UNROLL // RET_GROUP)


def _retention(z3, rd, norm_w, consts, rope):
    b, seq, _ = z3.shape
    nc = seq // CHUNK
    npair = H_RET // 2

    def zspec(blk0):
        return pl.BlockSpec((None, seq, LANES), lambda bi, p, blk0=blk0: (bi, 0, blk0 + p))

    def full2(shape):
        return pl.BlockSpec(shape, lambda bi, p: (0, 0))

    kern = functools.partial(_ret_kernel, seq=seq)
    return pl.pallas_call(
        kern,
        out_shape=jax.ShapeDtypeStruct((b, seq, W_RET), BF16),
        grid=(b, npair),
        in_specs=[zspec(ZB_RQ), zspec(ZB_RK), zspec(ZB_RV), zspec(ZB_RG),
                  pl.BlockSpec((None, 2, 1, LANES), lambda bi, p: (p, 0, 0, 0)),
                  pl.BlockSpec((1, LANES), lambda bi, p: (0, p)),
                  full2((CHUNK, LANES)), full2((CHUNK, LANES)),
                  pl.BlockSpec((nc, 1, LANES), lambda bi, p: (0, 0, 0)),
                  pl.BlockSpec((nc, 1, LANES), lambda bi, p: (0, 0, 0)),
                  full2((1, LANES)), full2((LANES, LANES)), full2((LANES, LANES)),
                  full2((2 * LANES, LANES))],
        out_specs=pl.BlockSpec((None, seq, LANES), lambda bi, p: (bi, 0, p)),
        scratch_shapes=[pltpu.VMEM((seq, LANES), BF16),
                        pltpu.VMEM((seq, LANES), BF16),
                        pltpu.VMEM((nc, CHUNK, LANES), BF16),
                        pltpu.VMEM((nc, CHUNK, LANES), BF16),
                        pltpu.VMEM((2, CHUNK, LANES), F32),
                        pltpu.VMEM((6, CHUNK, LANES), F32)],
        compiler_params=_cparams(("arbitrary", "arbitrary")),
        name="retention",
    )(z3, z3, z3, z3, rd, norm_w, rope["ca"], rope["sa"], rope["cb"], rope["sb"],
      consts["sgn"], consts["perm"], consts["bdm"], consts["bd2"])


def _na_kernel(q_ref, k_ref, v_ref, bias_ref, out_ref, *, seq):
    rows = seq // GRID_W
    ng = rows // NA_QROWS
    nq = NA_QROWS * GRID_W
    nk = NA_KROWS * GRID_W
    lane_lo = lax.broadcasted_iota(jnp.int32, (nq, LANES), 1) < HEAD_DIM
    ones_aug = jnp.ones((nk, LANES), BF16)

    def body(g, carry):
        t0 = pl.multiple_of(g * nq, nq)
        base = jnp.clip(g * NA_QROWS - NA_ROWS // 2, 0, rows - NA_KROWS)
        k0 = pl.multiple_of(base * GRID_W, GRID_W)
        case = jnp.where(g == 0, 0, jnp.where(g == ng - 1, 2, 1))
        q = q_ref[pl.ds(t0, nq), :]
        kk = k_ref[pl.ds(k0, nk), :]
        v_aug = jnp.concatenate([v_ref[pl.ds(k0, nk), :], ones_aug], axis=1)
        outs = []
        for j in (0, 1):
            qj = jnp.where(lane_lo if j == 0 else jnp.logical_not(lane_lo), q,
                           jnp.zeros_like(q))
            s = _dot_nt(qj, kk) + bias_ref[case, j]
            m = jnp.max(s, axis=1, keepdims=True)
            e = jnp.exp(s - m)
            pv = _dot(e.astype(BF16), v_aug)
            outs.append(pv[:, :LANES] / pv[:, LANES:])
        out_ref[pl.ds(t0, nq), :] = jnp.where(lane_lo, outs[0], outs[1]).astype(BF16)
        return carry

    lax.fori_loop(0, ng, body, 0, unroll=NA_UNROLL)


def _na(z3, bias):
    b, seq, _ = z3.shape
    npair = H_NA // 2
    nq = NA_QROWS * GRID_W
    nk = NA_KROWS * GRID_W

    def zspec(blk0):
        return pl.BlockSpec((None, seq, LANES), lambda bi, p, blk0=blk0: (bi, 0, blk0 + p))

    kern = functools.partial(_na_kernel, seq=seq)
    return pl.pallas_call(
        kern,
        out_shape=jax.ShapeDtypeStruct((b, seq, W_NA), BF16),
        grid=(b, npair),
        in_specs=[zspec(ZB_NQ), zspec(ZB_NK), zspec(ZB_NV),
                  pl.BlockSpec((None, 3, 2, nq, nk), lambda bi, p: (p, 0, 0, 0, 0))],
        out_specs=pl.BlockSpec((None, seq, LANES), lambda bi, p: (bi, 0, p)),
        compiler_params=_cparams(("arbitrary", "arbitrary")),
        name="natten",
    )(z3, z3, z3, bias)


def _na_index_tables(seq):
    rows = seq // GRID_W
    ng = rows // NA_QROWS
    tabs = []
    for g in (0, 1, ng - 1):
        base = int(np.clip(g * NA_QROWS - NA_ROWS // 2, 0, rows - NA_KROWS))
        qr = g * NA_QROWS + np.arange(NA_QROWS)[:, None, None, None]
        qc = np.arange(GRID_W)[None, :, None, None]
        kr = base + np.arange(NA_KROWS)[None, None, :, None]
        kc = np.arange(GRID_W)[None, None, None, :]
        rstart = np.clip(qr - NA_ROWS // 2, 0, rows - NA_ROWS)
        cstart = np.clip(qc - NA_COLS // 2, 0, GRID_W - NA_COLS)
        ok = (kr >= rstart) & (kr < rstart + NA_ROWS) & (kc >= cstart) & (kc < cstart + NA_COLS)
        rel_r = np.clip(kr - qr + NA_ROWS - 1, 0, 2 * NA_ROWS - 2)
        rel_c = np.clip(kc - qc + NA_COLS - 1, 0, 2 * NA_COLS - 2)
        shp = (NA_QROWS * GRID_W, NA_KROWS * GRID_W)
        full = (NA_QROWS, GRID_W, NA_KROWS, GRID_W)
        tabs.append((np.broadcast_to(ok, full).reshape(shp),
                     np.broadcast_to(rel_r, full).reshape(shp),
                     np.broadcast_to(rel_c, full).reshape(shp)))
    ok = np.stack([t[0] for t in tabs])
    rr = np.stack([t[1] for t in tabs])
    rc = np.stack([t[2] for t in tabs])
    return ok, rr, rc


def _na_bias(rpb, seq):
    ok, rr, rc = _na_index_tables(seq)
    h = rpb.shape[0]
    n_r, n_c = 2 * NA_ROWS - 1, 2 * NA_COLS - 1
    full = (3, NA_QROWS, GRID_W, NA_KROWS, GRID_W)
    rr5, rc5 = rr.reshape(full), rc.reshape(full)
    oh_r = (rr5[:, :, 0, :, 0][..., None] == np.arange(n_r)).astype(np.float32)
    oh_c = (rc5[0, 0, :, 0, :][..., None] == np.arange(n_c)).astype(np.float32)
    cols = jnp.einsum("hab,qkb->haqk", rpb.astype(F32), jnp.asarray(oh_c),
                      precision=lax.Precision.HIGHEST)
    bias = jnp.einsum("ciea,haqk->hciqek", jnp.asarray(oh_r), cols,
                      precision=lax.Precision.HIGHEST)
    bias = bias.reshape(h, 3, NA_QROWS * GRID_W, NA_KROWS * GRID_W)
    bias = jnp.where(ok[None], bias, NEG)
    return bias.reshape(h // 2, 2, 3, bias.shape[2], bias.shape[3]).transpose(0, 2, 1, 3, 4)


def _proj_out_kernel(ym_ref, yr_ref, yn_ref, x_ref, wo_ref, g_ref, rw_ref,
                     rb_ref, lst_ref, x2_ref, xn_ref, route_ref, cnt_ref):
    @pl.when(pl.program_id(0) == 0)
    def _():
        cnt_ref[...] = jnp.zeros_like(cnt_ref)

    y_cat = jnp.concatenate([ym_ref[...], yr_ref[...], yn_ref[...]], axis=1)
    x2 = x_ref[...] + _dot(y_cat, wo_ref[...])
    x2_ref[...] = x2
    ms = jnp.mean(x2 * x2, axis=-1, keepdims=True)
    xn = x2 * lax.rsqrt(ms + RMS_EPS) * g_ref[...]
    xn_ref[...] = xn
    xh = xn.astype(BF16)
    xl = (xn - xh.astype(F32)).astype(BF16)
    logits = (_dot(xh, rw_ref[0]) + _dot(xl, rw_ref[0]) + _dot(xh, rw_ref[1])) + rb_ref[...]
    tm = logits.shape[0]
    lane = lax.broadcasted_iota(jnp.int32, (tm, LANES), 1).astype(F32)
    glog = jnp.where(lane < N_GROUPS, logits, NEG)
    gmax = jnp.max(glog, axis=1, keepdims=True)
    grp = jnp.min(jnp.where(glog == gmax, lane, float(LANES)), axis=1, keepdims=True)
    p_grp = 1.0 / jnp.sum(jnp.exp(glog - gmax), axis=1, keepdims=True)
    lo = N_GROUPS + grp * EXPERTS_PER_GROUP
    ein = jnp.where((lane >= lo) & (lane < lo + EXPERTS_PER_GROUP), logits, NEG)
    v1 = jnp.max(ein, axis=1, keepdims=True)
    i1 = jnp.min(jnp.where(ein == v1, lane, float(LANES)), axis=1, keepdims=True)
    ein2 = jnp.where(lane == i1, NEG, ein)
    v2 = jnp.max(ein2, axis=1, keepdims=True)
    i2 = jnp.min(jnp.where(ein2 == v2, lane, float(LANES)), axis=1, keepdims=True)
    e21 = jnp.exp(v2 - v1)
    g1 = p_grp / (1.0 + e21)
    g2 = p_grp * e21 / (1.0 + e21)
    oh1 = jnp.where(lane == i1 - N_GROUPS, 1.0, 0.0)
    oh2 = jnp.where(lane == i2 - N_GROUPS, 1.0, 0.0)
    both = oh1 + oh2
    before = _dot(lst_ref[...], both.astype(BF16)) + cnt_ref[...]
    rk1 = jnp.sum(oh1 * before, axis=1, keepdims=True)
    rk2 = jnp.sum(oh2 * before, axis=1, keepdims=True)
    cnt_ref[...] += jnp.sum(both, axis=0, keepdims=True)
    vals = (i1 - N_GROUPS, i2 - N_GROUPS, g1, g2, rk1, rk2)
    route = jnp.zeros_like(logits)
    for c, val in enumerate(vals):
        route = jnp.where(lane == c, val, route)
    route_ref[...] = route


def _proj_out(ym, yr, yn, x2, wo, g, rw, rb):
    n = x2.shape[0]
    tm = PROJ_TM

    def rows(w):
        return pl.BlockSpec((tm, w), lambda i: (i, 0))

    def full(shape):
        return pl.BlockSpec(shape, lambda i: (0, 0))

    ii = np.arange(tm)
    lstrict = jnp.asarray(ii[None, :] < ii[:, None], BF16)
    return pl.pallas_call(
        _proj_out_kernel,
        out_shape=(jax.ShapeDtypeStruct((n, D_MODEL), F32),
                   jax.ShapeDtypeStruct((n, D_MODEL), F32),
                   jax.ShapeDtypeStruct((n, LANES), F32),
                   jax.ShapeDtypeStruct((1, LANES), F32)),
        grid=(n // tm,),
        in_specs=[rows(W_MLSTM), rows(W_RET), rows(W_NA), rows(D_MODEL),
                  full((D_MODEL, D_MODEL)), full((1, D_MODEL)),
                  pl.BlockSpec((2, D_MODEL, LANES), lambda i: (0, 0, 0)),
                  full((1, LANES)), full((tm, tm))],
        out_specs=(rows(D_MODEL), rows(D_MODEL), rows(LANES), full((1, LANES))),
        compiler_params=_cparams(("arbitrary",)),
        name="proj_out_router",
    )(ym, yr, yn, x2, wo, g, rw, rb, lstrict)


def _dispatch_kernel(pend_ref, cnt_ref, nused_ref, meta_hbm, xn_ref, xs_hbm,
                     meta0, meta1, zero_buf, sem_meta, sem_rows, sem_zero):
    i = pl.program_id(0)
    nsteps = pl.num_programs(0)
    slot = lax.rem(i, 2)
    n_blocks = xs_hbm.shape[0] // MOE_BLK

    metas = (meta0, meta1)

    def meta_copy(step, s):
        return pltpu.make_async_copy(meta_hbm.at[step], metas[s], sem_meta.at[s])

    def zero_copy(blk):
        start = pl.multiple_of(blk * MOE_BLK, MOE_BLK)
        return pltpu.make_async_copy(zero_buf, xs_hbm.at[pl.ds(start, MOE_BLK), :], sem_zero)

    @pl.when(i == 0)
    def _():
        meta_copy(0, 0).start()
        zero_buf[...] = jnp.zeros_like(zero_buf)

        def zstart(e, carry):
            @pl.when(cnt_ref[e] > 0)
            def _():
                zero_copy(pend_ref[e] // MOE_BLK - 1).start()
            return carry

        def zwait(e, carry):
            @pl.when(cnt_ref[e] > 0)
            def _():
                zero_copy(0).wait()
            return carry

        lax.fori_loop(0, N_EXPERTS, zstart, 0)
        lax.fori_loop(nused_ref[0], n_blocks, lambda b, c: (zero_copy(b).start(), c)[1], 0)
        lax.fori_loop(0, N_EXPERTS, zwait, 0)
        lax.fori_loop(nused_ref[0], n_blocks, lambda b, c: (zero_copy(0).wait(), c)[1], 0)

    def row_copy(r, dst):
        return pltpu.make_async_copy(xn_ref.at[pl.ds(r, 1), :], xs_hbm.at[pl.ds(dst, 1), :],
                                     sem_rows)

    for s in (0, 1):
        @pl.when(slot == s)
        def _(s=s):
            meta_copy(i, s).wait()

            @pl.when(i + 1 < nsteps)
            def _():
                meta_copy(i + 1, 1 - s).start()

            def issue(r, carry):
                for kk in range(TOP_K):
                    row_copy(r, metas[s][TOP_K * r + kk]).start()
                return carry

            lax.fori_loop(0, COMB_TM, issue, 0, unroll=DMA_ISSUE_UNROLL)

    def drain(r, carry):
        for kk in range(TOP_K):
            row_copy(r, 0).wait()
        return carry

    lax.fori_loop(0, COMB_TM, drain, 0, unroll=DMA_ISSUE_UNROLL)


def _dispatch(pend, counts, n_used, meta, xn, n_slots):
    n = xn.shape[0]
    tm = COMB_TM
    grid_spec = pltpu.PrefetchScalarGridSpec(
        num_scalar_prefetch=3,
        grid=(n // tm,),
        in_specs=[pl.BlockSpec(memory_space=pl.ANY),
                  pl.BlockSpec((tm, D_MODEL), lambda i, *_: (i, 0))],
        out_specs=pl.BlockSpec(memory_space=pl.ANY),
        scratch_shapes=[pltpu.SMEM((TOP_K * tm,), jnp.int32),
                        pltpu.SMEM((TOP_K * tm,), jnp.int32),
                        pltpu.VMEM((MOE_BLK, D_MODEL), F32),
                        pltpu.SemaphoreType.DMA((2,)),
                        pltpu.SemaphoreType.DMA,
                        pltpu.SemaphoreType.DMA],
    )
    return pl.pallas_call(
        _dispatch_kernel,
        out_shape=jax.ShapeDtypeStruct((n_slots, D_MODEL), F32),
        grid_spec=grid_spec,
        compiler_params=_cparams(("arbitrary",)),
        name="dispatch",
    )(pend, counts, n_used, meta, xn)


def _expert_kernel(be_ref, nused_ref, xs_ref, w1_ref, w3_ref, w2_ref, y_ref, wb1, wb3, wb2):
    i = pl.program_id(0)

    @pl.when((i == 0) | (be_ref[i] != be_ref[jnp.maximum(i - 1, 0)]))
    def _():
        wb1[...] = w1_ref[...].astype(BF16)
        wb3[...] = w3_ref[...].astype(BF16)
        wb2[...] = w2_ref[...].astype(BF16)

    @pl.when(i < nused_ref[0])
    def _():
        xb = xs_ref[...].astype(BF16)
        h1 = _dot(xb, wb1[...])
        h3 = _dot(xb, wb3[...])
        hb = (h1 * _sigmoid(h1) * h3).astype(BF16)
        y_ref[...] = _dot(hb, wb2[...])

    @pl.when(i >= nused_ref[0])
    def _():
        y_ref[...] = jnp.zeros_like(y_ref)


def _experts(block_expert, n_used, xs, w1, w3, w2, layer):
    n_blocks = xs.shape[0] // MOE_BLK

    def wspec(shape):
        return pl.BlockSpec(shape, lambda i, be, nu: (layer, be[i], 0, 0))

    grid_spec = pltpu.PrefetchScalarGridSpec(
        num_scalar_prefetch=2,
        grid=(n_blocks,),
        in_specs=[pl.BlockSpec((MOE_BLK, D_MODEL),
                               lambda i, be, nu: (jnp.minimum(i, nu[0] - 1), 0)),
                  wspec((None, None, D_MODEL, EXPERT_FF)), wspec((None, None, D_MODEL, EXPERT_FF)),
                  wspec((None, None, EXPERT_FF, D_MODEL))],
        out_specs=pl.BlockSpec((MOE_BLK, D_MODEL), lambda i, be, nu: (i, 0)),
        scratch_shapes=[pltpu.VMEM((D_MODEL, EXPERT_FF), BF16),
                        pltpu.VMEM((D_MODEL, EXPERT_FF), BF16),
                        pltpu.VMEM((EXPERT_FF, D_MODEL), BF16)],
    )
    return pl.pallas_call(
        _expert_kernel,
        out_shape=jax.ShapeDtypeStruct((n_blocks * MOE_BLK, D_MODEL), F32),
        grid_spec=grid_spec,
        compiler_params=_cparams(("arbitrary",)),
        name="experts",
    )(block_expert, n_used, xs, w1, w3, w2)


def _combine_kernel(meta_hbm, yb_hbm, x_ref, route_ref, g_ref, out_ref,
                    meta0, meta1, ybuf, sem_meta, sem_rows, *, final_norm):
    i = pl.program_id(0)
    nsteps = pl.num_programs(0)
    slot = lax.rem(i, 2)
    metas = (meta0, meta1)

    def meta_copy(step, s):
        return pltpu.make_async_copy(meta_hbm.at[step], metas[s], sem_meta.at[s])

    def row_copy(src, s, kk, r):
        return pltpu.make_async_copy(yb_hbm.at[pl.ds(src, 1), :],
                                     ybuf.at[s, kk, pl.ds(r, 1), :], sem_rows.at[s])

    def issue_tile(s):
        def issue(r, carry):
            for kk in range(TOP_K):
                row_copy(metas[s][TOP_K * r + kk], s, kk, r).start()
            return carry

        lax.fori_loop(0, COMB_TM, issue, 0, unroll=DMA_ISSUE_UNROLL)

    @pl.when(i == 0)
    def _():
        meta_copy(0, 0).start()
        meta_copy(0, 0).wait()
        issue_tile(0)

        @pl.when(nsteps > 1)
        def _():
            meta_copy(1, 1).start()

    for s in (0, 1):
        @pl.when(slot == s)
        def _(s=s):
            @pl.when(i + 1 < nsteps)
            def _():
                meta_copy(i + 1, 1 - s).wait()
                issue_tile(1 - s)

            @pl.when(i + 2 < nsteps)
            def _():
                meta_copy(i + 2, s).start()

            def drain(r, carry):
                for kk in range(TOP_K):
                    row_copy(0, s, kk, r).wait()
                return carry

            lax.fori_loop(0, COMB_TM, drain, 0, unroll=DMA_ISSUE_UNROLL)
            route = route_ref[...]
            y = x_ref[...]
            for kk in range(TOP_K):
                y = y + route[:, 2 + kk:3 + kk] * ybuf[s, kk]
            if final_norm:
                ms = jnp.mean(y * y, axis=-1, keepdims=True)
                y = y * lax.rsqrt(ms + RMS_EPS) * g_ref[...]
            out_ref[...] = y


def _combine(meta, yb, x2, route, g, final_norm):
    n = x2.shape[0]
    tm = COMB_TM
    kern = functools.partial(_combine_kernel, final_norm=final_norm)
    return pl.pallas_call(
        kern,
        out_shape=jax.ShapeDtypeStruct((n, D_MODEL), F32),
        grid=(n // tm,),
        in_specs=[pl.BlockSpec(memory_space=pl.ANY),
                  pl.BlockSpec(memory_space=pl.ANY),
                  pl.BlockSpec((tm, D_MODEL), lambda i: (i, 0)),
                  pl.BlockSpec((tm, LANES), lambda i: (i, 0)),
                  pl.BlockSpec((1, D_MODEL), lambda i: (0, 0))],
        out_specs=pl.BlockSpec((tm, D_MODEL), lambda i: (i, 0)),
        scratch_shapes=[pltpu.SMEM((TOP_K * tm,), jnp.int32),
                        pltpu.SMEM((TOP_K * tm,), jnp.int32),
                        pltpu.VMEM((2, TOP_K, tm, D_MODEL), F32),
                        pltpu.SemaphoreType.DMA((2,)),
                        pltpu.SemaphoreType.DMA((2,))],
        compiler_params=_cparams(("arbitrary",)),
        name="combine",
    )(meta, yb, x2, route, g)


def _dispatch_layout(route, counts_f, n):
    counts = counts_f[0, :N_EXPERTS].astype(jnp.int32)
    n_blocks = n * TOP_K // MOE_BLK + N_EXPERTS
    padded = (counts + MOE_BLK - 1) // MOE_BLK * MOE_BLK
    pend = jnp.cumsum(padded).astype(jnp.int32)
    pstart = pend - padded
    n_used = pend[-1] // MOE_BLK
    blk_start = jnp.arange(n_blocks, dtype=jnp.int32) * MOE_BLK
    blk_start = jnp.minimum(blk_start, (n_used - 1) * MOE_BLK)
    block_expert = jnp.sum((pend[None, :] <= blk_start[:, None]).astype(jnp.int32), axis=1)
    block_expert = jnp.clip(block_expert, 0, N_EXPERTS - 1).astype(jnp.int32)
    expert = route[:, 0:TOP_K].astype(jnp.int32)
    rank = route[:, 4:4 + TOP_K].astype(jnp.int32)
    seg = jnp.sum(jnp.where(expert[..., None] == jnp.arange(N_EXPERTS, dtype=jnp.int32),
                            pstart, 0), axis=-1)
    meta = (seg + rank).reshape(n // COMB_TM, TOP_K * COMB_TM)
    return block_expert, n_used.reshape(1), pend, counts, meta, n_blocks * MOE_BLK


def _const_tables():
    lane = np.arange(LANES)
    grp = lane // HEAD_DIM
    xc = np.zeros((2, 24, 8 * LANES), np.float32)
    for kk in range(4):
        xc[:, kk, kk * LANES:(kk + 1) * LANES] = 1.0
    for di in range(2):
        for j in range(2):
            for blk, base in ((4 + di, 8), (6 + di, 16)):
                lo = blk * LANES + j * HEAD_DIM
                xc[:, base + 2 * di + j, lo:lo + HEAD_DIM] = 1.0
    ii = np.arange(CHUNK)
    incl_f = (ii[:, None] <= ii[None, :]).astype(np.float32)
    incl_b = (ii[:, None] >= ii[None, :]).astype(np.float32)
    bd = (grp[:, None] == grp[None, :]).astype(np.float32)
    perm = np.zeros((LANES, LANES), np.float32)
    half = HEAD_DIM // 2
    src = (lane // HEAD_DIM) * HEAD_DIM + (lane % HEAD_DIM + half) % HEAD_DIM
    perm[src, lane] = 1.0
    sgn = np.where(lane % HEAD_DIM < half, -1.0, 1.0).astype(np.float32)[None, :]
    return {
        "xc": jnp.asarray(xc.reshape(48, 8 * LANES), BF16),
        "trif": jnp.asarray(np.concatenate([incl_f, incl_f], 0), BF16),
        "trib": jnp.asarray(np.concatenate([incl_b, incl_b], 0), BF16),
        "bdm": jnp.asarray(bd, F32), "bdm2": jnp.asarray(np.concatenate([bd, bd], 1), F32),
        "bd2": jnp.asarray(np.concatenate([bd, bd], 0), BF16),
        "perm": jnp.asarray(perm, BF16), "sgn": jnp.asarray(sgn, F32),
    }


def _rope_tables(seq):
    half = HEAD_DIM // 2
    nc = seq // CHUNK
    inv_freq = ROPE_BASE ** (-np.arange(half, dtype=np.float64) / half)
    freq = inv_freq[np.arange(LANES) % half]
    ang_a = np.arange(CHUNK, dtype=np.float64)[:, None] * freq[None, :]
    ang_b = (np.arange(nc, dtype=np.float64) * CHUNK)[:, None] * freq[None, :]
    return {"ca": jnp.asarray(np.cos(ang_a), F32), "sa": jnp.asarray(np.sin(ang_a), F32),
            "cb": jnp.asarray(np.cos(ang_b)[:, None, :], F32),
            "sb": jnp.asarray(np.sin(ang_b)[:, None, :], F32)}


def _layer_weights(l, w_in, mlstm_gate_bias, ret_decay, w_out, router_group_w, router_group_b,
                   router_expert_w, router_expert_b):
    sizes = (W_MLSTM,) * 4 + (4 * H_MLSTM,) + (W_RET,) * 4 + (W_NA,) * 3
    offs = np.concatenate([[0], np.cumsum(sizes)])
    col = lambda i: w_in[l][:, int(offs[i]):int(offs[i + 1])]
    mq, mk, mv, mo, mg, rq, rk, rv, rg, nq, nk, nv = [col(i) for i in range(12)]
    scale = HEAD_DIM ** -0.5
    w_all = jnp.concatenate([mq, mk, mv, mo, rq, rk * scale, rv, rg, nq * scale, nk, nv],
                            axis=1).astype(BF16)
    sel = np.zeros((GATE_ROWS, 4 * H_MLSTM), np.float32)
    for kind in range(2):
        for p in range(H_MLSTM // 2):
            for k in range(4):
                sel[(kind * (H_MLSTM // 2) + p) * 8 + k,
                    (2 * kind + k // 2) * H_MLSTM + 2 * p + k % 2] = 1.0
    sel = jnp.asarray(sel)
    wg = jnp.dot(sel, mg.T, precision=lax.Precision.HIGHEST).astype(BF16)
    gbias = jnp.dot(sel, mlstm_gate_bias[l].astype(F32).reshape(-1, 1),
                    precision=lax.Precision.HIGHEST)
    rd = ret_decay[l].astype(F32)
    rd = jnp.repeat(rd.reshape(2, H_RET // 2, 2), HEAD_DIM, axis=2)
    rd = rd.transpose(1, 0, 2)[:, :, None, :]
    wo = w_out[l].astype(BF16)
    rw = jnp.concatenate([router_group_w[l], router_expert_w[l],
                          jnp.zeros((D_MODEL, LANES - N_GROUPS - N_EXPERTS), F32)], axis=1)
    rb = jnp.concatenate([router_group_b[l].astype(F32), router_expert_b[l].astype(F32),
                          jnp.zeros((LANES - N_GROUPS - N_EXPERTS,), F32)])[None, :]
    rw_hi = rw.astype(BF16)
    rw2 = jnp.stack([rw_hi, (rw - rw_hi.astype(F32)).astype(BF16)])
    return w_all, wg, gbias, rd, wo, rw2, rb


def _encoder(x, consts, norm_mix, w_in, mlstm_conv_w, mlstm_conv_b, mlstm_gate_bias, mlstm_norm,
             ret_decay, ret_norm, na_rpb, w_out, norm_ffn, router_group_w, router_group_b,
             router_expert_w, router_expert_b, expert_w1, expert_w3, expert_w2, norm_final):
    b, seq, _ = x.shape
    n = b * seq
    depth = w_in.shape[0]
    rope = _rope_tables(seq)
    x2 = x.reshape(n, D_MODEL).astype(F32)
    for l in range(depth):
        w_all, wg, gbias, rd, wo, rw, rb = _layer_weights(
            l, w_in, mlstm_gate_bias, ret_decay, w_out, router_group_w, router_group_b,
            router_expert_w, router_expert_b)
        z, gates_t = _proj_in(x2, norm_mix[l].astype(F32)[None, :], w_all, wg)
        z3 = z.reshape(b, seq, Z_WIDTH)
        y_m = _mlstm(z3, gates_t, gbias, mlstm_conv_w[l].astype(F32),
                     mlstm_conv_b[l].astype(F32)[None, :], mlstm_norm[l].astype(F32)[None, :], consts)
        y_r = _retention(z3, rd, ret_norm[l].astype(F32)[None, :], consts, rope)
        y_n = _na(z3, _na_bias(na_rpb[l], seq))
        x2, xn, route, counts_f = _proj_out(y_m.reshape(n, W_MLSTM), y_r.reshape(n, W_RET),
                                            y_n.reshape(n, W_NA), x2, wo,
                                            norm_ffn[l].astype(F32)[None, :], rw, rb)
        block_expert, n_used, pend, counts, meta, n_slots = _dispatch_layout(route, counts_f, n)
        xs = _dispatch(pend, counts, n_used, meta, xn, n_slots)
        yb = _experts(block_expert, n_used, xs, expert_w1.astype(F32), expert_w3.astype(F32),
                      expert_w2.astype(F32), l)
        x2 = _combine(meta, yb, x2, route, norm_final.astype(F32)[None, :],
                      final_norm=(l == depth - 1))
    return x2.reshape(b, seq, D_MODEL)


def kernel(x_prompt, x_sample, norm_mix, w_in, mlstm_conv_w, mlstm_conv_b, mlstm_gate_bias,
           mlstm_norm, ret_decay, ret_norm, na_rpb, w_out, norm_ffn, router_group_w,
           router_group_b, router_expert_w, router_expert_b, expert_w1, expert_w3, expert_w2,
           norm_final):
    consts = _const_tables()
    weights = (norm_mix, w_in, mlstm_conv_w, mlstm_conv_b, mlstm_gate_bias, mlstm_norm,
               ret_decay, ret_norm, na_rpb, w_out, norm_ffn, router_group_w, router_group_b,
               router_expert_w, router_expert_b, expert_w1, expert_w3, expert_w2, norm_final)
    return (_encoder(x_prompt, consts, *weights), _encoder(x_sample, consts, *weights))
```

```python
import functools

import numpy as np
import jax
import jax.numpy as jnp
from jax import lax
from jax.experimental import pallas as pl
from jax.experimental.pallas import tpu as pltpu

F32 = jnp.float32
BF16 = jnp.bfloat16

D_MODEL = 1024
HEAD_DIM = 64
LANES = 128
CHUNK = 128
H_MLSTM = 6
H_RET = 6
H_NA = 4
W_MLSTM = H_MLSTM * HEAD_DIM
W_RET = H_RET * HEAD_DIM
W_NA = H_NA * HEAD_DIM
Z_WIDTH = 4 * W_MLSTM + 4 * W_RET + 3 * W_NA
GATE_ROWS = 2 * (H_MLSTM // 2) * 8
ROPE_BASE = 10000.0
GRID_W = 64
NA_ROWS = 8
NA_COLS = 16
NA_QROWS = 4
NA_KROWS = NA_QROWS + NA_ROWS
N_GROUPS = 4
EXPERTS_PER_GROUP = 8
N_EXPERTS = 32
TOP_K = 2
EXPERT_FF = 512
RMS_EPS = 1e-6
NEG = -1e30
VMEM_LIMIT = 56 * 1024 * 1024

PROJ_TM = 1024
MOE_BLK = 512
COMB_TM = 512
MLSTM_GATE_UNROLL = 2
MLSTM_PRE_UNROLL = 4
MLSTM_SCAN_UNROLL = 2
MLSTM_OUT_UNROLL = 8
MLSTM_GROUP = 4
RET_UNROLL = 16
RET_GROUP = 8
NA_UNROLL = 4
DMA_ISSUE_UNROLL = 8
GATE_GROUP = 16

ZB_MQ, ZB_MK, ZB_MV, ZB_MO = 0, 3, 6, 9
ZB_RQ, ZB_RK, ZB_RV, ZB_RG = 12, 15, 18, 21
ZB_NQ, ZB_NK, ZB_NV = 24, 26, 28


def _dot(a, b):
    return jnp.dot(a, b, preferred_element_type=F32)


def _dot_nt(a, b):
    return lax.dot_general(a, b, (((1,), (1,)), ((), ())), preferred_element_type=F32)


def _dot_tn(a, b):
    return lax.dot_general(a, b, (((0,), (0,)), ((), ())), preferred_element_type=F32)


def _split(x, axis):
    hi = x.astype(BF16).astype(F32)
    return jnp.concatenate([hi, x - hi], axis=axis).astype(BF16)


def _sigmoid(x):
    return 1.0 / (1.0 + jnp.exp(-x))


def _log_sigmoid(x):
    return -(jnp.maximum(-x, 0.0) + jnp.log(1.0 + jnp.exp(-jnp.abs(x))))


def _cparams(sem):
    return pltpu.CompilerParams(dimension_semantics=sem, vmem_limit_bytes=VMEM_LIMIT)


def _proj_in_kernel(x_ref, g_ref, w_ref, wg_ref, z_ref, gate_ref):
    x = x_ref[...]
    ms = jnp.mean(x * x, axis=-1, keepdims=True)
    hx = (x * lax.rsqrt(ms + RMS_EPS) * g_ref[...]).astype(BF16)
    cw = 768
    for j in range(0, Z_WIDTH, cw):
        z_ref[:, j:j + cw] = _dot(hx, w_ref[:, j:j + cw]).astype(BF16)
    gate_ref[...] = _dot_nt(wg_ref[...], hx)


def _proj_in(x2, g, w, wg):
    n = x2.shape[0]
    return pl.pallas_call(
        _proj_in_kernel,
        out_shape=(jax.ShapeDtypeStruct((n, Z_WIDTH), BF16),
                   jax.ShapeDtypeStruct((GATE_ROWS, n), F32)),
        grid=(n // PROJ_TM,),
        in_specs=[pl.BlockSpec((PROJ_TM, D_MODEL), lambda i: (i, 0)),
                  pl.BlockSpec((1, D_MODEL), lambda i: (0, 0)),
                  pl.BlockSpec((D_MODEL, Z_WIDTH), lambda i: (0, 0)),
                  pl.BlockSpec((GATE_ROWS, D_MODEL), lambda i: (0, 0))],
        out_specs=(pl.BlockSpec((PROJ_TM, Z_WIDTH), lambda i: (i, 0)),
                   pl.BlockSpec((GATE_ROWS, PROJ_TM), lambda i: (0, i))),
        compiler_params=_cparams(("arbitrary",)),
        name="proj_in",
    )(x2, g, w, wg)


def _lane_lt64():
    return lax.broadcasted_iota(jnp.int32, (CHUNK, LANES), 1) < HEAD_DIM


def _tri_mask(d):
    r = lax.broadcasted_iota(jnp.int32, (CHUNK, CHUNK), 0)
    c = lax.broadcasted_iota(jnp.int32, (CHUNK, CHUNK), 1)
    return (c <= r) if d == 0 else (c >= r)


def _head_norm(h, bd2_ref):
    ms = _dot(_split(h * h, 1), bd2_ref[...]) * (1.0 / HEAD_DIM)
    return h * lax.rsqrt(ms + RMS_EPS)


def _mlstm_kernel(q_ref, k_ref, v_ref, o_ref, li_ref, lf_ref, bli_ref, blf_ref, cwq_ref, cwk_ref,
                  cbq_ref, cbk_ref, nw_ref, trif_ref, trib_ref, xc_ref, bdm_ref, bd2_ref,
                  out_ref, qc_ref, kc_ref, kvf_ref, kvb_ref, st_ref, rr_ref, aa_ref, cc_ref,
                  atot_ref, mloc_ref, mpf_ref, mpb_ref, ee_ref, *, seq):
    nc = seq // CHUNK
    lane_lo = _lane_lt64()
    row = lax.broadcasted_iota(jnp.int32, (CHUNK, LANES), 0)
    grow = lax.broadcasted_iota(jnp.int32, (8, CHUNK), 0)
    is_fwd = grow < 2
    ones_aug = jnp.ones((CHUNK, LANES), BF16)
    tri_f = _tri_mask(0)
    tri_b = _tri_mask(1)

    def conv_chunk(src_ref, w_ref, b_ref, c, scale):
        t0 = pl.multiple_of(c * CHUNK, CHUNK)
        x = src_ref[pl.ds(t0, CHUNK), :].astype(F32)
        tp = pl.multiple_of(jnp.maximum(t0 - 16, 0), 16)
        tn = pl.multiple_of(jnp.minimum(t0 + CHUNK, seq - 16), 16)
        prev_last = src_ref[pl.ds(tp, 16), :].astype(F32)[15:16, :]
        next_first = src_ref[pl.ds(tn, 16), :].astype(F32)[0:1, :]
        prev_last = prev_last * jnp.where(c > 0, 1.0, 0.0)
        next_first = next_first * jnp.where(c < nc - 1, 1.0, 0.0)
        xm1 = jnp.where(row == 0, prev_last, pltpu.roll(x, 1, 0))
        xp1 = jnp.where(row == CHUNK - 1, next_first, pltpu.roll(x, CHUNK - 1, 0))
        y = w_ref[0:1, :] * xm1 + w_ref[1:2, :] * x + w_ref[2:3, :] * xp1 + b_ref[...]
        return y * _sigmoid(y) * scale

    def v_aug_at(t0):
        return jnp.concatenate([v_ref[pl.ds(t0, CHUNK), :], ones_aug], axis=1)

    rows_g = GATE_GROUP * 8
    grow_g = lax.broadcasted_iota(jnp.int32, (rows_g, CHUNK), 0)
    glane_g = lax.broadcasted_iota(jnp.int32, (rows_g, CHUNK), 1)
    is_fwd_g = lax.rem(grow_g, 8) < 2

    def stack_chunks(w):
        return jnp.concatenate([w[:, i * CHUNK:(i + 1) * CHUNK] for i in range(GATE_GROUP)],
                               axis=0)

    def gate_body(gi, carry):
        c0 = pl.multiple_of(gi * GATE_GROUP, GATE_GROUP)
        t0 = pl.multiple_of(gi * (GATE_GROUP * CHUNK), GATE_GROUP * CHUNK)
        li = stack_chunks(li_ref[:, pl.ds(t0, GATE_GROUP * CHUNK)] + bli_ref[...])
        lf = stack_chunks(_log_sigmoid(lf_ref[:, pl.ds(t0, GATE_GROUP * CHUNK)] + blf_ref[...]))
        lfs = _split(lf, 1)
        a = jnp.where(is_fwd_g, _dot(lfs, trif_ref[...]), _dot(lfs, trib_ref[...]))
        r = a - li
        xf = -r
        xb = -r
        s = 1
        while s < CHUNK:
            xf = jnp.maximum(xf, jnp.where(glane_g >= s, pltpu.roll(xf, s, 1), NEG))
            xb = jnp.maximum(xb, jnp.where(glane_g < CHUNK - s, pltpu.roll(xb, CHUNK - s, 1), NEG))
            s *= 2
        a_tot = jnp.where(is_fwd_g, a[:, CHUNK - 1:CHUNK], a[:, 0:1])
        w_loc = a_tot - r
        m_loc = jnp.max(w_loc, axis=1, keepdims=True)
        tiles = ((rr_ref, r), (aa_ref, a), (cc_ref, jnp.where(is_fwd_g, xf, xb)),
                 (atot_ref, a_tot), (mloc_ref, jnp.broadcast_to(m_loc, (rows_g, CHUNK))),
                 (ee_ref, jnp.exp(w_loc - m_loc)))
        for ref, val in tiles:
            ref[pl.ds(c0, GATE_GROUP)] = val.reshape(GATE_GROUP, 8, CHUNK)
        return carry

    lax.fori_loop(0, nc // GATE_GROUP, gate_body, 0, unroll=MLSTM_GATE_UNROLL)

    def pre_body(c, carry):
        t0 = pl.multiple_of(c * CHUNK, CHUNK)
        qc_ref[pl.ds(t0, CHUNK), :] = conv_chunk(q_ref, cwq_ref, cbq_ref, c, 1.0).astype(BF16)
        kc = conv_chunk(k_ref, cwk_ref, cbk_ref, c, HEAD_DIM ** -0.5).astype(BF16)
        kc_ref[pl.ds(t0, CHUNK), :] = kc
        k_t = kc.astype(F32).T
        e = ee_ref[c]
        v_aug = v_aug_at(t0)
        for di, kv_ref in enumerate((kvf_ref, kvb_ref)):
            e_rows = jnp.where(row < HEAD_DIM, e[2 * di:2 * di + 1, :], e[2 * di + 1:2 * di + 2, :])
            kv = _dot((k_t * e_rows).astype(BF16), v_aug) * bdm_ref[...]
            kv_ref[c] = kv.astype(BF16)
        return carry

    lax.fori_loop(0, nc, pre_body, 0, unroll=MLSTM_PRE_UNROLL)

    st_ref[...] = jnp.zeros_like(st_ref)

    def lane_pair(t, k0):
        v = jnp.where(lane_lo[0:1, :], t[k0:k0 + 1, :], t[k0 + 1:k0 + 2, :])
        return jnp.concatenate([v, v], axis=1)

    def scan_body(t, m):
        cf = t
        cb = nc - 1 - t
        atot = jnp.where(is_fwd, atot_ref[cf], atot_ref[cb])
        mloc = jnp.where(is_fwd, mloc_ref[cf], mloc_ref[cb])
        mpf_ref[cf] = m
        mpb_ref[cb] = m
        m_new = jnp.maximum(atot + m, mloc)
        s_old = jnp.exp(atot + m - m_new)
        s_new = jnp.exp(mloc - m_new)
        for di, (kv_ref, cidx) in enumerate(((kvf_ref, cf), (kvb_ref, cb))):
            kv = kv_ref[cidx].astype(F32)
            s_st = st_ref[di]
            kv_ref[cidx] = s_st.astype(BF16)
            st_ref[di] = lane_pair(s_old, 2 * di) * s_st + lane_pair(s_new, 2 * di) * kv
        return m_new

    lax.fori_loop(0, nc, scan_body, jnp.zeros((8, CHUNK), F32),
                  unroll=MLSTM_SCAN_UNROLL)

    def out_body(g, carry):
        cs = [g * MLSTM_GROUP + i for i in range(MLSTM_GROUP)]
        t0s = [pl.multiple_of(c * CHUNK, CHUNK) for c in cs]
        qs = [qc_ref[pl.ds(t0, CHUNK), :] for t0 in t0s]
        ks = [kc_ref[pl.ds(t0, CHUNK), :] for t0 in t0s]
        v_augs = [v_aug_at(t0) for t0 in t0s]
        bcs = []
        for c in cs:
            mp = jnp.where(is_fwd, mpf_ref[c], mpb_ref[c])
            u = -jnp.maximum(cc_ref[c], mp)
            cols = jnp.concatenate([u, jnp.exp(mp + u), jnp.exp(u - aa_ref[c])], axis=0)
            bcs.append(_dot_tn(_split(cols, 0), xc_ref[...]))
        scores = [[_dot_nt(jnp.where(lane_lo if j == 0 else jnp.logical_not(lane_lo), q,
                                     jnp.zeros_like(q)), k) for j in (0, 1)]
                  for q, k in zip(qs, ks)]
        q_ss = [[_dot(q, kv_ref[c]) for kv_ref in (kvf_ref, kvb_ref)] for q, c in zip(qs, cs)]
        ps = []
        for c, bc, sc in zip(cs, bcs, scores):
            r = rr_ref[c]
            row_p = []
            for di, tri in enumerate((tri_f, tri_b)):
                for j in (0, 1):
                    kk = 2 * di + j
                    arg = jnp.where(tri, bc[:, kk * LANES:(kk + 1) * LANES] - r[kk:kk + 1, :], NEG)
                    row_p.append((sc[j] * jnp.exp(arg)).astype(BF16))
            ps.append(row_p)
        pvs = [[_dot(p, v_aug) for p in row_p] for row_p, v_aug in zip(ps, v_augs)]
        hs = []
        for bc, pv, q_s in zip(bcs, pvs, q_ss):
            h = None
            for di in (0, 1):
                p0, p1 = pv[2 * di], pv[2 * di + 1]
                sint = bc[:, (4 + di) * LANES:(5 + di) * LANES]
                em = bc[:, (6 + di) * LANES:(7 + di) * LANES]
                num = jnp.where(lane_lo, p0[:, :LANES], p1[:, :LANES]) + sint * q_s[di][:, :LANES]
                den = jnp.where(lane_lo, p0[:, LANES:], p1[:, LANES:]) + sint * q_s[di][:, LANES:]
                hd = num / jnp.maximum(jnp.abs(den), em)
                h = hd if h is None else h + hd
            hs.append(h)
        ms = [_dot(_split(h * h, 1), bd2_ref[...]) for h in hs]
        for h, m, t0 in zip(hs, ms, t0s):
            y = h * lax.rsqrt(m * (1.0 / HEAD_DIM) + RMS_EPS) * nw_ref[...]
            y = y * _sigmoid(o_ref[pl.ds(t0, CHUNK), :].astype(F32))
            out_ref[pl.ds(t0, CHUNK), :] = y.astype(BF16)
        return carry

    lax.fori_loop(0, nc // MLSTM_GROUP, out_body, 0, unroll=MLSTM_OUT_UNROLL // MLSTM_GROUP)


def _mlstm(z3, gates_t, gbias, conv_w, conv_b, norm_w, consts):
    b, seq, _ = z3.shape
    nc = seq // CHUNK
    npair = H_MLSTM // 2

    def zspec(blk0):
        return pl.BlockSpec((None, seq, LANES), lambda bi, p, blk0=blk0: (bi, 0, blk0 + p))

    def full2(shape):
        return pl.BlockSpec(shape, lambda bi, p: (0, 0))

    gate_tile = pltpu.VMEM((nc, 8, CHUNK), F32)
    kern = functools.partial(_mlstm_kernel, seq=seq)
    return pl.pallas_call(
        kern,
        out_shape=jax.ShapeDtypeStruct((b, seq, W_MLSTM), BF16),
        grid=(b, npair),
        in_specs=[zspec(ZB_MQ), zspec(ZB_MK), zspec(ZB_MV), zspec(ZB_MO),
                  pl.BlockSpec((8, seq), lambda bi, p: (p, bi)),
                  pl.BlockSpec((8, seq), lambda bi, p: (npair + p, bi)),
                  pl.BlockSpec((8, 1), lambda bi, p: (p, 0)),
                  pl.BlockSpec((8, 1), lambda bi, p: (npair + p, 0)),
                  pl.BlockSpec((3, LANES), lambda bi, p: (0, p)),
                  pl.BlockSpec((3, LANES), lambda bi, p: (0, npair + p)),
                  pl.BlockSpec((1, LANES), lambda bi, p: (0, p)),
                  pl.BlockSpec((1, LANES), lambda bi, p: (0, npair + p)),
                  pl.BlockSpec((1, LANES), lambda bi, p: (0, p)),
                  full2((2 * CHUNK, CHUNK)), full2((2 * CHUNK, CHUNK)),
                  full2((48, 8 * LANES)),
                  full2((LANES, 2 * LANES)), full2((2 * LANES, LANES))],
        out_specs=pl.BlockSpec((None, seq, LANES), lambda bi, p: (bi, 0, p)),
        scratch_shapes=[pltpu.VMEM((seq, LANES), BF16),
                        pltpu.VMEM((seq, LANES), BF16),
                        pltpu.VMEM((nc, CHUNK, 2 * LANES), BF16),
                        pltpu.VMEM((nc, CHUNK, 2 * LANES), BF16),
                        pltpu.VMEM((2, CHUNK, 2 * LANES), F32)] + [gate_tile] * 8,
        compiler_params=_cparams(("arbitrary", "arbitrary")),
        name="mlstm",
    )(z3, z3, z3, z3, gates_t, gates_t, gbias, gbias, conv_w, conv_w, conv_b, conv_b, norm_w,
      consts["trif"], consts["trib"], consts["xc"], consts["bdm2"], consts["bd2"])


def _ret_kernel(q_ref, k_ref, v_ref, gt_ref, rd_ref, nw_ref, ca_ref, sa_ref, cb_ref, sb_ref,
                sgn_ref, perm_ref, bdm_ref, bd2_ref, out_ref,
                qr_ref, kr_ref, kvf_ref, kvb_ref, st_ref, tab_ref, *, seq):
    nc = seq // CHUNK
    lane_lo = _lane_lt64()
    rowf = lax.broadcasted_iota(jnp.int32, (CHUNK, LANES), 0).astype(F32)
    ri = lax.broadcasted_iota(jnp.int32, (CHUNK, CHUNK), 0)
    ci = lax.broadcasted_iota(jnp.int32, (CHUNK, CHUNK), 1)
    dist = (ri - ci).astype(F32)
    lg_f = -jnp.exp(rd_ref[0])
    lg_b = -jnp.exp(rd_ref[1])
    tab_ref[0] = jnp.exp(lg_f * (CHUNK - 1.0 - rowf))
    tab_ref[1] = jnp.exp(lg_b * rowf)
    tab_ref[2] = jnp.exp(lg_f * (rowf + 1.0))
    tab_ref[3] = jnp.exp(lg_b * (CHUNK - rowf))
    for j in (0, 1):
        lgf_j = lg_f[:, HEAD_DIM * j:HEAD_DIM * j + 1]
        lgb_j = lg_b[:, HEAD_DIM * j:HEAD_DIM * j + 1]
        tab_ref[4 + j] = (jnp.where(dist >= 0.0, jnp.exp(lgf_j * jnp.maximum(dist, 0.0)), 0.0)
                          + jnp.where(dist <= 0.0, jnp.exp(lgb_j * jnp.maximum(-dist, 0.0)), 0.0))
    cdec_f = jnp.exp(lg_f * float(CHUNK))
    cdec_b = jnp.exp(lg_b * float(CHUNK))

    def pre_body(g, carry):
        cs = [g * RET_GROUP + i for i in range(RET_GROUP)]
        t0s = [pl.multiple_of(c * CHUNK, CHUNK) for c in cs]
        xq = [q_ref[pl.ds(t0, CHUNK), :] for t0 in t0s]
        xk = [k_ref[pl.ds(t0, CHUNK), :] for t0 in t0s]
        sq = [_dot(x, perm_ref[...]) for x in xq]
        sk = [_dot(x, perm_ref[...]) for x in xk]
        krs = []
        for i, (c, t0) in enumerate(zip(cs, t0s)):
            cb = cb_ref[c]
            sb = sb_ref[c]
            cos = ca_ref[...] * cb - sa_ref[...] * sb
            sin = (sa_ref[...] * cb + ca_ref[...] * sb) * sgn_ref[...]
            qr_ref[pl.ds(t0, CHUNK), :] = (xq[i].astype(F32) * cos + sq[i] * sin).astype(BF16)
            kr = (xk[i].astype(F32) * cos + sk[i] * sin).astype(BF16)
            kr_ref[pl.ds(t0, CHUNK), :] = kr
            krs.append(kr.astype(F32))
        vs = [v_ref[pl.ds(t0, CHUNK), :] for t0 in t0s]
        kvs = [[_dot_tn((kf * tab_ref[di]).astype(BF16), v) for di in (0, 1)]
               for kf, v in zip(krs, vs)]
        for c, kv in zip(cs, kvs):
            kvf_ref[c] = (kv[0] * bdm_ref[...]).astype(BF16)
            kvb_ref[c] = (kv[1] * bdm_ref[...]).astype(BF16)
        return carry

    lax.fori_loop(0, nc // RET_GROUP, pre_body, 0, unroll=RET_UNROLL // RET_GROUP)

    st_ref[...] = jnp.zeros_like(st_ref)

    def scan_body(t, carry):
        for di, (kv_ref, cidx, cdec) in enumerate(((kvf_ref, t, cdec_f),
                                                   (kvb_ref, nc - 1 - t, cdec_b))):
            kv = kv_ref[cidx].astype(F32)
            s_st = st_ref[di]
            kv_ref[cidx] = s_st.astype(BF16)
            st_ref[di] = cdec * s_st + kv
        return carry

    lax.fori_loop(0, nc, scan_body, 0, unroll=RET_UNROLL)

    def out_body(g, carry):
        cs = [g * RET_GROUP + i for i in range(RET_GROUP)]
        t0s = [pl.multiple_of(c * CHUNK, CHUNK) for c in cs]
        qs = [qr_ref[pl.ds(t0, CHUNK), :] for t0 in t0s]
        ks = [kr_ref[pl.ds(t0, CHUNK), :] for t0 in t0s]
        vs = [v_ref[pl.ds(t0, CHUNK), :] for t0 in t0s]
        scores = [[_dot_nt(jnp.where(lane_lo if j == 0 else jnp.logical_not(lane_lo), q,
                                     jnp.zeros_like(q)), k) for j in (0, 1)]
                  for q, k in zip(qs, ks)]
        inters = []
        for q, c in zip(qs, cs):
            qf = q.astype(F32)
            q_dec = jnp.concatenate([(qf * tab_ref[2]).astype(BF16),
                                     (qf * tab_ref[3]).astype(BF16)], axis=1)
            inters.append(_dot(q_dec, jnp.concatenate([kvf_ref[c], kvb_ref[c]], axis=0)))
        ps = [[(s[j] * tab_ref[4 + j]).astype(BF16) for j in (0, 1)] for s in scores]
        ys = [[_dot(p[j], v) for j in (0, 1)] for p, v in zip(ps, vs)]
        hs = [jnp.where(lane_lo, y[0], y[1]) + it for y, it in zip(ys, inters)]
        ms = [_dot(_split(h * h, 1), bd2_ref[...]) for h in hs]
        for h, m, t0 in zip(hs, ms, t0s):
            y = h * lax.rsqrt(m * (1.0 / HEAD_DIM) + RMS_EPS) * nw_ref[...]
            gt = gt_ref[pl.ds(t0, CHUNK), :].astype(F32)
            out_ref[pl.ds(t0, CHUNK), :] = (y * (gt * _sigmoid(gt))).astype(BF16)
        return carry

    lax.fori_loop(0, nc // RET_GROUP, out_body, 0, unroll=RET_UNROLL // RET_GROUP)


def _retention(z3, rd, norm_w, consts, rope):
    b, seq, _ = z3.shape
    nc = seq // CHUNK
    npair = H_RET // 2

    def zspec(blk0):
        return pl.BlockSpec((None, seq, LANES), lambda bi, p, blk0=blk0: (bi, 0, blk0 + p))

    def full2(shape):
        return pl.BlockSpec(shape, lambda bi, p: (0, 0))

    kern = functools.partial(_ret_kernel, seq=seq)
    return pl.pallas_call(
        kern,
        out_shape=jax.ShapeDtypeStruct((b, seq, W_RET), BF16),
        grid=(b, npair),
        in_specs=[zspec(ZB_RQ), zspec(ZB_RK), zspec(ZB_RV), zspec(ZB_RG),
                  pl.BlockSpec((None, 2, 1, LANES), lambda bi, p: (p, 0, 0, 0)),
                  pl.BlockSpec((1, LANES), lambda bi, p: (0, p)),
                  full2((CHUNK, LANES)), full2((CHUNK, LANES)),
                  pl.BlockSpec((nc, 1, LANES), lambda bi, p: (0, 0, 0)),
                  pl.BlockSpec((nc, 1, LANES), lambda bi, p: (0, 0, 0)),
                  full2((1, LANES)), full2((LANES, LANES)), full2((LANES, LANES)),
                  full2((2 * LANES, LANES))],
        out_specs=pl.BlockSpec((None, seq, LANES), lambda bi, p: (bi, 0, p)),
        scratch_shapes=[pltpu.VMEM((seq, LANES), BF16),
                        pltpu.VMEM((seq, LANES), BF16),
                        pltpu.VMEM((nc, CHUNK, LANES), BF16),
                        pltpu.VMEM((nc, CHUNK, LANES), BF16),
                        pltpu.VMEM((2, CHUNK, LANES), F32),
                        pltpu.VMEM((6, CHUNK, LANES), F32)],
        compiler_params=_cparams(("arbitrary", "arbitrary")),
        name="retention",
    )(z3, z3, z3, z3, rd, norm_w, rope["ca"], rope["sa"], rope["cb"], rope["sb"],
      consts["sgn"], consts["perm"], consts["bdm"], consts["bd2"])


def _na_kernel(q_ref, k_ref, v_ref, bias_ref, out_ref, *, seq):
    rows = seq // GRID_W
    ng = rows // NA_QROWS
    nq = NA_QROWS * GRID_W
    nk = NA_KROWS * GRID_W
    lane_lo = lax.broadcasted_iota(jnp.int32, (nq, LANES), 1) < HEAD_DIM
    ones_aug = jnp.ones((nk, LANES), BF16)

    def body(g, carry):
        t0 = pl.multiple_of(g * nq, nq)
        base = jnp.clip(g * NA_QROWS - NA_ROWS // 2, 0, rows - NA_KROWS)
        k0 = pl.multiple_of(base * GRID_W, GRID_W)
        case = jnp.where(g == 0, 0, jnp.where(g == ng - 1, 2, 1))
        q = q_ref[pl.ds(t0, nq), :]
        kk = k_ref[pl.ds(k0, nk), :]
        v_aug = jnp.concatenate([v_ref[pl.ds(k0, nk), :], ones_aug], axis=1)
        outs = []
        for j in (0, 1):
            qj = jnp.where(lane_lo if j == 0 else jnp.logical_not(lane_lo), q,
                           jnp.zeros_like(q))
            s = _dot_nt(qj, kk) + bias_ref[case, j]
            m = jnp.max(s, axis=1, keepdims=True)
            e = jnp.exp(s - m)
            pv = _dot(e.astype(BF16), v_aug)
            outs.append(pv[:, :LANES] / pv[:, LANES:])
        out_ref[pl.ds(t0, nq), :] = jnp.where(lane_lo, outs[0], outs[1]).astype(BF16)
        return carry

    lax.fori_loop(0, ng, body, 0, unroll=NA_UNROLL)


def _na(z3, bias):
    b, seq, _ = z3.shape
    npair = H_NA // 2
    nq = NA_QROWS * GRID_W
    nk = NA_KROWS * GRID_W

    def zspec(blk0):
        return pl.BlockSpec((None, seq, LANES), lambda bi, p, blk0=blk0: (bi, 0, blk0 + p))

    kern = functools.partial(_na_kernel, seq=seq)
    return pl.pallas_call(
        kern,
        out_shape=jax.ShapeDtypeStruct((b, seq, W_NA), BF16),
        grid=(b, npair),
        in_specs=[zspec(ZB_NQ), zspec(ZB_NK), zspec(ZB_NV),
                  pl.BlockSpec((None, 3, 2, nq, nk), lambda bi, p: (p, 0, 0, 0, 0))],
        out_specs=pl.BlockSpec((None, seq, LANES), lambda bi, p: (bi, 0, p)),
        compiler_params=_cparams(("arbitrary", "arbitrary")),
        name="natten",
    )(z3, z3, z3, bias)


def _na_index_tables(seq):
    rows = seq // GRID_W
    ng = rows // NA_QROWS
    tabs = []
    for g in (0, 1, ng - 1):
        base = int(np.clip(g * NA_QROWS - NA_ROWS // 2, 0, rows - NA_KROWS))
        qr = g * NA_QROWS + np.arange(NA_QROWS)[:, None, None, None]
        qc = np.arange(GRID_W)[None, :, None, None]
        kr = base + np.arange(NA_KROWS)[None, None, :, None]
        kc = np.arange(GRID_W)[None, None, None, :]
        rstart = np.clip(qr - NA_ROWS // 2, 0, rows - NA_ROWS)
        cstart = np.clip(qc - NA_COLS // 2, 0, GRID_W - NA_COLS)
        ok = (kr >= rstart) & (kr < rstart + NA_ROWS) & (kc >= cstart) & (kc < cstart + NA_COLS)
        rel_r = np.clip(kr - qr + NA_ROWS - 1, 0, 2 * NA_ROWS - 2)
        rel_c = np.clip(kc - qc + NA_COLS - 1, 0, 2 * NA_COLS - 2)
        shp = (NA_QROWS * GRID_W, NA_KROWS * GRID_W)
        full = (NA_QROWS, GRID_W, NA_KROWS, GRID_W)
        tabs.append((np.broadcast_to(ok, full).reshape(shp),
                     np.broadcast_to(rel_r, full).reshape(shp),
                     np.broadcast_to(rel_c, full).reshape(shp)))
    ok = np.stack([t[0] for t in tabs])
    rr = np.stack([t[1] for t in tabs])
    rc = np.stack([t[2] for t in tabs])
    return ok, rr, rc


def _na_bias(rpb, seq):
    ok, rr, rc = _na_index_tables(seq)
    h = rpb.shape[0]
    n_r, n_c = 2 * NA_ROWS - 1, 2 * NA_COLS - 1
    full = (3, NA_QROWS, GRID_W, NA_KROWS, GRID_W)
    rr5, rc5 = rr.reshape(full), rc.reshape(full)
    oh_r = (rr5[:, :, 0, :, 0][..., None] == np.arange(n_r)).astype(np.float32)
    oh_c = (rc5[0, 0, :, 0, :][..., None] == np.arange(n_c)).astype(np.float32)
    cols = jnp.einsum("hab,qkb->haqk", rpb.astype(F32), jnp.asarray(oh_c),
                      precision=lax.Precision.HIGHEST)
    bias = jnp.einsum("ciea,haqk->hciqek", jnp.asarray(oh_r), cols,
                      precision=lax.Precision.HIGHEST)
    bias = bias.reshape(h, 3, NA_QROWS * GRID_W, NA_KROWS * GRID_W)
    bias = jnp.where(ok[None], bias, NEG)
    return bias.reshape(h // 2, 2, 3, bias.shape[2], bias.shape[3]).transpose(0, 2, 1, 3, 4)


def _proj_out_kernel(ym_ref, yr_ref, yn_ref, x_ref, wo_ref, g_ref, rw_ref,
                     rb_ref, lst_ref, x2_ref, xn_ref, route_ref, cnt_ref):
    @pl.when(pl.program_id(0) == 0)
    def _():
        cnt_ref[...] = jnp.zeros_like(cnt_ref)

    y_cat = jnp.concatenate([ym_ref[...], yr_ref[...], yn_ref[...]], axis=1)
    x2 = x_ref[...] + _dot(y_cat, wo_ref[...])
    x2_ref[...] = x2
    ms = jnp.mean(x2 * x2, axis=-1, keepdims=True)
    xn = x2 * lax.rsqrt(ms + RMS_EPS) * g_ref[...]
    xn_ref[...] = xn
    xh = xn.astype(BF16)
    xl = (xn - xh.astype(F32)).astype(BF16)
    logits = (_dot(xh, rw_ref[0]) + _dot(xl, rw_ref[0]) + _dot(xh, rw_ref[1])) + rb_ref[...]
    tm = logits.shape[0]
    lane = lax.broadcasted_iota(jnp.int32, (tm, LANES), 1).astype(F32)
    glog = jnp.where(lane < N_GROUPS, logits, NEG)
    gmax = jnp.max(glog, axis=1, keepdims=True)
    grp = jnp.min(jnp.where(glog == gmax, lane, float(LANES)), axis=1, keepdims=True)
    p_grp = 1.0 / jnp.sum(jnp.exp(glog - gmax), axis=1, keepdims=True)
    lo = N_GROUPS + grp * EXPERTS_PER_GROUP
    ein = jnp.where((lane >= lo) & (lane < lo + EXPERTS_PER_GROUP), logits, NEG)
    v1 = jnp.max(ein, axis=1, keepdims=True)
    i1 = jnp.min(jnp.where(ein == v1, lane, float(LANES)), axis=1, keepdims=True)
    ein2 = jnp.where(lane == i1, NEG, ein)
    v2 = jnp.max(ein2, axis=1, keepdims=True)
    i2 = jnp.min(jnp.where(ein2 == v2, lane, float(LANES)), axis=1, keepdims=True)
    e21 = jnp.exp(v2 - v1)
    g1 = p_grp / (1.0 + e21)
    g2 = p_grp * e21 / (1.0 + e21)
    oh1 = jnp.where(lane == i1 - N_GROUPS, 1.0, 0.0)
    oh2 = jnp.where(lane == i2 - N_GROUPS, 1.0, 0.0)
    both = oh1 + oh2
    before = _dot(lst_ref[...], both.astype(BF16)) + cnt_ref[...]
    rk1 = jnp.sum(oh1 * before, axis=1, keepdims=True)
    rk2 = jnp.sum(oh2 * before, axis=1, keepdims=True)
    cnt_ref[...] += jnp.sum(both, axis=0, keepdims=True)
    vals = (i1 - N_GROUPS, i2 - N_GROUPS, g1, g2, rk1, rk2)
    route = jnp.zeros_like(logits)
    for c, val in enumerate(vals):
        route = jnp.where(lane == c, val, route)
    route_ref[...] = route


def _proj_out(ym, yr, yn, x2, wo, g, rw, rb):
    n = x2.shape[0]
    tm = PROJ_TM

    def rows(w):
        return pl.BlockSpec((tm, w), lambda i: (i, 0))

    def full(shape):
        return pl.BlockSpec(shape, lambda i: (0, 0))

    ii = np.arange(tm)
    lstrict = jnp.asarray(ii[None, :] < ii[:, None], BF16)
    return pl.pallas_call(
        _proj_out_kernel,
        out_shape=(jax.ShapeDtypeStruct((n, D_MODEL), F32),
                   jax.ShapeDtypeStruct((n, D_MODEL), F32),
                   jax.ShapeDtypeStruct((n, LANES), F32),
                   jax.ShapeDtypeStruct((1, LANES), F32)),
        grid=(n // tm,),
        in_specs=[rows(W_MLSTM), rows(W_RET), rows(W_NA), rows(D_MODEL),
                  full((D_MODEL, D_MODEL)), full((1, D_MODEL)),
                  pl.BlockSpec((2, D_MODEL, LANES), lambda i: (0, 0, 0)),
                  full((1, LANES)), full((tm, tm))],
        out_specs=(rows(D_MODEL), rows(D_MODEL), rows(LANES), full((1, LANES))),
        compiler_params=_cparams(("arbitrary",)),
        name="proj_out_router",
    )(ym, yr, yn, x2, wo, g, rw, rb, lstrict)


def _dispatch_kernel(pend_ref, cnt_ref, nused_ref, meta_hbm, xn_ref, xs_hbm,
                     meta0, meta1, zero_buf, sem_meta, sem_rows, sem_zero):
    i = pl.program_id(0)
    nsteps = pl.num_programs(0)
    slot = lax.rem(i, 2)
    n_blocks = xs_hbm.shape[0] // MOE_BLK

    metas = (meta0, meta1)

    def meta_copy(step, s):
        return pltpu.make_async_copy(meta_hbm.at[step], metas[s], sem_meta.at[s])

    def zero_copy(blk):
        start = pl.multiple_of(blk * MOE_BLK, MOE_BLK)
        return pltpu.make_async_copy(zero_buf, xs_hbm.at[pl.ds(start, MOE_BLK), :], sem_zero)

    @pl.when(i == 0)
    def _():
        meta_copy(0, 0).start()
        zero_buf[...] = jnp.zeros_like(zero_buf)

        def zstart(e, carry):
            @pl.when(cnt_ref[e] > 0)
            def _():
                zero_copy(pend_ref[e] // MOE_BLK - 1).start()
            return carry

        def zwait(e, carry):
            @pl.when(cnt_ref[e] > 0)
            def _():
                zero_copy(0).wait()
            return carry

        lax.fori_loop(0, N_EXPERTS, zstart, 0)
        lax.fori_loop(nused_ref[0], n_blocks, lambda b, c: (zero_copy(b).start(), c)[1], 0)
        lax.fori_loop(0, N_EXPERTS, zwait, 0)
        lax.fori_loop(nused_ref[0], n_blocks, lambda b, c: (zero_copy(0).wait(), c)[1], 0)

    def row_copy(r, dst):
        return pltpu.make_async_copy(xn_ref.at[pl.ds(r, 1), :], xs_hbm.at[pl.ds(dst, 1), :],
                                     sem_rows)

    for s in (0, 1):
        @pl.when(slot == s)
        def _(s=s):
            meta_copy(i, s).wait()

            @pl.when(i + 1 < nsteps)
            def _():
                meta_copy(i + 1, 1 - s).start()

            def issue(r, carry):
                for kk in range(TOP_K):
                    row_copy(r, metas[s][TOP_K * r + kk]).start(priority=kk % 2)
                return carry

            lax.fori_loop(0, COMB_TM, issue, 0, unroll=DMA_ISSUE_UNROLL)

    def drain(r, carry):
        for kk in range(TOP_K):
            row_copy(r, 0).wait()
        return carry

    lax.fori_loop(0, COMB_TM, drain, 0, unroll=DMA_ISSUE_UNROLL)


def _dispatch(pend, counts, n_used, meta, xn, n_slots):
    n = xn.shape[0]
    tm = COMB_TM
    grid_spec = pltpu.PrefetchScalarGridSpec(
        num_scalar_prefetch=3,
        grid=(n // tm,),
        in_specs=[pl.BlockSpec(memory_space=pl.ANY),
                  pl.BlockSpec((tm, D_MODEL), lambda i, *_: (i, 0))],
        out_specs=pl.BlockSpec(memory_space=pl.ANY),
        scratch_shapes=[pltpu.SMEM((TOP_K * tm,), jnp.int32),
                        pltpu.SMEM((TOP_K * tm,), jnp.int32),
                        pltpu.VMEM((MOE_BLK, D_MODEL), F32),
                        pltpu.SemaphoreType.DMA((2,)),
                        pltpu.SemaphoreType.DMA,
                        pltpu.SemaphoreType.DMA],
    )
    return pl.pallas_call(
        _dispatch_kernel,
        out_shape=jax.ShapeDtypeStruct((n_slots, D_MODEL), F32),
        grid_spec=grid_spec,
        compiler_params=_cparams(("arbitrary",)),
        name="dispatch",
    )(pend, counts, n_used, meta, xn)


def _expert_kernel(be_ref, nused_ref, xs_ref, w1_ref, w3_ref, w2_ref, y_ref, wb1, wb3, wb2):
    i = pl.program_id(0)

    @pl.when((i == 0) | (be_ref[i] != be_ref[jnp.maximum(i - 1, 0)]))
    def _():
        wb1[...] = w1_ref[...].astype(BF16)
        wb3[...] = w3_ref[...].astype(BF16)
        wb2[...] = w2_ref[...].astype(BF16)

    @pl.when(i < nused_ref[0])
    def _():
        xb = xs_ref[...].astype(BF16)
        h1 = _dot(xb, wb1[...])
        h3 = _dot(xb, wb3[...])
        hb = (h1 * _sigmoid(h1) * h3).astype(BF16)
        y_ref[...] = _dot(hb, wb2[...])

    @pl.when(i >= nused_ref[0])
    def _():
        y_ref[...] = jnp.zeros_like(y_ref)


def _experts(block_expert, n_used, xs, w1, w3, w2, layer):
    n_blocks = xs.shape[0] // MOE_BLK

    def wspec(shape):
        return pl.BlockSpec(shape, lambda i, be, nu: (layer, be[i], 0, 0))

    grid_spec = pltpu.PrefetchScalarGridSpec(
        num_scalar_prefetch=2,
        grid=(n_blocks,),
        in_specs=[pl.BlockSpec((MOE_BLK, D_MODEL),
                               lambda i, be, nu: (jnp.minimum(i, nu[0] - 1), 0)),
                  wspec((None, None, D_MODEL, EXPERT_FF)), wspec((None, None, D_MODEL, EXPERT_FF)),
                  wspec((None, None, EXPERT_FF, D_MODEL))],
        out_specs=pl.BlockSpec((MOE_BLK, D_MODEL), lambda i, be, nu: (i, 0)),
        scratch_shapes=[pltpu.VMEM((D_MODEL, EXPERT_FF), BF16),
                        pltpu.VMEM((D_MODEL, EXPERT_FF), BF16),
                        pltpu.VMEM((EXPERT_FF, D_MODEL), BF16)],
    )
    return pl.pallas_call(
        _expert_kernel,
        out_shape=jax.ShapeDtypeStruct((n_blocks * MOE_BLK, D_MODEL), F32),
        grid_spec=grid_spec,
        compiler_params=_cparams(("arbitrary",)),
        name="experts",
    )(block_expert, n_used, xs, w1, w3, w2)


def _combine_kernel(meta_hbm, yb_hbm, x_ref, route_ref, g_ref, out_ref,
                    meta0, meta1, ybuf, sem_meta, sem_rows, *, final_norm):
    i = pl.program_id(0)
    nsteps = pl.num_programs(0)
    slot = lax.rem(i, 2)
    metas = (meta0, meta1)

    def meta_copy(step, s):
        return pltpu.make_async_copy(meta_hbm.at[step], metas[s], sem_meta.at[s])

    def row_copy(src, s, kk, r):
        return pltpu.make_async_copy(yb_hbm.at[pl.ds(src, 1), :],
                                     ybuf.at[s, kk, pl.ds(r, 1), :], sem_rows.at[s])

    def issue_tile(s):
        def issue(r, carry):
            for kk in range(TOP_K):
                row_copy(metas[s][TOP_K * r + kk], s, kk, r).start(priority=kk % 2)
            return carry

        lax.fori_loop(0, COMB_TM, issue, 0, unroll=DMA_ISSUE_UNROLL)

    @pl.when(i == 0)
    def _():
        meta_copy(0, 0).start()
        meta_copy(0, 0).wait()
        issue_tile(0)

        @pl.when(nsteps > 1)
        def _():
            meta_copy(1, 1).start()

    for s in (0, 1):
        @pl.when(slot == s)
        def _(s=s):
            @pl.when(i + 1 < nsteps)
            def _():
                meta_copy(i + 1, 1 - s).wait()
                issue_tile(1 - s)

            @pl.when(i + 2 < nsteps)
            def _():
                meta_copy(i + 2, s).start()

            def drain(r, carry):
                for kk in range(TOP_K):
                    row_copy(0, s, kk, r).wait()
                return carry

            lax.fori_loop(0, COMB_TM, drain, 0, unroll=DMA_ISSUE_UNROLL)
            route = route_ref[...]
            y = x_ref[...]
            for kk in range(TOP_K):
                y = y + route[:, 2 + kk:3 + kk] * ybuf[s, kk]
            if final_norm:
                ms = jnp.mean(y * y, axis=-1, keepdims=True)
                y = y * lax.rsqrt(ms + RMS_EPS) * g_ref[...]
            out_ref[...] = y


def _combine(meta, yb, x2, route, g, final_norm):
    n = x2.shape[0]
    tm = COMB_TM
    kern = functools.partial(_combine_kernel, final_norm=final_norm)
    return pl.pallas_call(
        kern,
        out_shape=jax.ShapeDtypeStruct((n, D_MODEL), F32),
        grid=(n // tm,),
        in_specs=[pl.BlockSpec(memory_space=pl.ANY),
                  pl.BlockSpec(memory_space=pl.ANY),
                  pl.BlockSpec((tm, D_MODEL), lambda i: (i, 0)),
                  pl.BlockSpec((tm, LANES), lambda i: (i, 0)),
                  pl.BlockSpec((1, D_MODEL), lambda i: (0, 0))],
        out_specs=pl.BlockSpec((tm, D_MODEL), lambda i: (i, 0)),
        scratch_shapes=[pltpu.SMEM((TOP_K * tm,), jnp.int32),
                        pltpu.SMEM((TOP_K * tm,), jnp.int32),
                        pltpu.VMEM((2, TOP_K, tm, D_MODEL), F32),
                        pltpu.SemaphoreType.DMA((2,)),
                        pltpu.SemaphoreType.DMA((2,))],
        compiler_params=_cparams(("arbitrary",)),
        name="combine",
    )(meta, yb, x2, route, g)


def _dispatch_layout(route, counts_f, n):
    counts = counts_f[0, :N_EXPERTS].astype(jnp.int32)
    n_blocks = n * TOP_K // MOE_BLK + N_EXPERTS
    padded = (counts + MOE_BLK - 1) // MOE_BLK * MOE_BLK
    pend = jnp.cumsum(padded).astype(jnp.int32)
    pstart = pend - padded
    n_used = pend[-1] // MOE_BLK
    blk_start = jnp.arange(n_blocks, dtype=jnp.int32) * MOE_BLK
    blk_start = jnp.minimum(blk_start, (n_used - 1) * MOE_BLK)
    block_expert = jnp.sum((pend[None, :] <= blk_start[:, None]).astype(jnp.int32), axis=1)
    block_expert = jnp.clip(block_expert, 0, N_EXPERTS - 1).astype(jnp.int32)
    expert = route[:, 0:TOP_K].astype(jnp.int32)
    rank = route[:, 4:4 + TOP_K].astype(jnp.int32)
    seg = jnp.sum(jnp.where(expert[..., None] == jnp.arange(N_EXPERTS, dtype=jnp.int32),
                            pstart, 0), axis=-1)
    meta = (seg + rank).reshape(n // COMB_TM, TOP_K * COMB_TM)
    return block_expert, n_used.reshape(1), pend, counts, meta, n_blocks * MOE_BLK


def _const_tables():
    lane = np.arange(LANES)
    grp = lane // HEAD_DIM
    xc = np.zeros((2, 24, 8 * LANES), np.float32)
    for kk in range(4):
        xc[:, kk, kk * LANES:(kk + 1) * LANES] = 1.0
    for di in range(2):
        for j in range(2):
            for blk, base in ((4 + di, 8), (6 + di, 16)):
                lo = blk * LANES + j * HEAD_DIM
                xc[:, base + 2 * di + j, lo:lo + HEAD_DIM] = 1.0
    ii = np.arange(CHUNK)
    incl_f = (ii[:, None] <= ii[None, :]).astype(np.float32)
    incl_b = (ii[:, None] >= ii[None, :]).astype(np.float32)
    bd = (grp[:, None] == grp[None, :]).astype(np.float32)
    perm = np.zeros((LANES, LANES), np.float32)
    half = HEAD_DIM // 2
    src = (lane // HEAD_DIM) * HEAD_DIM + (lane % HEAD_DIM + half) % HEAD_DIM
    perm[src, lane] = 1.0
    sgn = np.where(lane % HEAD_DIM < half, -1.0, 1.0).astype(np.float32)[None, :]
    return {
        "xc": jnp.asarray(xc.reshape(48, 8 * LANES), BF16),
        "trif": jnp.asarray(np.concatenate([incl_f, incl_f], 0), BF16),
        "trib": jnp.asarray(np.concatenate([incl_b, incl_b], 0), BF16),
        "bdm": jnp.asarray(bd, F32), "bdm2": jnp.asarray(np.concatenate([bd, bd], 1), F32),
        "bd2": jnp.asarray(np.concatenate([bd, bd], 0), BF16),
        "perm": jnp.asarray(perm, BF16), "sgn": jnp.asarray(sgn, F32),
    }


def _rope_tables(seq):
    half = HEAD_DIM // 2
    nc = seq // CHUNK
    inv_freq = ROPE_BASE ** (-np.arange(half, dtype=np.float64) / half)
    freq = inv_freq[np.arange(LANES) % half]
    ang_a = np.arange(CHUNK, dtype=np.float64)[:, None] * freq[None, :]
    ang_b = (np.arange(nc, dtype=np.float64) * CHUNK)[:, None] * freq[None, :]
    return {"ca": jnp.asarray(np.cos(ang_a), F32), "sa": jnp.asarray(np.sin(ang_a), F32),
            "cb": jnp.asarray(np.cos(ang_b)[:, None, :], F32),
            "sb": jnp.asarray(np.sin(ang_b)[:, None, :], F32)}


def _layer_weights(l, w_in, mlstm_gate_bias, ret_decay, w_out, router_group_w, router_group_b,
                   router_expert_w, router_expert_b):
    sizes = (W_MLSTM,) * 4 + (4 * H_MLSTM,) + (W_RET,) * 4 + (W_NA,) * 3
    offs = np.concatenate([[0], np.cumsum(sizes)])
    col = lambda i: w_in[l][:, int(offs[i]):int(offs[i + 1])]
    mq, mk, mv, mo, mg, rq, rk, rv, rg, nq, nk, nv = [col(i) for i in range(12)]
    scale = HEAD_DIM ** -0.5
    w_all = jnp.concatenate([mq, mk, mv, mo, rq, rk * scale, rv, rg, nq * scale, nk, nv],
                            axis=1).astype(BF16)
    sel = np.zeros((GATE_ROWS, 4 * H_MLSTM), np.float32)
    for kind in range(2):
        for p in range(H_MLSTM // 2):
            for k in range(4):
                sel[(kind * (H_MLSTM // 2) + p) * 8 + k,
                    (2 * kind + k // 2) * H_MLSTM + 2 * p + k % 2] = 1.0
    sel = jnp.asarray(sel)
    wg = jnp.dot(sel, mg.T, precision=lax.Precision.HIGHEST).astype(BF16)
    gbias = jnp.dot(sel, mlstm_gate_bias[l].astype(F32).reshape(-1, 1),
                    precision=lax.Precision.HIGHEST)
    rd = ret_decay[l].astype(F32)
    rd = jnp.repeat(rd.reshape(2, H_RET // 2, 2), HEAD_DIM, axis=2)
    rd = rd.transpose(1, 0, 2)[:, :, None, :]
    wo = w_out[l].astype(BF16)
    rw = jnp.concatenate([router_group_w[l], router_expert_w[l],
                          jnp.zeros((D_MODEL, LANES - N_GROUPS - N_EXPERTS), F32)], axis=1)
    rb = jnp.concatenate([router_group_b[l].astype(F32), router_expert_b[l].astype(F32),
                          jnp.zeros((LANES - N_GROUPS - N_EXPERTS,), F32)])[None, :]
    rw_hi = rw.astype(BF16)
    rw2 = jnp.stack([rw_hi, (rw - rw_hi.astype(F32)).astype(BF16)])
    return w_all, wg, gbias, rd, wo, rw2, rb


def _encoder(x, consts, norm_mix, w_in, mlstm_conv_w, mlstm_conv_b, mlstm_gate_bias, mlstm_norm,
             ret_decay, ret_norm, na_rpb, w_out, norm_ffn, router_group_w, router_group_b,
             router_expert_w, router_expert_b, expert_w1, expert_w3, expert_w2, norm_final):
    b, seq, _ = x.shape
    n = b * seq
    depth = w_in.shape[0]
    rope = _rope_tables(seq)
    x2 = x.reshape(n, D_MODEL).astype(F32)
    for l in range(depth):
        w_all, wg, gbias, rd, wo, rw, rb = _layer_weights(
            l, w_in, mlstm_gate_bias, ret_decay, w_out, router_group_w, router_group_b,
            router_expert_w, router_expert_b)
        z, gates_t = _proj_in(x2, norm_mix[l].astype(F32)[None, :], w_all, wg)
        z3 = z.reshape(b, seq, Z_WIDTH)
        y_m = _mlstm(z3, gates_t, gbias, mlstm_conv_w[l].astype(F32),
                     mlstm_conv_b[l].astype(F32)[None, :], mlstm_norm[l].astype(F32)[None, :], consts)
        y_r = _retention(z3, rd, ret_norm[l].astype(F32)[None, :], consts, rope)
        y_n = _na(z3, _na_bias(na_rpb[l], seq))
        x2, xn, route, counts_f = _proj_out(y_m.reshape(n, W_MLSTM), y_r.reshape(n, W_RET),
                                            y_n.reshape(n, W_NA), x2, wo,
                                            norm_ffn[l].astype(F32)[None, :], rw, rb)
        block_expert, n_used, pend, counts, meta, n_slots = _dispatch_layout(route, counts_f, n)
        xs = _dispatch(pend, counts, n_used, meta, xn, n_slots)
        yb = _experts(block_expert, n_used, xs, expert_w1.astype(F32), expert_w3.astype(F32),
                      expert_w2.astype(F32), l)
        x2 = _combine(meta, yb, x2, route, norm_final.astype(F32)[None, :],
                      final_norm=(l == depth - 1))
    return x2.reshape(b, seq, D_MODEL)


def kernel(x_prompt, x_sample, norm_mix, w_in, mlstm_conv_w, mlstm_conv_b, mlstm_gate_bias,
           mlstm_norm, ret_decay, ret_norm, na_rpb, w_out, norm_ffn, router_group_w,
           router_group_b, router_expert_w, router_expert_b, expert_w1, expert_w3, expert_w2,
           norm_final):
    consts = _const_tables()
    weights = (norm_mix, w_in, mlstm_conv_w, mlstm_conv_b, mlstm_gate_bias, mlstm_norm,
               ret_decay, ret_norm, na_rpb, w_out, norm_ffn, router_group_w, router_group_b,
               router_expert_w, router_expert_b, expert_w1, expert_w3, expert_w2, norm_final)
    return (_encoder(x_prompt, consts, *weights), _encoder(x_sample, consts, *weights))
```
